```python
import jax, jax.numpy as jnp
from jax import lax
import numpy as np

D_MODEL = 1024
BATCH = 2
SEQ = 16384
DEPTH = 4

HEAD_DIM = 64
ROT_DIM = HEAD_DIM // 4
ROPE_THETA = 500000.0
A_HEADS = D_MODEL // HEAD_DIM
A_KV_HEADS = 4
A_GROUP = A_HEADS // A_KV_HEADS
CMP_BLOCK = 32
CMP_STRIDE = 16
CMP_HID = 256
SLC_BLOCK = 64
N_SEL = 16
NSA_WIN = 512
Q_BLOCK = 128
A_Q = A_HEADS * HEAD_DIM
A_KV = A_KV_HEADS * HEAD_DIM
A_SPLITS = tuple(A_Q + i * A_KV for i in range(7))
A_IN = A_Q + 6 * A_KV + 3 * A_HEADS
B_HEADS = D_MODEL // HEAD_DIM
B_KV_HEADS = 2
B_GROUP = B_HEADS // B_KV_HEADS
SWA_WIN = 128
SWA_BLOCK = 128
B_Q = B_HEADS * HEAD_DIM
B_KV = B_KV_HEADS * HEAD_DIM
D_FF = 2816
N_EXPERTS = 8
TOP_K = 2
D_FF_EXPERT = 3584
MOE_BLOCK = 128
N_A_LAYERS = DEPTH // 2
N_B_LAYERS = DEPTH - N_A_LAYERS
N_DENSE = (DEPTH + 1) // 2
N_MOE = DEPTH // 2
ALPHA = (2 * DEPTH) ** 0.25
BETA = (8 * DEPTH) ** -0.25
LN_EPS = 1e-5
NEG = -1e30
FORCE = 1e9

kernel_name = "yoco_nsa_swa_sink_moe_deepnorm"


def layer_norm(x, g, b):
    xf = x.astype(jnp.float32)
    mu = xf.mean(-1, keepdims=True)
    var = jnp.square(xf - mu).mean(-1, keepdims=True)
    return ((xf - mu) * lax.rsqrt(var + LN_EPS) * g + b).astype(x.dtype)


def rope_tables(positions):
    inv = ROPE_THETA ** (-jnp.arange(0, ROT_DIM, 2, dtype=jnp.float32) / ROT_DIM)
    ang = positions.astype(jnp.float32)[..., None] * inv
    return jnp.cos(ang), jnp.sin(ang)


def apply_partial_rope(x, cos, sin):
    xr = x[..., :ROT_DIM].astype(jnp.float32)
    x1, x2 = xr[..., :ROT_DIM // 2], xr[..., ROT_DIM // 2:]
    c, s = cos[:, :, None, :], sin[:, :, None, :]
    rot = jnp.concatenate([x1 * c - x2 * s, x2 * c + x1 * s], axis=-1)
    return jnp.concatenate([rot.astype(x.dtype), x[..., ROT_DIM:]], axis=-1)


def masked_softmax(s, mask):
    p = jax.nn.softmax(jnp.where(mask, s, NEG), axis=-1)
    return jnp.where(mask, p, 0.0)


gather_blocks = jax.vmap(jax.vmap(lambda blocks, ix: blocks[ix]))


def compress_blocks(t, pos, w1, w2):
    bsz, seq, g, dh = t.shape
    r = CMP_BLOCK // CMP_STRIDE
    chunks = t.reshape(bsz, seq // CMP_STRIDE, CMP_STRIDE, g, dh)
    n_cmp = seq // CMP_STRIDE - r + 1
    blocks = jnp.concatenate([chunks[:, i:i + n_cmp] for i in range(r)], axis=2)
    blocks = blocks + pos[:, None, :]
    flat = blocks.transpose(0, 1, 3, 2, 4).reshape(bsz, n_cmp, g, CMP_BLOCK * dh)
    return (jax.nn.gelu(flat @ w1) @ w2).transpose(0, 2, 1, 3)


def nsa_mixer(h, cos, sin, w_in, w_out, cmp_pos, cmp_w1, cmp_w2):
    bsz, seq, _ = h.shape
    G, R, dh = A_KV_HEADS, A_GROUP, HEAD_DIM
    q, kc, vc, ks, vs, kw, vw, gl = jnp.split(h @ w_in, A_SPLITS, axis=-1)
    q = q.reshape(bsz, seq, A_HEADS, dh)
    kc, vc, ks, vs, kw, vw = (t.reshape(bsz, seq, G, dh) for t in (kc, vc, ks, vs, kw, vw))
    q_rot = apply_partial_rope(q, cos, sin)
    ks = apply_partial_rope(ks, cos, sin)
    kw = apply_partial_rope(kw, cos, sin)
    kc = compress_blocks(kc, cmp_pos[0], cmp_w1[0], cmp_w2[0])
    vc = compress_blocks(vc, cmp_pos[1], cmp_w1[1], cmp_w2[1])
    n_cmp = kc.shape[2]
    n_slc = seq // SLC_BLOCK
    n_sel = min(N_SEL, n_slc)

    def to_heads(t):
        return t.reshape(bsz, seq, G, R, dh).transpose(0, 2, 3, 1, 4)

    def to_blocks(t):
        return t.reshape(bsz, n_slc, SLC_BLOCK, G, dh).transpose(0, 3, 1, 2, 4)

    def to_window(t):
        return jnp.pad(t.transpose(0, 2, 1, 3), ((0, 0), (0, 0), (NSA_WIN, 0), (0, 0)))

    qg, qg_rot = to_heads(q), to_heads(q_rot)
    ks_b, vs_b = to_blocks(ks), to_blocks(vs)
    kw_p, vw_p = to_window(kw), to_window(vw)
    gates = jax.nn.sigmoid(gl.astype(jnp.float32)).reshape(bsz, seq, 3, G, R).transpose(0, 3, 4, 1, 2)
    cmp_end = jnp.arange(n_cmp) * CMP_STRIDE + CMP_BLOCK - 1
    n_idx = jnp.arange(n_cmp)[:, None]
    j_idx = jnp.arange(n_slc)[None, :]
    overlap = (jnp.minimum(n_idx * CMP_STRIDE + CMP_BLOCK, j_idx * SLC_BLOCK + SLC_BLOCK)
               - jnp.maximum(n_idx * CMP_STRIDE, j_idx * SLC_BLOCK))
    cmp_to_slc = jnp.clip(overlap, 0).astype(jnp.float32) / CMP_BLOCK
    jj = jnp.arange(n_slc)
    scale = dh ** -0.5

    def query_block(i):
        q0 = i * Q_BLOCK
        t = q0 + jnp.arange(Q_BLOCK)
        qb = lax.dynamic_slice_in_dim(qg, q0, Q_BLOCK, axis=3)
        qbr = lax.dynamic_slice_in_dim(qg_rot, q0, Q_BLOCK, axis=3)
        gb = lax.dynamic_slice_in_dim(gates, q0, Q_BLOCK, axis=3)
        s_c = jnp.einsum('bgrtd,bgnd->bgrtn', qb, kc).astype(jnp.float32) * scale
        p_c = masked_softmax(s_c, cmp_end[None, :] <= t[:, None])
        o_c = jnp.einsum('bgrtn,bgnd->bgrtd', p_c.astype(vc.dtype), vc)
        imp = jnp.einsum('bgtn,nj->bgtj', p_c.sum(2), cmp_to_slc)
        cur = t // SLC_BLOCK
        forced = (jj[None, :] == 0) | (jj[None, :] == cur[:, None]) | (jj[None, :] == cur[:, None] - 1)
        score = jnp.where(jj[None, :] > cur[:, None], -1.0, jnp.where(forced, FORCE, imp))
        _, idx = lax.top_k(score, n_sel)
        k_sel = gather_blocks(ks_b, idx).reshape(bsz, G, Q_BLOCK, n_sel * SLC_BLOCK, dh)
        v_sel = gather_blocks(vs_b, idx).reshape(bsz, G, Q_BLOCK, n_sel * SLC_BLOCK, dh)
        sel_pos = (idx[..., None] * SLC_BLOCK + jnp.arange(SLC_BLOCK)).reshape(bsz, G, Q_BLOCK, -1)
        m_s = (sel_pos <= t[:, None])[:, :, None]
        s_s = jnp.einsum('bgrtd,bgtkd->bgrtk', qbr, k_sel).astype(jnp.float32) * scale
        p_s = masked_softmax(s_s, m_s)
        o_s = jnp.einsum('bgrtk,bgtkd->bgrtd', p_s.astype(v_sel.dtype), v_sel)
        kwb = lax.dynamic_slice_in_dim(kw_p, q0, Q_BLOCK + NSA_WIN, axis=2)
        vwb = lax.dynamic_slice_in_dim(vw_p, q0, Q_BLOCK + NSA_WIN, axis=2)
        spos = q0 - NSA_WIN + jnp.arange(Q_BLOCK + NSA_WIN)
        diff = t[:, None] - spos[None, :]
        m_w = (diff >= 0) & (diff < NSA_WIN) & (spos[None, :] >= 0)
        s_w = jnp.einsum('bgrtd,bgkd->bgrtk', qbr, kwb).astype(jnp.float32) * scale
        p_w = masked_softmax(s_w, m_w)
        o_w = jnp.einsum('bgrtk,bgkd->bgrtd', p_w.astype(vwb.dtype), vwb)
        o = gb[..., 0:1] * o_c + gb[..., 1:2] * o_s + gb[..., 2:3] * o_w
        return o.astype(h.dtype)

    out = lax.map(query_block, jnp.arange(seq // Q_BLOCK))
    out = out.transpose(1, 0, 4, 2, 3, 5).reshape(bsz, seq, A_Q)
    return out @ w_out


def shared_kv(h, cos, sin, w_kv):
    bsz, seq, _ = h.shape
    nb = seq // SWA_BLOCK
    kv = h @ w_kv
    k = apply_partial_rope(kv[..., :B_KV].reshape(bsz, seq, B_KV_HEADS, HEAD_DIM), cos, sin)
    v = kv[..., B_KV:].reshape(bsz, seq, B_KV_HEADS, HEAD_DIM)

    def band(t):
        tb = t.reshape(bsz, nb, SWA_BLOCK, B_KV_HEADS, HEAD_DIM)
        prev = jnp.concatenate([jnp.zeros_like(tb[:, :1]), tb[:, :-1]], axis=1)
        return jnp.concatenate([prev, tb], axis=2).transpose(1, 0, 2, 3, 4)

    return band(k), band(v)


def swa_sink_mixer(h, cos, sin, k_band, v_band, w_q, w_out, sinks):
    bsz, seq, _ = h.shape
    G, R, dh = B_KV_HEADS, B_GROUP, HEAD_DIM
    nb = seq // SWA_BLOCK
    q = apply_partial_rope((h @ w_q).reshape(bsz, seq, B_HEADS, dh), cos, sin)
    q = q.reshape(bsz, nb, SWA_BLOCK, G, R, dh).transpose(1, 0, 2, 3, 4, 5)
    sink = sinks.astype(jnp.float32).reshape(1, G, R, 1, 1)
    ii = jnp.arange(SWA_BLOCK)[:, None]
    jj = jnp.arange(2 * SWA_BLOCK)[None, :]
    diff = SWA_BLOCK + ii - jj
    band = (diff >= 0) & (diff < SWA_WIN)
    scale = dh ** -0.5

    def block(args):
        qb, kb, vb, n = args
        s = jnp.einsum('bigrd,bjgd->bgrij', qb, kb).astype(jnp.float32) * scale
        mask = band & ((n > 0) | (jj >= SWA_BLOCK))
        s = jnp.where(mask, s, NEG)
        m = jnp.maximum(s.max(-1, keepdims=True), sink)
        e = jnp.exp(s - m)
        p = e / (e.sum(-1, keepdims=True) + jnp.exp(sink - m))
        return jnp.einsum('bgrij,bjgd->bigrd', p.astype(vb.dtype), vb)

    out = lax.map(block, (q, k_band, v_band, jnp.arange(nb)))
    out = out.transpose(1, 0, 2, 3, 4, 5).reshape(bsz, seq, B_Q)
    return out @ w_out


def dense_swiglu(h, w_in, w_out):
    g, u = jnp.split(h @ w_in, 2, axis=-1)
    return (jax.nn.silu(g) * u) @ w_out


def moe_swiglu(h, w_router, w_in, w_out):
    bsz, seq, d = h.shape
    n_tok = bsz * seq
    n_asg = n_tok * TOP_K
    xt = h.reshape(n_tok, d)
    logits = (xt @ w_router).astype(jnp.float32)
    top_logit, top_e = lax.top_k(logits, TOP_K)
    gate = jax.nn.softmax(top_logit, axis=-1)
    e_flat = top_e.reshape(-1)
    tok_flat = jnp.repeat(jnp.arange(n_tok, dtype=jnp.int32), TOP_K)
    order = jnp.argsort(e_flat)
    e_s, tok_s, g_s = e_flat[order], tok_flat[order], gate.reshape(-1)[order]
    counts = jnp.bincount(e_flat, length=N_EXPERTS)
    padded = (counts + MOE_BLOCK - 1) // MOE_BLOCK * MOE_BLOCK
    start = jnp.cumsum(counts) - counts
    pad_end = jnp.cumsum(padded)
    pad_start = pad_end - padded
    dest = pad_start[e_s] + jnp.arange(n_asg) - start[e_s]
    cap = n_asg + N_EXPERTS * MOE_BLOCK
    cap = (cap + MOE_BLOCK - 1) // MOE_BLOCK * MOE_BLOCK
    n_blk = cap // MOE_BLOCK
    buf_tok = jnp.zeros((cap,), jnp.int32).at[dest].set(tok_s)
    buf_w = jnp.zeros((cap,), jnp.float32).at[dest].set(g_s)
    blk_e = jnp.clip(jnp.searchsorted(pad_end, jnp.arange(n_blk) * MOE_BLOCK, side='right'), 0, N_EXPERTS - 1)
    xb = xt[buf_tok].reshape(n_blk, MOE_BLOCK, d)

    def expert_block(args):
        xblk, e = args
        g, u = jnp.split(xblk @ w_in[e], 2, axis=-1)
        return (jax.nn.silu(g) * u) @ w_out[e]

    yb = lax.map(expert_block, (xb, blk_e)).reshape(cap, d)
    yb = (yb.astype(jnp.float32) * buf_w[:, None]).astype(h.dtype)
    out = jnp.zeros((n_tok, d), h.dtype).at[buf_tok].add(yb)
    return out.reshape(bsz, seq, d)


def setup_inputs(seed: int = 0) -> dict:
    key = jax.random.key(seed)
    ks = jax.random.split(key, 20)
    f32 = jnp.float32
    nrm = lambda k, shape, s: jax.random.normal(k, shape, f32) * s
    return {
        "x": nrm(ks[0], (BATCH, SEQ, D_MODEL), 1.0),
        "positions": jnp.arange(SEQ, dtype=jnp.int32)[None, :] + jax.random.randint(ks[1], (BATCH, 1), 0, 4096, dtype=jnp.int32),
        "w_in_a": nrm(ks[2], (N_A_LAYERS, D_MODEL, A_IN), D_MODEL ** -0.5),
        "w_out_a": nrm(ks[3], (N_A_LAYERS, A_Q, D_MODEL), BETA * A_Q ** -0.5),
        "cmp_pos": nrm(ks[4], (N_A_LAYERS, 2, CMP_BLOCK, HEAD_DIM), 0.1),
        "cmp_w1": nrm(ks[5], (N_A_LAYERS, 2, CMP_BLOCK * HEAD_DIM, CMP_HID), (CMP_BLOCK * HEAD_DIM) ** -0.5),
        "cmp_w2": nrm(ks[6], (N_A_LAYERS, 2, CMP_HID, HEAD_DIM), CMP_HID ** -0.5),
        "w_kv_shared": nrm(ks[7], (D_MODEL, 2 * B_KV), D_MODEL ** -0.5),
        "w_q_b": nrm(ks[8], (N_B_LAYERS, D_MODEL, B_Q), D_MODEL ** -0.5),
        "w_out_b": nrm(ks[9], (N_B_LAYERS, B_Q, D_MODEL), BETA * B_Q ** -0.5),
        "sinks_b": nrm(ks[10], (N_B_LAYERS, B_HEADS), 0.5),
        "ln_g": 1.0 + nrm(ks[11], (DEPTH, 2, D_MODEL), 0.02),
        "ln_b": nrm(ks[12], (DEPTH, 2, D_MODEL), 0.02),
        "dense_w_in": nrm(ks[13], (N_DENSE, D_MODEL, 2 * D_FF), D_MODEL ** -0.5),
        "dense_w_out": nrm(ks[14], (N_DENSE, D_FF, D_MODEL), BETA * D_FF ** -0.5),
        "moe_router": nrm(ks[15], (N_MOE, D_MODEL, N_EXPERTS), D_MODEL ** -0.5),
        "moe_w_in": nrm(ks[16], (N_MOE, N_EXPERTS, D_MODEL, 2 * D_FF_EXPERT), D_MODEL ** -0.5),
        "moe_w_out": nrm(ks[17], (N_MOE, N_EXPERTS, D_FF_EXPERT, D_MODEL), BETA * D_FF_EXPERT ** -0.5),
    }


def reference(x, positions, w_in_a, w_out_a, cmp_pos, cmp_w1, cmp_w2, w_kv_shared, w_q_b, w_out_b,
              sinks_b, ln_g, ln_b, dense_w_in, dense_w_out, moe_router, moe_w_in, moe_w_out):
    cos, sin = rope_tables(positions)
    h = x
    k_band = v_band = None
    for l in range(DEPTH):
        if l < N_A_LAYERS:
            mix = nsa_mixer(h, cos, sin, w_in_a[l], w_out_a[l], cmp_pos[l], cmp_w1[l], cmp_w2[l])
        else:
            b = l - N_A_LAYERS
            mix = swa_sink_mixer(h, cos, sin, k_band, v_band, w_q_b[b], w_out_b[b], sinks_b[b])
        h = layer_norm(ALPHA * h + mix, ln_g[l, 0], ln_b[l, 0])
        if l % 2 == 0:
            ffn = dense_swiglu(h, dense_w_in[l // 2], dense_w_out[l // 2])
        else:
            ffn = moe_swiglu(h, moe_router[l // 2], moe_w_in[l // 2], moe_w_out[l // 2])
        h = layer_norm(ALPHA * h + ffn, ln_g[l, 1], ln_b[l, 1])
        if l == N_A_LAYERS - 1:
            k_band, v_band = shared_kv(h, cos, sin, w_kv_shared)
    return h
```

```python
import functools

import numpy as np
import jax
import jax.numpy as jnp
from jax import lax
from jax.experimental import pallas as pl
from jax.experimental.pallas import tpu as pltpu

F32 = jnp.float32
BF16 = jnp.bfloat16

D_MODEL = 1024
DEPTH = 4
HEAD_DIM = 64
LANES = 128
ROT_DIM = HEAD_DIM // 4
ROPE_THETA = 500000.0
A_HEADS = 16
A_KV_HEADS = 4
A_GROUP = 4
CMP_BLOCK = 32
CMP_STRIDE = 16
CMP_HID = 256
SLC_BLOCK = 64
N_SEL = 16
NSA_WIN = 512
B_HEADS = 16
B_KV_HEADS = 2
B_GROUP = 8
SWA_WIN = 128
D_FF = 2816
N_EXPERTS = 8
D_FF_EXPERT = 3584
ALPHA = (2 * DEPTH) ** 0.25
LN_EPS = 1e-5
NEG = -1e30
FORCE = 1e9

TQ = 128
KT = 512
BLK_PER_KT = KT // SLC_BLOCK
MOE_TM = 512
VMEM_LIMIT = 56 * 1024 * 1024


def _cp(*sem):
    return pltpu.CompilerParams(dimension_semantics=sem, vmem_limit_bytes=VMEM_LIMIT)


def _dot(a, b):
    return jnp.dot(a, b, preferred_element_type=F32)


def _dot_nt(a, b):
    return lax.dot_general(a, b, (((1,), (1,)), ((), ())), preferred_element_type=F32)


def _sigmoid(x):
    return 1.0 / (1.0 + jnp.exp(-x))


def _layer_norm(z, g, b):
    mu = jnp.mean(z, axis=-1, keepdims=True)
    zc = z - mu
    var = jnp.mean(zc * zc, axis=-1, keepdims=True)
    return zc * lax.rsqrt(var + LN_EPS) * g + b


def _rope(x, c, sa, sb):
    w = x.shape[1]
    return x * c + pltpu.roll(x, w - ROT_DIM // 2, 1) * sa + pltpu.roll(x, ROT_DIM // 2, 1) * sb


def _tile_lanes(t, n):
    return t if n == 1 else jnp.concatenate([t] * n, axis=1)


def _tile_rows(t, n):
    return t if n == 1 else jnp.concatenate([t] * n, axis=0)


def _mm_kernel(x_ref, w_ref, o_ref, *, act):
    acc = _dot(x_ref[...].astype(BF16), w_ref[...])
    if act == "sigmoid":
        acc = _sigmoid(acc)
    o_ref[...] = acc.astype(o_ref.dtype)


def _mm(x, w, out_dtype, act=None, tm=512):
    m, k = x.shape
    n = w.shape[1]
    tm = min(tm, m)
    return pl.pallas_call(
        functools.partial(_mm_kernel, act=act),
        grid=(m // tm,),
        in_specs=[pl.BlockSpec((tm, k), lambda i: (i, 0)), pl.BlockSpec((k, n), lambda i: (0, 0))],
        out_specs=pl.BlockSpec((tm, n), lambda i: (i, 0)),
        out_shape=jax.ShapeDtypeStruct((m, n), out_dtype),
        compiler_params=_cp("parallel"),
        name="mm",
    )(x, w)


def _proj_kernel(x_ref, w_ref, c_ref, sa_ref, sb_ref, o_ref, *, modes, tm, tn):
    i = pl.program_id(0)
    j = pl.program_id(1)
    acc = _dot(x_ref[...].astype(BF16), w_ref[...])
    ns = tn // LANES
    lane = lax.broadcasted_iota(jnp.int32, (tm, tn), 1) % LANES

    def roped():
        return _rope(acc, _tile_lanes(c_ref[...], ns), _tile_lanes(sa_ref[...], ns), _tile_lanes(sb_ref[...], ns))

    for mode in sorted(set(modes)):
        pred = functools.reduce(jnp.logical_or, [j == jj for jj, mm in enumerate(modes) if mm == mode])

        @pl.when(pred)
        def _(mode=mode):
            if mode == "plain":
                y = acc
            elif mode == "rope":
                y = roped()
            elif mode == "rope_onehot":
                row = i * tm + lax.broadcasted_iota(jnp.int32, (tm, tn), 0)
                blk = (row // SLC_BLOCK) % BLK_PER_KT
                y = roped() + jnp.where(lane == HEAD_DIM + blk, 1.0, 0.0)
            elif mode == "ones":
                y = acc + jnp.where(lane == HEAD_DIM, 1.0, 0.0)
            o_ref[...] = y.astype(o_ref.dtype)


def _proj(x, w, tabs, modes, tn, tm=512):
    m, k = x.shape
    n = w.shape[1]
    c, sa, sb = tabs
    return pl.pallas_call(
        functools.partial(_proj_kernel, modes=tuple(modes), tm=tm, tn=tn),
        grid=(m // tm, n // tn),
        in_specs=[
            pl.BlockSpec((tm, k), lambda i, j: (i, 0)),
            pl.BlockSpec((k, tn), lambda i, j: (0, j)),
            pl.BlockSpec((tm, LANES), lambda i, j: (i, 0)),
            pl.BlockSpec((tm, LANES), lambda i, j: (i, 0)),
            pl.BlockSpec((tm, LANES), lambda i, j: (i, 0)),
        ],
        out_specs=pl.BlockSpec((tm, tn), lambda i, j: (i, j)),
        out_shape=jax.ShapeDtypeStruct((m, n), BF16),
        compiler_params=_cp("parallel", "arbitrary"),
        name="proj",
    )(x, w, c, sa, sb)


def _out_ln_kernel(x_ref, w_ref, h_ref, g_ref, b_ref, o_ref):
    mix = _dot(x_ref[...].astype(BF16), w_ref[...])
    o_ref[...] = _layer_norm(ALPHA * h_ref[...] + mix, g_ref[...], b_ref[...])


def _out_ln(x, w, h, g, b, tm=512):
    m, k = x.shape
    d = w.shape[1]
    return pl.pallas_call(
        _out_ln_kernel,
        grid=(m // tm,),
        in_specs=[
            pl.BlockSpec((tm, k), lambda i: (i, 0)),
            pl.BlockSpec((k, d), lambda i: (0, 0)),
            pl.BlockSpec((tm, d), lambda i: (i, 0)),
            pl.BlockSpec((1, d), lambda i: (0, 0)),
            pl.BlockSpec((1, d), lambda i: (0, 0)),
        ],
        out_specs=pl.BlockSpec((tm, d), lambda i: (i, 0)),
        out_shape=jax.ShapeDtypeStruct((m, d), F32),
        compiler_params=_cp("parallel"),
        name="out_ln",
    )(x, w, h, g, b)


def _ffn_kernel(x_ref, wg_ref, wu_ref, wo_ref, g_ref, b_ref, o_ref, acc_ref):
    c = pl.program_id(1)

    @pl.when(c == 0)
    def _():
        acc_ref[...] = jnp.zeros_like(acc_ref)

    xb = x_ref[...].astype(BF16)
    gate = _dot(xb, wg_ref[...])
    up = _dot(xb, wu_ref[...])
    a = gate * _sigmoid(gate) * up
    acc_ref[...] += _dot(a.astype(BF16), wo_ref[...])

    @pl.when(c == pl.num_programs(1) - 1)
    def _():
        o_ref[...] = _layer_norm(ALPHA * x_ref[...] + acc_ref[...], g_ref[...], b_ref[...])


def _ffn_dense(h, w_in, w_out, g, b, tm=512, fc=1408):
    m, d = h.shape
    ff = w_out.shape[0]
    nc = ff // fc
    return pl.pallas_call(
        _ffn_kernel,
        grid=(m // tm, nc),
        in_specs=[
            pl.BlockSpec((tm, d), lambda i, c: (i, 0)),
            pl.BlockSpec((d, fc), lambda i, c: (0, c)),
            pl.BlockSpec((d, fc), lambda i, c: (0, nc + c)),
            pl.BlockSpec((fc, d), lambda i, c: (c, 0)),
            pl.BlockSpec((1, d), lambda i, c: (0, 0)),
            pl.BlockSpec((1, d), lambda i, c: (0, 0)),
        ],
        out_specs=pl.BlockSpec((tm, d), lambda i, c: (i, 0)),
        out_shape=jax.ShapeDtypeStruct((m, d), F32),
        scratch_shapes=[pltpu.VMEM((tm, d), F32)],
        compiler_params=_cp("parallel", "arbitrary"),
        name="ffn_dense",
    )(h, w_in, w_in, w_out, g, b)


def _compress_kernel(p_ref, q_ref, pos_ref, w1_ref, w2_ref, o_ref):
    posb = _dot(pos_ref[...], w1_ref[...])[0:1, :]
    hid = p_ref[...] + q_ref[...] + posb
    act = jax.nn.gelu(hid, approximate=True)
    o_ref[...] = _dot(act.astype(BF16), w2_ref[...])


def _compress(pp, qq, pos8, w1, w2, tm=1024):
    m, hid = pp.shape
    dh = w2.shape[1]
    return pl.pallas_call(
        _compress_kernel,
        grid=(m // tm,),
        in_specs=[
            pl.BlockSpec((tm, hid), lambda i: (i, 0)),
            pl.BlockSpec((tm, hid), lambda i: (i, 0)),
            pl.BlockSpec(pos8.shape, lambda i: (0, 0)),
            pl.BlockSpec(w1.shape, lambda i: (0, 0)),
            pl.BlockSpec(w2.shape, lambda i: (0, 0)),
        ],
        out_specs=pl.BlockSpec((tm, dh), lambda i: (i, 0)),
        out_shape=jax.ShapeDtypeStruct((m, dh), F32),
        compiler_params=_cp("parallel"),
        name="compress",
    )(pp, qq, pos8, w1, w2)


def _cmp_select_kernel(q_ref, kc_ref, vc_ref, map_ref, oc_ref, sb_ref, *, ncp, nslc):
    qi = pl.program_id(2)
    t0 = qi * TQ
    rows = A_GROUP * TQ
    qb = q_ref[...]
    q4 = jnp.concatenate([qb[:, r * LANES:(r + 1) * LANES] for r in range(A_GROUP)], axis=0)
    s = _dot_nt(q4, kc_ref[...])
    t_row = t0 + lax.broadcasted_iota(jnp.int32, (rows, ncp), 0) % TQ
    n_idx = lax.broadcasted_iota(jnp.int32, (rows, ncp), 1)
    valid = (n_idx * CMP_STRIDE + CMP_BLOCK - 1 <= t_row) & (n_idx < ncp - 1)
    s = jnp.where(valid, s, NEG)
    m = jnp.max(s, axis=1, keepdims=True)
    p = jnp.where(valid, jnp.exp(s - m), 0.0)
    l = jnp.sum(p, axis=1, keepdims=True)
    pn = p * jnp.where(l > 0.0, 1.0 / l, 0.0)
    oc = _dot(pn.astype(BF16), vc_ref[...])
    oc_ref[...] = oc[:, :HEAD_DIM].reshape(A_GROUP, TQ, HEAD_DIM)

    psum = pn[0:TQ] + pn[TQ:2 * TQ] + pn[2 * TQ:3 * TQ] + pn[3 * TQ:4 * TQ]
    p_hi = psum.astype(BF16)
    p_lo = (psum - p_hi.astype(F32)).astype(BF16)
    imp = _dot(p_hi, map_ref[...]) + _dot(p_lo, map_ref[...])

    j_idx = lax.broadcasted_iota(jnp.int32, (TQ, nslc), 1)
    cur = (t0 + lax.broadcasted_iota(jnp.int32, (TQ, nslc), 0)) // SLC_BLOCK
    forced = (j_idx == 0) | (j_idx == cur) | (j_idx == cur - 1)
    score = jnp.where(j_idx > cur, -1.0, jnp.where(forced, FORCE, imp))
    sel = jnp.zeros((TQ, nslc), jnp.bool_)
    for _ in range(min(N_SEL, nslc)):
        mx = jnp.max(score, axis=1, keepdims=True)
        first = jnp.min(jnp.where(score == mx, j_idx, nslc), axis=1, keepdims=True)
        pick = j_idx == first
        sel = sel | pick
        score = jnp.where(pick, -jnp.inf, score)
    sb_ref[...] = jnp.where(sel & (j_idx <= cur), 0.0, NEG)


def _cmp_select(proj, kc_cmp, vc_cmp, cmp_map, bsz, seq):
    ncp = kc_cmp.shape[2]
    nslc = seq // SLC_BLOCK
    nq = seq // TQ
    g_ = A_KV_HEADS
    return pl.pallas_call(
        functools.partial(_cmp_select_kernel, ncp=ncp, nslc=nslc),
        grid=(bsz, g_, nq),
        in_specs=[
            pl.BlockSpec((TQ, A_GROUP * LANES), lambda b, g, i: (b * nq + i, g)),
            pl.BlockSpec((None, None, ncp, LANES), lambda b, g, i: (b, g, 0, 0)),
            pl.BlockSpec((None, None, ncp, LANES), lambda b, g, i: (b, g, 0, 0)),
            pl.BlockSpec((ncp, nslc), lambda b, g, i: (0, 0)),
        ],
        out_specs=[
            pl.BlockSpec((None, None, A_GROUP, TQ, HEAD_DIM), lambda b, g, i: (b, g, 0, i, 0)),
            pl.BlockSpec((TQ, nslc), lambda b, g, i: (b * nq + i, g)),
        ],
        out_shape=[
            jax.ShapeDtypeStruct((bsz, g_, A_GROUP, seq, HEAD_DIM), F32),
            jax.ShapeDtypeStruct((bsz * seq, g_ * nslc), F32),
        ],
        compiler_params=_cp("parallel", "parallel", "arbitrary"),
        name="cmp_select",
    )(proj, kc_cmp, vc_cmp, cmp_map)


def _sel_win_kernel(q_ref, c_ref, sa_ref, sb_ref, bias_ref, ks_ref, vs_ref, kw_ref, vw_ref, oc_ref, gate_ref,
                    o_ref, *, nslc):
    qi = pl.program_id(2)
    t0 = qi * TQ
    rows = A_GROUP * TQ
    qb = q_ref[...]
    q4 = jnp.concatenate([qb[:, r * LANES:(r + 1) * LANES] for r in range(A_GROUP)], axis=0).astype(F32)
    q_rot = _rope(q4, _tile_rows(c_ref[...], A_GROUP), _tile_rows(sa_ref[...], A_GROUP),
                  _tile_rows(sb_ref[...], A_GROUP))
    q_rot_b = q_rot.astype(BF16)
    bias = bias_ref[...]
    lane = lax.broadcasted_iota(jnp.int32, (TQ, LANES), 1)
    bias_lanes = (lane >= HEAD_DIM) & (lane < HEAD_DIM + BLK_PER_KT)
    t_row = t0 + lax.broadcasted_iota(jnp.int32, (rows, KT), 0) % TQ
    k_off = lax.broadcasted_iota(jnp.int32, (rows, KT), 1)

    def tile(kt, carry, causal):
        m, acc = carry
        shift = (HEAD_DIM + nslc - BLK_PER_KT * kt) % nslc
        rolled = pltpu.roll(bias, shift, 1)[:, :LANES]
        qa = (q_rot + _tile_rows(jnp.where(bias_lanes, rolled, 0.0), A_GROUP)).astype(BF16)
        k0 = pl.multiple_of(kt * KT, KT)
        s = _dot_nt(qa, ks_ref[pl.ds(k0, KT), :])
        if causal:
            s = jnp.where(k0 + k_off <= t_row, s, NEG)
        m_new = jnp.maximum(m, jnp.max(s, axis=1, keepdims=True))
        p = jnp.exp(s - m_new)
        acc = jnp.exp(m - m_new) * acc + _dot(p.astype(BF16), vs_ref[pl.ds(k0, KT), :])
        return m_new, acc

    kt_d = t0 // KT
    init = (jnp.full((rows, 1), NEG, F32), jnp.zeros((rows, LANES), F32))
    carry = lax.fori_loop(0, kt_d, lambda kt, c: tile(kt, c, False), init)
    _, acc_s = tile(kt_d, carry, True)
    o_s = acc_s[:, :HEAD_DIM] / acc_s[:, HEAD_DIM:HEAD_DIM + 1]

    wspan = NSA_WIN + TQ
    w0 = pl.multiple_of(jnp.maximum(t0 - NSA_WIN, 0), TQ)
    s_w = _dot_nt(q_rot_b, kw_ref[pl.ds(w0, wspan), :])
    tw = t0 + lax.broadcasted_iota(jnp.int32, (rows, wspan), 0) % TQ
    diff = tw - (w0 + lax.broadcasted_iota(jnp.int32, (rows, wspan), 1))
    s_w = jnp.where((diff >= 0) & (diff < NSA_WIN), s_w, NEG)
    p_w = jnp.exp(s_w - jnp.max(s_w, axis=1, keepdims=True))
    acc_w = _dot(p_w.astype(BF16), vw_ref[pl.ds(w0, wspan), :])
    o_w = acc_w[:, :HEAD_DIM] / acc_w[:, HEAD_DIM:HEAD_DIM + 1]

    gates = gate_ref[...]

    def gcol(c):
        return jnp.concatenate([gates[:, c * A_GROUP + r:c * A_GROUP + r + 1] for r in range(A_GROUP)], axis=0)

    o_c = oc_ref[...].reshape(rows, HEAD_DIM)
    o = gcol(0) * o_c + gcol(1) * o_s + gcol(2) * o_w
    o_ref[...] = o.reshape(A_GROUP, TQ, HEAD_DIM)


def _sel_win(proj, tabs, selbias, o_c, gates, bsz, seq):
    nslc = seq // SLC_BLOCK
    nq = seq // TQ
    g_ = A_KV_HEADS
    c, sa, sb = tabs
    tab_spec = pl.BlockSpec((TQ, LANES), lambda b, g, i: (b * nq + i, 0))
    q_slabs = A_HEADS

    def kv_spec(base):
        return pl.BlockSpec((seq, LANES), lambda b, g, i: (b, base + g))

    return pl.pallas_call(
        functools.partial(_sel_win_kernel, nslc=nslc),
        grid=(bsz, g_, nq),
        in_specs=[
            pl.BlockSpec((TQ, A_GROUP * LANES), lambda b, g, i: (b * nq + i, g)),
            tab_spec, tab_spec, tab_spec,
            pl.BlockSpec((TQ, nslc), lambda b, g, i: (b * nq + i, g)),
            kv_spec(q_slabs), kv_spec(q_slabs + g_), kv_spec(q_slabs + 2 * g_), kv_spec(q_slabs + 3 * g_),
            pl.BlockSpec((None, None, A_GROUP, TQ, HEAD_DIM), lambda b, g, i: (b, g, 0, i, 0)),
            pl.BlockSpec((TQ, LANES), lambda b, g, i: (b * nq + i, g)),
        ],
        out_specs=pl.BlockSpec((None, None, A_GROUP, TQ, HEAD_DIM), lambda b, g, i: (b, g, 0, i, 0)),
        out_shape=jax.ShapeDtypeStruct((bsz, g_, A_GROUP, seq, HEAD_DIM), F32),
        compiler_params=_cp("parallel", "parallel", "arbitrary"),
        name="sel_win",
    )(proj, c, sa, sb, selbias, proj, proj, proj, proj, o_c, gates)


def _swa_kernel(q_ref, k_ref, v_ref, sink_ref, o_ref):
    qi = pl.program_id(2)
    t0 = qi * TQ
    rows = B_GROUP * TQ
    span = SWA_WIN + TQ
    qb = q_ref[...]
    q8 = jnp.concatenate([qb[:, r * LANES:(r + 1) * LANES] for r in range(B_GROUP)], axis=0)
    w0 = pl.multiple_of(jnp.maximum(t0 - SWA_WIN, 0), TQ)
    s = _dot_nt(q8, k_ref[pl.ds(w0, span), :])
    tq = t0 + lax.broadcasted_iota(jnp.int32, (rows, span), 0) % TQ
    diff = tq - (w0 + lax.broadcasted_iota(jnp.int32, (rows, span), 1))
    s = jnp.where((diff >= 0) & (diff < SWA_WIN), s, NEG)
    sk = sink_ref[...]
    sink = jnp.concatenate([jnp.broadcast_to(sk[r:r + 1, 0:1], (TQ, 1)) for r in range(B_GROUP)], axis=0)
    m = jnp.maximum(jnp.max(s, axis=1, keepdims=True), sink)
    e = jnp.exp(s - m)
    acc = _dot(e.astype(BF16), v_ref[pl.ds(w0, span), :])
    o = acc[:, :HEAD_DIM] / (acc[:, HEAD_DIM:HEAD_DIM + 1] + jnp.exp(sink - m))
    o_ref[...] = o.reshape(B_GROUP, TQ, HEAD_DIM)


def _swa(qproj, kvproj, sinks, bsz, seq):
    nq = seq // TQ
    g_ = B_KV_HEADS
    return pl.pallas_call(
        _swa_kernel,
        grid=(bsz, g_, nq),
        in_specs=[
            pl.BlockSpec((TQ, B_GROUP * LANES), lambda b, g, i: (b * nq + i, g)),
            pl.BlockSpec((seq, LANES), lambda b, g, i: (b, g)),
            pl.BlockSpec((seq, LANES), lambda b, g, i: (b, g_ + g)),
            pl.BlockSpec((None, B_GROUP, LANES), lambda b, g, i: (g, 0, 0)),
        ],
        out_specs=pl.BlockSpec((None, None, B_GROUP, TQ, HEAD_DIM), lambda b, g, i: (b, g, 0, i, 0)),
        out_shape=jax.ShapeDtypeStruct((bsz, g_, B_GROUP, seq, HEAD_DIM), F32),
        compiler_params=_cp("parallel", "parallel", "arbitrary"),
        name="swa",
    )(qproj, kvproj, kvproj, sinks)


def _router_kernel(x_ref, wh_ref, wl_ref, o_ref):
    x = x_ref[...]
    xh = x.astype(BF16)
    xl = (x - xh.astype(F32)).astype(BF16)
    logits = _dot(xh, wh_ref[...]) + _dot(xh, wl_ref[...]) + _dot(xl, wh_ref[...])
    lane = lax.broadcasted_iota(jnp.int32, logits.shape, 1)
    lg = jnp.where(lane < N_EXPERTS, logits, -jnp.inf)
    m1 = jnp.max(lg, axis=1, keepdims=True)
    i1 = jnp.min(jnp.where(lg == m1, lane, LANES), axis=1, keepdims=True)
    lg2 = jnp.where(lane == i1, -jnp.inf, lg)
    m2 = jnp.max(lg2, axis=1, keepdims=True)
    i2 = jnp.min(jnp.where(lg2 == m2, lane, LANES), axis=1, keepdims=True)
    e2 = jnp.exp(m2 - m1)
    g1 = 1.0 / (1.0 + e2)
    g2 = e2 * g1
    out = jnp.where(lane == 0, i1.astype(F32), jnp.where(lane == 1, i2.astype(F32),
                    jnp.where(lane == 2, g1, jnp.where(lane == 3, g2, 0.0))))
    o_ref[...] = out


def _router(h, w_hi, w_lo, tm=512):
    m, d = h.shape
    return pl.pallas_call(
        _router_kernel,
        grid=(m // tm,),
        in_specs=[pl.BlockSpec((tm, d), lambda i: (i, 0)), pl.BlockSpec((d, LANES), lambda i: (0, 0)),
                  pl.BlockSpec((d, LANES), lambda i: (0, 0))],
        out_specs=pl.BlockSpec((tm, LANES), lambda i: (i, 0)),
        out_shape=jax.ShapeDtypeStruct((m, LANES), F32),
        compiler_params=_cp("parallel"),
        name="router",
    )(h, w_hi, w_lo)


def _moe_ffn_kernel(blk_e_ref, x_ref, wg_ref, wu_ref, wo_ref, o_ref, acc_ref):
    c = pl.program_id(1)

    @pl.when(c == 0)
    def _():
        acc_ref[...] = jnp.zeros_like(acc_ref)

    xb = x_ref[...]
    gate = _dot(xb, wg_ref[...])
    up = _dot(xb, wu_ref[...])
    a = gate * _sigmoid(gate) * up
    acc_ref[...] += _dot(a.astype(BF16), wo_ref[...])

    @pl.when(c == pl.num_programs(1) - 1)
    def _():
        o_ref[...] = acc_ref[...]


def _moe_ffn(xs, blk_e, w_in, w_out, tm=MOE_TM, fc=896):
    cap, d = xs.shape
    ff = w_out.shape[1]
    nc = ff // fc
    grid_spec = pltpu.PrefetchScalarGridSpec(
        num_scalar_prefetch=1,
        grid=(cap // tm, nc),
        in_specs=[
            pl.BlockSpec((tm, d), lambda i, c, e: (i, 0)),
            pl.BlockSpec((None, d, fc), lambda i, c, e: (e[i], 0, c)),
            pl.BlockSpec((None, d, fc), lambda i, c, e: (e[i], 0, nc + c)),
            pl.BlockSpec((None, fc, d), lambda i, c, e: (e[i], c, 0)),
        ],
        out_specs=pl.BlockSpec((tm, d), lambda i, c, e: (i, 0)),
        scratch_shapes=[pltpu.VMEM((tm, d), F32)],
    )
    return pl.pallas_call(
        _moe_ffn_kernel,
        grid_spec=grid_spec,
        out_shape=jax.ShapeDtypeStruct((cap, d), F32),
        compiler_params=_cp("parallel", "arbitrary"),
        name="moe_ffn",
    )(blk_e, xs, w_in, w_in, w_out)


def _combine_ln_kernel(h_ref, y1_ref, y2_ref, r_ref, g_ref, b_ref, o_ref):
    r = r_ref[...]
    ffn = r[:, 2:3] * y1_ref[...] + r[:, 3:4] * y2_ref[...]
    o_ref[...] = _layer_norm(ALPHA * h_ref[...] + ffn, g_ref[...], b_ref[...])


def _combine_ln(h, y1, y2, route, g, b, tm=512):
    m, d = h.shape
    row = pl.BlockSpec((tm, d), lambda i: (i, 0))
    vec = pl.BlockSpec((1, d), lambda i: (0, 0))
    return pl.pallas_call(
        _combine_ln_kernel,
        grid=(m // tm,),
        in_specs=[row, row, row, pl.BlockSpec((tm, LANES), lambda i: (i, 0)), vec, vec],
        out_specs=row,
        out_shape=jax.ShapeDtypeStruct((m, d), F32),
        compiler_params=_cp("parallel"),
        name="combine_ln",
    )(h, y1, y2, route, g, b)


def _pad_heads(w, heads):
    d = w.shape[0]
    w = w.reshape(d, heads, HEAD_DIM)
    return jnp.pad(w, ((0, 0), (0, 0), (0, LANES - HEAD_DIM))).reshape(d, heads * LANES)


def _rope_tables(positions):
    half = ROT_DIM // 2
    inv = ROPE_THETA ** (-jnp.arange(0, ROT_DIM, 2, dtype=F32) / ROT_DIM)
    ang = positions.astype(F32).reshape(-1, 1) * inv
    cos, sin = jnp.cos(ang), jnp.sin(ang)
    n = ang.shape[0]
    ones = jnp.ones((n, LANES - ROT_DIM), F32)
    zeros = jnp.zeros((n, LANES - half), F32)
    c = jnp.concatenate([cos, cos, ones], axis=1)
    sa = jnp.concatenate([-sin, zeros], axis=1)
    sb = jnp.concatenate([jnp.zeros((n, half), F32), sin, jnp.zeros((n, LANES - ROT_DIM), F32)], axis=1)
    return c, sa, sb


def _cmp_to_slc(ncp, nslc):
    n = np.arange(ncp)[:, None]
    j = np.arange(nslc)[None, :]
    overlap = (np.minimum(n * CMP_STRIDE + CMP_BLOCK, j * SLC_BLOCK + SLC_BLOCK)
               - np.maximum(n * CMP_STRIDE, j * SLC_BLOCK))
    m = np.clip(overlap, 0, None).astype(np.float32) / CMP_BLOCK
    m[ncp - 1, :] = 0.0
    return jnp.asarray(m, BF16)


def _compress_branch(t, pos, w1, w2, bsz, seq):
    g_ = A_KV_HEADS
    nch = seq // CMP_STRIDE
    half = CMP_STRIDE * HEAD_DIM
    a = t.reshape(bsz, nch, CMP_STRIDE, g_, HEAD_DIM).transpose(0, 1, 3, 2, 4).reshape(bsz * nch * g_, half)
    w1b = w1.astype(BF16)
    w1cat = jnp.concatenate([w1b[:half], w1b[half:]], axis=1)
    pq = _mm(a, w1cat, F32, tm=1024).reshape(bsz, nch, g_, 2 * CMP_HID)
    pp = pq[..., :CMP_HID].reshape(-1, CMP_HID)
    qq = jnp.concatenate([pq[:, 1:, :, CMP_HID:], jnp.zeros((bsz, 1, g_, CMP_HID), F32)], axis=1).reshape(-1, CMP_HID)
    pos8 = jnp.zeros((8, CMP_BLOCK * HEAD_DIM), BF16).at[0].set(pos.reshape(-1).astype(BF16))
    out = _compress(pp, qq, pos8, w1b, w2.astype(BF16))
    out = out.reshape(bsz, nch, g_, HEAD_DIM).transpose(0, 2, 1, 3)
    return out


def _nsa_layer(h, tabs, w_in, w_out, cmp_pos, cmp_w1, cmp_w2, ln_g, ln_b, bsz, seq):
    o, _, _ = _nsa_attn(h, tabs, w_in, cmp_pos, cmp_w1, cmp_w2, bsz, seq)
    return _out_ln(o, w_out.astype(BF16), h, ln_g[None, :], ln_b[None, :])


def _nsa_attn(h, tabs, w_in, cmp_pos, cmp_w1, cmp_w2, bsz, seq):
    g_ = A_KV_HEADS
    aq = A_HEADS * HEAD_DIM
    akv = g_ * HEAD_DIM
    wq = w_in[:, :aq] * (HEAD_DIM ** -0.5)
    w_kc, w_vc, w_ks, w_vs, w_kw, w_vw = (w_in[:, aq + i * akv: aq + (i + 1) * akv] for i in range(6))
    w_gl = w_in[:, aq + 6 * akv:]
    w_big = jnp.concatenate([_pad_heads(wq, A_HEADS), _pad_heads(w_ks, g_), _pad_heads(w_vs, g_),
                             _pad_heads(w_kw, g_), _pad_heads(w_vw, g_)], axis=1).astype(BF16)
    modes = ["plain"] * 4 + ["rope_onehot", "ones", "rope", "ones"]
    proj = _proj(h, w_big, tabs, modes, tn=g_ * LANES)

    w_gl_g = w_gl.reshape(-1, 3, g_, A_GROUP).transpose(0, 2, 1, 3).reshape(-1, g_, 3 * A_GROUP)
    w_gl_g = jnp.pad(w_gl_g, ((0, 0), (0, 0), (0, LANES - 3 * A_GROUP))).reshape(-1, g_ * LANES).astype(BF16)
    gates = _mm(h, w_gl_g, F32, act="sigmoid")

    kcvc = _mm(h, jnp.concatenate([w_kc, w_vc], axis=1).astype(BF16), F32)
    kc = _compress_branch(kcvc[:, :akv], cmp_pos[0], cmp_w1[0], cmp_w2[0], bsz, seq)
    vc = _compress_branch(kcvc[:, akv:], cmp_pos[1], cmp_w1[1], cmp_w2[1], bsz, seq)
    pad = ((0, 0), (0, 0), (0, 0), (0, LANES - HEAD_DIM))
    kc = jnp.pad(kc, pad).astype(BF16)
    vc = jnp.pad(vc, pad).astype(BF16)
    ncp = seq // CMP_STRIDE
    cmp_map = _cmp_to_slc(ncp, seq // SLC_BLOCK)

    o_c, selbias = _cmp_select(proj, kc, vc, cmp_map, bsz, seq)
    o = _sel_win(proj, tabs, selbias, o_c, gates, bsz, seq)
    o = o.transpose(0, 3, 1, 2, 4).reshape(bsz * seq, aq)
    return o, selbias, o_c


def _shared_kv(h, tabs, w_kv):
    g_ = B_KV_HEADS
    bkv = g_ * HEAD_DIM
    w = jnp.concatenate([_pad_heads(w_kv[:, :bkv], g_), _pad_heads(w_kv[:, bkv:], g_)], axis=1).astype(BF16)
    return _proj(h, w, tabs, ["rope", "ones"], tn=g_ * LANES)


def _swa_layer(h, tabs, kvproj, w_q, w_out, sinks, ln_g, ln_b, bsz, seq):
    o = _swa_attn(h, tabs, kvproj, w_q, sinks, bsz, seq)
    return _out_ln(o, w_out.astype(BF16), h, ln_g[None, :], ln_b[None, :])


def _swa_attn(h, tabs, kvproj, w_q, sinks, bsz, seq):
    wq = _pad_heads(w_q * (HEAD_DIM ** -0.5), B_HEADS).astype(BF16)
    qproj = _proj(h, wq, tabs, ["rope"] * 4, tn=4 * LANES)
    sk = jnp.broadcast_to(sinks.astype(F32).reshape(B_KV_HEADS, B_GROUP, 1), (B_KV_HEADS, B_GROUP, LANES))
    o = _swa(qproj, kvproj, sk, bsz, seq)
    return o.transpose(0, 3, 1, 2, 4).reshape(bsz * seq, B_HEADS * HEAD_DIM)


def _moe_layer(h, w_router, w_in, w_out, ln_g, ln_b):
    y1, y2, route = _moe_experts(h, w_router, w_in, w_out)
    return _combine_ln(h, y1, y2, route, ln_g[None, :], ln_b[None, :])


def _moe_ffn_out(h, w_router, w_in, w_out):
    y1, y2, route = _moe_experts(h, w_router, w_in, w_out)
    return route[:, 2:3] * y1 + route[:, 3:4] * y2


def _moe_experts(h, w_router, w_in, w_out):
    n_tok, d = h.shape
    wr = jnp.pad(w_router, ((0, 0), (0, LANES - N_EXPERTS)))
    wr_hi = wr.astype(BF16)
    wr_lo = (wr - wr_hi.astype(F32)).astype(BF16)
    route = _router(h, wr_hi, wr_lo)
    top_e = route[:, :2].astype(jnp.int32)

    e_flat = top_e.reshape(-1)
    onehot = (e_flat[:, None] == jnp.arange(N_EXPERTS)[None, :]).astype(jnp.int32)
    csum = jnp.cumsum(onehot, axis=0)
    counts = csum[-1]
    rank = jnp.take_along_axis(csum, e_flat[:, None], axis=1)[:, 0] - 1
    padded = (counts + MOE_TM - 1) // MOE_TM * MOE_TM
    pad_end = jnp.cumsum(padded)
    pad_start = pad_end - padded
    dest = pad_start[e_flat] + rank
    cap = n_tok * 2 + N_EXPERTS * MOE_TM
    n_blk = cap // MOE_TM
    tok_flat = jnp.repeat(jnp.arange(n_tok, dtype=jnp.int32), 2)
    buf_tok = jnp.zeros((cap,), jnp.int32).at[dest].set(tok_flat)
    blk_e = jnp.clip(jnp.searchsorted(pad_end, jnp.arange(n_blk) * MOE_TM, side="right"), 0, N_EXPERTS - 1)
    blk_e = blk_e.astype(jnp.int32)

    xs = h.astype(BF16)[buf_tok]
    y = _moe_ffn(xs, blk_e, w_in.astype(BF16), w_out.astype(BF16))
    dest2 = dest.reshape(n_tok, 2)
    y1 = y[dest2[:, 0]]
    y2 = y[dest2[:, 1]]
    return y1, y2, route


def kernel(x, positions, w_in_a, w_out_a, cmp_pos, cmp_w1, cmp_w2, w_kv_shared, w_q_b, w_out_b, sinks_b,
           ln_g, ln_b, dense_w_in, dense_w_out, moe_router, moe_w_in, moe_w_out):
    bsz, seq, d = x.shape
    n_a = DEPTH // 2
    tabs = _rope_tables(positions)
    h = x.reshape(bsz * seq, d)
    kvproj = None
    for l in range(DEPTH):
        if l < n_a:
            h = _nsa_layer(h, tabs, w_in_a[l], w_out_a[l], cmp_pos[l], cmp_w1[l], cmp_w2[l],
                           ln_g[l, 0], ln_b[l, 0], bsz, seq)
        else:
            b = l - n_a
            h = _swa_layer(h, tabs, kvproj, w_q_b[b], w_out_b[b], sinks_b[b], ln_g[l, 0], ln_b[l, 0], bsz, seq)
        if l % 2 == 0:
            h = _ffn_dense(h, dense_w_in[l // 2].astype(BF16), dense_w_out[l // 2].astype(BF16),
                           ln_g[l, 1][None, :], ln_b[l, 1][None, :])
        else:
            h = _moe_layer(h, moe_router[l // 2], moe_w_in[l // 2], moe_w_out[l // 2], ln_g[l, 1], ln_b[l, 1])
        if l == n_a - 1:
            kvproj = _shared_kv(h, tabs, w_kv_shared)
    return h.reshape(bsz, seq, d)
```

```python
import functools

import numpy as np
import jax
import jax.numpy as jnp
from jax import lax
from jax.experimental import pallas as pl
from jax.experimental.pallas import tpu as pltpu

F32 = jnp.float32
BF16 = jnp.bfloat16

D_MODEL = 1024
DEPTH = 4
HEAD_DIM = 64
LANES = 128
ROT_DIM = HEAD_DIM // 4
ROPE_THETA = 500000.0
A_HEADS = 16
A_KV_HEADS = 4
A_GROUP = 4
CMP_BLOCK = 32
CMP_STRIDE = 16
CMP_HID = 256
SLC_BLOCK = 64
N_SEL = 16
NSA_WIN = 512
B_HEADS = 16
B_KV_HEADS = 2
B_GROUP = 8
SWA_WIN = 128
D_FF = 2816
N_EXPERTS = 8
D_FF_EXPERT = 3584
ALPHA = (2 * DEPTH) ** 0.25
LN_EPS = 1e-5
NEG = -1e30
FORCE = 1e9

TQ = 128
TQA = 256
LOG2E = 1.4426950408889634
KT = 512
BLK_PER_KT = KT // SLC_BLOCK
MOE_TM = 512
VMEM_LIMIT = 56 * 1024 * 1024


def _cp(*sem):
    return pltpu.CompilerParams(dimension_semantics=sem, vmem_limit_bytes=VMEM_LIMIT)


def _dot(a, b):
    return jnp.dot(a, b, preferred_element_type=F32)


def _dot_nt(a, b):
    return lax.dot_general(a, b, (((1,), (1,)), ((), ())), preferred_element_type=F32)


def _sigmoid(x):
    return 1.0 / (1.0 + jnp.exp(-x))


def _layer_norm(z, g, b):
    mu = jnp.mean(z, axis=-1, keepdims=True)
    zc = z - mu
    var = jnp.mean(zc * zc, axis=-1, keepdims=True)
    return zc * lax.rsqrt(var + LN_EPS) * g + b


def _rope(x, c, sa, sb):
    w = x.shape[1]
    return x * c + pltpu.roll(x, w - ROT_DIM // 2, 1) * sa + pltpu.roll(x, ROT_DIM // 2, 1) * sb


def _tile_lanes(t, n):
    return t if n == 1 else jnp.concatenate([t] * n, axis=1)


def _tile_rows(t, n):
    return t if n == 1 else jnp.concatenate([t] * n, axis=0)


def _mm_kernel(x_ref, w_ref, o_ref, *, act):
    acc = _dot(x_ref[...].astype(BF16), w_ref[...])
    if act == "sigmoid":
        acc = _sigmoid(acc)
    o_ref[...] = acc.astype(o_ref.dtype)


def _mm(x, w, out_dtype, act=None, tm=512):
    m, k = x.shape
    n = w.shape[1]
    tm = min(tm, m)
    return pl.pallas_call(
        functools.partial(_mm_kernel, act=act),
        grid=(m // tm,),
        in_specs=[pl.BlockSpec((tm, k), lambda i: (i, 0)), pl.BlockSpec((k, n), lambda i: (0, 0))],
        out_specs=pl.BlockSpec((tm, n), lambda i: (i, 0)),
        out_shape=jax.ShapeDtypeStruct((m, n), out_dtype),
        compiler_params=_cp("parallel"),
        name="mm",
    )(x, w)


def _proj_kernel(x_ref, w_ref, c_ref, sa_ref, sb_ref, o_ref, *, modes, tm, tn):
    i = pl.program_id(0)
    j = pl.program_id(1)
    acc = _dot(x_ref[...].astype(BF16), w_ref[...])
    ns = tn // LANES
    lane = lax.broadcasted_iota(jnp.int32, (tm, tn), 1) % LANES

    def roped():
        return _rope(acc, _tile_lanes(c_ref[...], ns), _tile_lanes(sa_ref[...], ns), _tile_lanes(sb_ref[...], ns))

    for mode in sorted(set(modes)):
        pred = functools.reduce(jnp.logical_or, [j == jj for jj, mm in enumerate(modes) if mm == mode])

        @pl.when(pred)
        def _(mode=mode):
            if mode == "plain":
                y = acc
            elif mode == "rope":
                y = roped()
            elif mode == "rope_onehot":
                row = i * tm + lax.broadcasted_iota(jnp.int32, (tm, tn), 0)
                blk = (row // SLC_BLOCK) % BLK_PER_KT
                y = roped() + jnp.where(lane == HEAD_DIM + blk, 1.0, 0.0)
            elif mode == "ones":
                y = acc + jnp.where(lane == HEAD_DIM, 1.0, 0.0)
            o_ref[...] = y.astype(o_ref.dtype)


def _proj(x, w, tabs, modes, tn, tm=512):
    m, k = x.shape
    n = w.shape[1]
    c, sa, sb = tabs
    return pl.pallas_call(
        functools.partial(_proj_kernel, modes=tuple(modes), tm=tm, tn=tn),
        grid=(m // tm, n // tn),
        in_specs=[
            pl.BlockSpec((tm, k), lambda i, j: (i, 0)),
            pl.BlockSpec((k, tn), lambda i, j: (0, j)),
            pl.BlockSpec((tm, LANES), lambda i, j: (i, 0)),
            pl.BlockSpec((tm, LANES), lambda i, j: (i, 0)),
            pl.BlockSpec((tm, LANES), lambda i, j: (i, 0)),
        ],
        out_specs=pl.BlockSpec((tm, tn), lambda i, j: (i, j)),
        out_shape=jax.ShapeDtypeStruct((m, n), BF16),
        compiler_params=_cp("parallel", "arbitrary"),
        name="proj",
    )(x, w, c, sa, sb)


def _out_ln_kernel(x_ref, w_ref, h_ref, g_ref, b_ref, o_ref):
    mix = _dot(x_ref[...].astype(BF16), w_ref[...])
    o_ref[...] = _layer_norm(ALPHA * h_ref[...] + mix, g_ref[...], b_ref[...])


def _out_ln(x, w, h, g, b, tm=512):
    m, k = x.shape
    d = w.shape[1]
    return pl.pallas_call(
        _out_ln_kernel,
        grid=(m // tm,),
        in_specs=[
            pl.BlockSpec((tm, k), lambda i: (i, 0)),
            pl.BlockSpec((k, d), lambda i: (0, 0)),
            pl.BlockSpec((tm, d), lambda i: (i, 0)),
            pl.BlockSpec((1, d), lambda i: (0, 0)),
            pl.BlockSpec((1, d), lambda i: (0, 0)),
        ],
        out_specs=pl.BlockSpec((tm, d), lambda i: (i, 0)),
        out_shape=jax.ShapeDtypeStruct((m, d), F32),
        compiler_params=_cp("parallel"),
        name="out_ln",
    )(x, w, h, g, b)


def _ffn_kernel(x_ref, wg_ref, wu_ref, wo_ref, g_ref, b_ref, o_ref, acc_ref):
    c = pl.program_id(1)

    @pl.when(c == 0)
    def _():
        acc_ref[...] = jnp.zeros_like(acc_ref)

    xb = x_ref[...].astype(BF16)
    gate = _dot(xb, wg_ref[...])
    up = _dot(xb, wu_ref[...])
    a = gate * _sigmoid(gate) * up
    acc_ref[...] += _dot(a.astype(BF16), wo_ref[...])

    @pl.when(c == pl.num_programs(1) - 1)
    def _():
        o_ref[...] = _layer_norm(ALPHA * x_ref[...] + acc_ref[...], g_ref[...], b_ref[...])


def _ffn_dense(h, w_in, w_out, g, b, tm=512, fc=1408):
    m, d = h.shape
    ff = w_out.shape[0]
    nc = ff // fc
    return pl.pallas_call(
        _ffn_kernel,
        grid=(m // tm, nc),
        in_specs=[
            pl.BlockSpec((tm, d), lambda i, c: (i, 0)),
            pl.BlockSpec((d, fc), lambda i, c: (0, c)),
            pl.BlockSpec((d, fc), lambda i, c: (0, nc + c)),
            pl.BlockSpec((fc, d), lambda i, c: (c, 0)),
            pl.BlockSpec((1, d), lambda i, c: (0, 0)),
            pl.BlockSpec((1, d), lambda i, c: (0, 0)),
        ],
        out_specs=pl.BlockSpec((tm, d), lambda i, c: (i, 0)),
        out_shape=jax.ShapeDtypeStruct((m, d), F32),
        scratch_shapes=[pltpu.VMEM((tm, d), F32)],
        compiler_params=_cp("parallel", "arbitrary"),
        name="ffn_dense",
    )(h, w_in, w_in, w_out, g, b)


def _compress_kernel(p_ref, q_ref, pos_ref, w1_ref, w2_ref, o_ref):
    posb = _dot(pos_ref[...], w1_ref[...])[0:1, :]
    hid = p_ref[...] + q_ref[...] + posb
    act = jax.nn.gelu(hid, approximate=True)
    o_ref[...] = _dot(act.astype(BF16), w2_ref[...])


def _compress(pp, qq, pos8, w1, w2, tm=1024):
    m, hid = pp.shape
    dh = w2.shape[1]
    return pl.pallas_call(
        _compress_kernel,
        grid=(m // tm,),
        in_specs=[
            pl.BlockSpec((tm, hid), lambda i: (i, 0)),
            pl.BlockSpec((tm, hid), lambda i: (i, 0)),
            pl.BlockSpec(pos8.shape, lambda i: (0, 0)),
            pl.BlockSpec(w1.shape, lambda i: (0, 0)),
            pl.BlockSpec(w2.shape, lambda i: (0, 0)),
        ],
        out_specs=pl.BlockSpec((tm, dh), lambda i: (i, 0)),
        out_shape=jax.ShapeDtypeStruct((m, dh), F32),
        compiler_params=_cp("parallel"),
        name="compress",
    )(pp, qq, pos8, w1, w2)


def _cmp_select_kernel(q_ref, kc_ref, vc_ref, mapt_ref, oc_ref, sb_ref, *, ncp, nslc):
    qi = pl.program_id(2)
    t0 = qi * TQA
    rows = A_GROUP * TQA
    qb = q_ref[...]
    q4 = jnp.concatenate([qb[:, r * LANES:(r + 1) * LANES] for r in range(A_GROUP)], axis=0)
    s = _dot_nt(q4, kc_ref[...])
    t_row = t0 + lax.broadcasted_iota(jnp.int32, (rows, ncp), 0) % TQA
    n_idx = lax.broadcasted_iota(jnp.int32, (rows, ncp), 1)
    valid = (n_idx * CMP_STRIDE + CMP_BLOCK - 1 <= t_row) & (n_idx < ncp - 1)
    s = jnp.where(valid, s, NEG)
    m = jnp.max(s, axis=1, keepdims=True)
    p = jnp.where(valid, jnp.exp(s - m), 0.0)
    l = jnp.sum(p, axis=1, keepdims=True)
    pn = p * jnp.where(l > 0.0, 1.0 / l, 0.0)
    oc = _dot(pn.astype(BF16), vc_ref[...])
    oc_ref[...] = oc[:, :HEAD_DIM].reshape(A_GROUP, TQA, HEAD_DIM)

    psum = pn[0:TQA] + pn[TQA:2 * TQA] + pn[2 * TQA:3 * TQA] + pn[3 * TQA:4 * TQA]
    p_hi = psum.astype(BF16)
    p_lo = (psum - p_hi.astype(F32)).astype(BF16)
    imp = _dot_nt(mapt_ref[...], p_hi) + _dot_nt(mapt_ref[...], p_lo)

    j_idx = lax.broadcasted_iota(jnp.int32, (nslc, TQA), 0)
    cur = (t0 + lax.broadcasted_iota(jnp.int32, (nslc, TQA), 1)) // SLC_BLOCK
    forced = (j_idx == 0) | (j_idx == cur) | (j_idx == cur - 1)
    score = jnp.where(j_idx > cur, -1.0, jnp.where(forced, FORCE, imp))
    sel = jnp.zeros((nslc, TQA), jnp.bool_)
    for _ in range(min(N_SEL, nslc)):
        mx = jnp.max(score, axis=0, keepdims=True)
        first = jnp.min(jnp.where(score == mx, j_idx, nslc), axis=0, keepdims=True)
        pick = j_idx == first
        sel = sel | pick
        score = jnp.where(pick, -jnp.inf, score)
    sb_ref[...] = jnp.where(sel & (j_idx <= cur), 0.0, NEG).T


def _cmp_select(proj, kc_cmp, vc_cmp, cmp_map_t, bsz, seq):
    ncp = kc_cmp.shape[2]
    nslc = seq // SLC_BLOCK
    nq = seq // TQA
    g_ = A_KV_HEADS
    return pl.pallas_call(
        functools.partial(_cmp_select_kernel, ncp=ncp, nslc=nslc),
        grid=(bsz, g_, nq),
        in_specs=[
            pl.BlockSpec((TQA, A_GROUP * LANES), lambda b, g, i: (b * nq + i, g)),
            pl.BlockSpec((None, None, ncp, LANES), lambda b, g, i: (b, g, 0, 0)),
            pl.BlockSpec((None, None, ncp, LANES), lambda b, g, i: (b, g, 0, 0)),
            pl.BlockSpec((nslc, ncp), lambda b, g, i: (0, 0)),
        ],
        out_specs=[
            pl.BlockSpec((None, None, A_GROUP, TQA, HEAD_DIM), lambda b, g, i: (b, g, 0, i, 0)),
            pl.BlockSpec((TQA, nslc), lambda b, g, i: (b * nq + i, g)),
        ],
        out_shape=[
            jax.ShapeDtypeStruct((bsz, g_, A_GROUP, seq, HEAD_DIM), F32),
            jax.ShapeDtypeStruct((bsz * seq, g_ * nslc), F32),
        ],
        compiler_params=_cp("parallel", "parallel", "arbitrary"),
        name="cmp_select",
    )(proj, kc_cmp, vc_cmp, cmp_map_t)


def _sel_win_kernel(q_ref, c_ref, sa_ref, sb_ref, bias_ref, ks_ref, vs_ref, kw_ref, vw_ref, oc_ref, gate_ref,
                    o_ref, *, nslc):
    qi = pl.program_id(2)
    t0 = qi * TQA
    rows = A_GROUP * TQA
    qb = q_ref[...]
    q4 = jnp.concatenate([qb[:, r * LANES:(r + 1) * LANES] for r in range(A_GROUP)], axis=0).astype(F32)
    q_rot = _rope(q4, _tile_rows(c_ref[...], A_GROUP), _tile_rows(sa_ref[...], A_GROUP),
                  _tile_rows(sb_ref[...], A_GROUP)) * LOG2E
    q_rot_b = q_rot.astype(BF16)
    bias = bias_ref[...]
    lane = lax.broadcasted_iota(jnp.int32, (TQA, LANES), 1)
    bias_lanes = (lane >= HEAD_DIM) & (lane < HEAD_DIM + BLK_PER_KT)
    t_row = t0 + lax.broadcasted_iota(jnp.int32, (rows, KT), 0) % TQA
    k_off = lax.broadcasted_iota(jnp.int32, (rows, KT), 1)

    def scores(kt, causal):
        shift = (HEAD_DIM + nslc - BLK_PER_KT * kt) % nslc
        rolled = pltpu.roll(bias, shift, 1)[:, :LANES]
        qa = (q_rot + _tile_rows(jnp.where(bias_lanes, rolled, 0.0), A_GROUP)).astype(BF16)
        k0 = pl.multiple_of(kt * KT, KT)
        s = _dot_nt(qa, ks_ref[pl.ds(k0, KT), :])
        if causal:
            s = jnp.where(k0 + k_off <= t_row, s, NEG)
        return s

    def pair(j, carry, causal):
        m, acc = carry
        s0 = scores(2 * j, causal)
        s1 = scores(2 * j + 1, causal)
        m_new = jnp.maximum(m, jnp.maximum(jnp.max(s0, axis=1, keepdims=True), jnp.max(s1, axis=1, keepdims=True)))
        p0 = jnp.exp2(s0 - m_new).astype(BF16)
        p1 = jnp.exp2(s1 - m_new).astype(BF16)
        k0 = pl.multiple_of(2 * j * KT, 2 * KT)
        pv = _dot(p0, vs_ref[pl.ds(k0, KT), :]) + _dot(p1, vs_ref[pl.ds(k0 + KT, KT), :])
        return m_new, jnp.exp2(m - m_new) * acc + pv

    n_pairs = t0 // (2 * KT) + 1
    init = (jnp.full((rows, 1), NEG, F32), jnp.zeros((rows, LANES), F32))
    carry = lax.fori_loop(0, n_pairs - 1, lambda j, c: pair(j, c, False), init)
    _, acc_s = pair(n_pairs - 1, carry, True)
    o_s = acc_s[:, :HEAD_DIM] / acc_s[:, HEAD_DIM:HEAD_DIM + 1]

    wspan = NSA_WIN + TQA
    w0 = pl.multiple_of(jnp.maximum(t0 - NSA_WIN, 0), TQA)
    s_w = _dot_nt(q_rot_b, kw_ref[pl.ds(w0, wspan), :])
    tw = t0 + lax.broadcasted_iota(jnp.int32, (rows, wspan), 0) % TQA
    diff = tw - (w0 + lax.broadcasted_iota(jnp.int32, (rows, wspan), 1))
    s_w = jnp.where((diff >= 0) & (diff < NSA_WIN), s_w, NEG)
    p_w = jnp.exp2(s_w - jnp.max(s_w, axis=1, keepdims=True))
    acc_w = _dot(p_w.astype(BF16), vw_ref[pl.ds(w0, wspan), :])
    o_w = acc_w[:, :HEAD_DIM] / acc_w[:, HEAD_DIM:HEAD_DIM + 1]

    gates = gate_ref[...]

    def gcol(c):
        return jnp.concatenate([gates[:, c * A_GROUP + r:c * A_GROUP + r + 1] for r in range(A_GROUP)], axis=0)

    o_c = oc_ref[...].reshape(rows, HEAD_DIM)
    o = gcol(0) * o_c + gcol(1) * o_s + gcol(2) * o_w
    o_ref[...] = o.reshape(A_GROUP, TQA, HEAD_DIM)


def _sel_win(proj, tabs, selbias, o_c, gates, bsz, seq):
    nslc = seq // SLC_BLOCK
    nq = seq // TQA
    g_ = A_KV_HEADS
    c, sa, sb = tabs
    tab_spec = pl.BlockSpec((TQA, LANES), lambda b, g, i: (b * nq + i, 0))
    q_slabs = A_HEADS

    def kv_spec(base):
        return pl.BlockSpec((seq, LANES), lambda b, g, i: (b, base + g))

    return pl.pallas_call(
        functools.partial(_sel_win_kernel, nslc=nslc),
        grid=(bsz, g_, nq),
        in_specs=[
            pl.BlockSpec((TQA, A_GROUP * LANES), lambda b, g, i: (b * nq + i, g)),
            tab_spec, tab_spec, tab_spec,
            pl.BlockSpec((TQA, nslc), lambda b, g, i: (b * nq + i, g)),
            kv_spec(q_slabs), kv_spec(q_slabs + g_), kv_spec(q_slabs + 2 * g_), kv_spec(q_slabs + 3 * g_),
            pl.BlockSpec((None, None, A_GROUP, TQA, HEAD_DIM), lambda b, g, i: (b, g, 0, i, 0)),
            pl.BlockSpec((TQA, LANES), lambda b, g, i: (b * nq + i, g)),
        ],
        out_specs=pl.BlockSpec((None, None, A_GROUP, TQA, HEAD_DIM), lambda b, g, i: (b, g, 0, i, 0)),
        out_shape=jax.ShapeDtypeStruct((bsz, g_, A_GROUP, seq, HEAD_DIM), F32),
        compiler_params=_cp("parallel", "parallel", "arbitrary"),
        name="sel_win",
    )(proj, c, sa, sb, selbias, proj, proj, proj, proj, o_c, gates)


def _swa_kernel(q_ref, k_ref, v_ref, sink_ref, o_ref):
    qi = pl.program_id(2)
    t0 = qi * TQ
    rows = B_GROUP * TQ
    span = SWA_WIN + TQ
    qb = q_ref[...]
    q8 = jnp.concatenate([qb[:, r * LANES:(r + 1) * LANES] for r in range(B_GROUP)], axis=0)
    w0 = pl.multiple_of(jnp.maximum(t0 - SWA_WIN, 0), TQ)
    s = _dot_nt(q8, k_ref[pl.ds(w0, span), :])
    tq = t0 + lax.broadcasted_iota(jnp.int32, (rows, span), 0) % TQ
    diff = tq - (w0 + lax.broadcasted_iota(jnp.int32, (rows, span), 1))
    s = jnp.where((diff >= 0) & (diff < SWA_WIN), s, NEG)
    sk = sink_ref[...]
    sink = jnp.concatenate([jnp.broadcast_to(sk[r:r + 1, 0:1], (TQ, 1)) for r in range(B_GROUP)], axis=0)
    m = jnp.maximum(jnp.max(s, axis=1, keepdims=True), sink)
    e = jnp.exp(s - m)
    acc = _dot(e.astype(BF16), v_ref[pl.ds(w0, span), :])
    o = acc[:, :HEAD_DIM] / (acc[:, HEAD_DIM:HEAD_DIM + 1] + jnp.exp(sink - m))
    o_ref[...] = o.reshape(B_GROUP, TQ, HEAD_DIM)


def _swa(qproj, kvproj, sinks, bsz, seq):
    nq = seq // TQ
    g_ = B_KV_HEADS
    return pl.pallas_call(
        _swa_kernel,
        grid=(bsz, g_, nq),
        in_specs=[
            pl.BlockSpec((TQ, B_GROUP * LANES), lambda b, g, i: (b * nq + i, g)),
            pl.BlockSpec((seq, LANES), lambda b, g, i: (b, g)),
            pl.BlockSpec((seq, LANES), lambda b, g, i: (b, g_ + g)),
            pl.BlockSpec((None, B_GROUP, LANES), lambda b, g, i: (g, 0, 0)),
        ],
        out_specs=pl.BlockSpec((None, None, B_GROUP, TQ, HEAD_DIM), lambda b, g, i: (b, g, 0, i, 0)),
        out_shape=jax.ShapeDtypeStruct((bsz, g_, B_GROUP, seq, HEAD_DIM), F32),
        compiler_params=_cp("parallel", "parallel", "arbitrary"),
        name="swa",
    )(qproj, kvproj, kvproj, sinks)


def _router_kernel(x_ref, wh_ref, wl_ref, o_ref):
    x = x_ref[...]
    xh = x.astype(BF16)
    xl = (x - xh.astype(F32)).astype(BF16)
    logits = _dot(xh, wh_ref[...]) + _dot(xh, wl_ref[...]) + _dot(xl, wh_ref[...])
    lane = lax.broadcasted_iota(jnp.int32, logits.shape, 1)
    lg = jnp.where(lane < N_EXPERTS, logits, -jnp.inf)
    m1 = jnp.max(lg, axis=1, keepdims=True)
    i1 = jnp.min(jnp.where(lg == m1, lane, LANES), axis=1, keepdims=True)
    lg2 = jnp.where(lane == i1, -jnp.inf, lg)
    m2 = jnp.max(lg2, axis=1, keepdims=True)
    i2 = jnp.min(jnp.where(lg2 == m2, lane, LANES), axis=1, keepdims=True)
    e2 = jnp.exp(m2 - m1)
    g1 = 1.0 / (1.0 + e2)
    g2 = e2 * g1
    out = jnp.where(lane == 0, i1.astype(F32), jnp.where(lane == 1, i2.astype(F32),
                    jnp.where(lane == 2, g1, jnp.where(lane == 3, g2, 0.0))))
    o_ref[...] = out


def _router(h, w_hi, w_lo, tm=512):
    m, d = h.shape
    return pl.pallas_call(
        _router_kernel,
        grid=(m // tm,),
        in_specs=[pl.BlockSpec((tm, d), lambda i: (i, 0)), pl.BlockSpec((d, LANES), lambda i: (0, 0)),
                  pl.BlockSpec((d, LANES), lambda i: (0, 0))],
        out_specs=pl.BlockSpec((tm, LANES), lambda i: (i, 0)),
        out_shape=jax.ShapeDtypeStruct((m, LANES), F32),
        compiler_params=_cp("parallel"),
        name="router",
    )(h, w_hi, w_lo)


def _moe_ffn_kernel(blk_e_ref, x_ref, wg_ref, wu_ref, wo_ref, o_ref, acc_ref):
    c = pl.program_id(1)

    @pl.when(c == 0)
    def _():
        acc_ref[...] = jnp.zeros_like(acc_ref)

    xb = x_ref[...]
    gate = _dot(xb, wg_ref[...])
    up = _dot(xb, wu_ref[...])
    a = gate * _sigmoid(gate) * up
    acc_ref[...] += _dot(a.astype(BF16), wo_ref[...])

    @pl.when(c == pl.num_programs(1) - 1)
    def _():
        o_ref[...] = acc_ref[...]


def _moe_ffn(xs, blk_e, w_in, w_out, tm=MOE_TM, fc=896):
    cap, d = xs.shape
    ff = w_out.shape[1]
    nc = ff // fc
    grid_spec = pltpu.PrefetchScalarGridSpec(
        num_scalar_prefetch=1,
        grid=(cap // tm, nc),
        in_specs=[
            pl.BlockSpec((tm, d), lambda i, c, e: (i, 0)),
            pl.BlockSpec((None, d, fc), lambda i, c, e: (e[i], 0, c)),
            pl.BlockSpec((None, d, fc), lambda i, c, e: (e[i], 0, nc + c)),
            pl.BlockSpec((None, fc, d), lambda i, c, e: (e[i], c, 0)),
        ],
        out_specs=pl.BlockSpec((tm, d), lambda i, c, e: (i, 0)),
        scratch_shapes=[pltpu.VMEM((tm, d), F32)],
    )
    return pl.pallas_call(
        _moe_ffn_kernel,
        grid_spec=grid_spec,
        out_shape=jax.ShapeDtypeStruct((cap, d), F32),
        compiler_params=_cp("parallel", "arbitrary"),
        name="moe_ffn",
    )(blk_e, xs, w_in, w_in, w_out)


def _combine_ln_kernel(h_ref, y1_ref, y2_ref, r_ref, g_ref, b_ref, o_ref):
    r = r_ref[...]
    ffn = r[:, 2:3] * y1_ref[...] + r[:, 3:4] * y2_ref[...]
    o_ref[...] = _layer_norm(ALPHA * h_ref[...] + ffn, g_ref[...], b_ref[...])


def _combine_ln(h, y1, y2, route, g, b, tm=512):
    m, d = h.shape
    row = pl.BlockSpec((tm, d), lambda i: (i, 0))
    vec = pl.BlockSpec((1, d), lambda i: (0, 0))
    return pl.pallas_call(
        _combine_ln_kernel,
        grid=(m // tm,),
        in_specs=[row, row, row, pl.BlockSpec((tm, LANES), lambda i: (i, 0)), vec, vec],
        out_specs=row,
        out_shape=jax.ShapeDtypeStruct((m, d), F32),
        compiler_params=_cp("parallel"),
        name="combine_ln",
    )(h, y1, y2, route, g, b)


def _pad_heads(w, heads):
    d = w.shape[0]
    w = w.reshape(d, heads, HEAD_DIM)
    return jnp.pad(w, ((0, 0), (0, 0), (0, LANES - HEAD_DIM))).reshape(d, heads * LANES)


def _rope_tables(positions):
    half = ROT_DIM // 2
    inv = ROPE_THETA ** (-jnp.arange(0, ROT_DIM, 2, dtype=F32) / ROT_DIM)
    ang = positions.astype(F32).reshape(-1, 1) * inv
    cos, sin = jnp.cos(ang), jnp.sin(ang)
    n = ang.shape[0]
    ones = jnp.ones((n, LANES - ROT_DIM), F32)
    zeros = jnp.zeros((n, LANES - half), F32)
    c = jnp.concatenate([cos, cos, ones], axis=1)
    sa = jnp.concatenate([-sin, zeros], axis=1)
    sb = jnp.concatenate([jnp.zeros((n, half), F32), sin, jnp.zeros((n, LANES - ROT_DIM), F32)], axis=1)
    return c, sa, sb


def _cmp_to_slc(ncp, nslc):
    n = np.arange(ncp)[:, None]
    j = np.arange(nslc)[None, :]
    overlap = (np.minimum(n * CMP_STRIDE + CMP_BLOCK, j * SLC_BLOCK + SLC_BLOCK)
               - np.maximum(n * CMP_STRIDE, j * SLC_BLOCK))
    m = np.clip(overlap, 0, None).astype(np.float32) / CMP_BLOCK
    m[ncp - 1, :] = 0.0
    return jnp.asarray(m, BF16)


def _compress_branch(t, pos, w1, w2, bsz, seq):
    g_ = A_KV_HEADS
    nch = seq // CMP_STRIDE
    half = CMP_STRIDE * HEAD_DIM
    a = t.reshape(bsz, nch, CMP_STRIDE, g_, HEAD_DIM).transpose(0, 1, 3, 2, 4).reshape(bsz * nch * g_, half)
    w1b = w1.astype(BF16)
    w1cat = jnp.concatenate([w1b[:half], w1b[half:]], axis=1)
    pq = _mm(a, w1cat, F32, tm=1024).reshape(bsz, nch, g_, 2 * CMP_HID)
    pp = pq[..., :CMP_HID].reshape(-1, CMP_HID)
    qq = jnp.concatenate([pq[:, 1:, :, CMP_HID:], jnp.zeros((bsz, 1, g_, CMP_HID), F32)], axis=1).reshape(-1, CMP_HID)
    pos8 = jnp.zeros((8, CMP_BLOCK * HEAD_DIM), BF16).at[0].set(pos.reshape(-1).astype(BF16))
    out = _compress(pp, qq, pos8, w1b, w2.astype(BF16))
    out = out.reshape(bsz, nch, g_, HEAD_DIM).transpose(0, 2, 1, 3)
    return out


def _nsa_layer(h, tabs, w_in, w_out, cmp_pos, cmp_w1, cmp_w2, ln_g, ln_b, bsz, seq):
    o, _, _ = _nsa_attn(h, tabs, w_in, cmp_pos, cmp_w1, cmp_w2, bsz, seq)
    return _out_ln(o, w_out.astype(BF16), h, ln_g[None, :], ln_b[None, :])


def _nsa_attn(h, tabs, w_in, cmp_pos, cmp_w1, cmp_w2, bsz, seq):
    g_ = A_KV_HEADS
    aq = A_HEADS * HEAD_DIM
    akv = g_ * HEAD_DIM
    wq = w_in[:, :aq] * (HEAD_DIM ** -0.5)
    w_kc, w_vc, w_ks, w_vs, w_kw, w_vw = (w_in[:, aq + i * akv: aq + (i + 1) * akv] for i in range(6))
    w_gl = w_in[:, aq + 6 * akv:]
    w_big = jnp.concatenate([_pad_heads(wq, A_HEADS), _pad_heads(w_ks, g_), _pad_heads(w_vs, g_),
                             _pad_heads(w_kw, g_), _pad_heads(w_vw, g_)], axis=1).astype(BF16)
    modes = ["plain"] * 4 + ["rope_onehot", "ones", "rope", "ones"]
    proj = _proj(h, w_big, tabs, modes, tn=g_ * LANES)

    w_gl_g = w_gl.reshape(-1, 3, g_, A_GROUP).transpose(0, 2, 1, 3).reshape(-1, g_, 3 * A_GROUP)
    w_gl_g = jnp.pad(w_gl_g, ((0, 0), (0, 0), (0, LANES - 3 * A_GROUP))).reshape(-1, g_ * LANES).astype(BF16)
    gates = _mm(h, w_gl_g, F32, act="sigmoid")

    kcvc = _mm(h, jnp.concatenate([w_kc, w_vc], axis=1).astype(BF16), F32)
    kc = _compress_branch(kcvc[:, :akv], cmp_pos[0], cmp_w1[0], cmp_w2[0], bsz, seq)
    vc = _compress_branch(kcvc[:, akv:], cmp_pos[1], cmp_w1[1], cmp_w2[1], bsz, seq)
    pad = ((0, 0), (0, 0), (0, 0), (0, LANES - HEAD_DIM))
    kc = jnp.pad(kc, pad).astype(BF16)
    vc = jnp.pad(vc, pad).astype(BF16)
    ncp = seq // CMP_STRIDE
    cmp_map_t = _cmp_to_slc(ncp, seq // SLC_BLOCK).T

    o_c, selbias = _cmp_select(proj, kc, vc, cmp_map_t, bsz, seq)
    o = _sel_win(proj, tabs, selbias, o_c, gates, bsz, seq)
    o = o.transpose(0, 3, 1, 2, 4).reshape(bsz * seq, aq)
    return o, selbias, o_c


def _shared_kv(h, tabs, w_kv):
    g_ = B_KV_HEADS
    bkv = g_ * HEAD_DIM
    w = jnp.concatenate([_pad_heads(w_kv[:, :bkv], g_), _pad_heads(w_kv[:, bkv:], g_)], axis=1).astype(BF16)
    return _proj(h, w, tabs, ["rope", "ones"], tn=g_ * LANES)


def _swa_layer(h, tabs, kvproj, w_q, w_out, sinks, ln_g, ln_b, bsz, seq):
    o = _swa_attn(h, tabs, kvproj, w_q, sinks, bsz, seq)
    return _out_ln(o, w_out.astype(BF16), h, ln_g[None, :], ln_b[None, :])


def _swa_attn(h, tabs, kvproj, w_q, sinks, bsz, seq):
    wq = _pad_heads(w_q * (HEAD_DIM ** -0.5), B_HEADS).astype(BF16)
    qproj = _proj(h, wq, tabs, ["rope"] * 4, tn=4 * LANES)
    sk = jnp.broadcast_to(sinks.astype(F32).reshape(B_KV_HEADS, B_GROUP, 1), (B_KV_HEADS, B_GROUP, LANES))
    o = _swa(qproj, kvproj, sk, bsz, seq)
    return o.transpose(0, 3, 1, 2, 4).reshape(bsz * seq, B_HEADS * HEAD_DIM)


def _moe_layer(h, w_router, w_in, w_out, ln_g, ln_b):
    y1, y2, route = _moe_experts(h, w_router, w_in, w_out)
    return _combine_ln(h, y1, y2, route, ln_g[None, :], ln_b[None, :])


def _moe_ffn_out(h, w_router, w_in, w_out):
    y1, y2, route = _moe_experts(h, w_router, w_in, w_out)
    return route[:, 2:3] * y1 + route[:, 3:4] * y2


def _moe_experts(h, w_router, w_in, w_out):
    n_tok, d = h.shape
    wr = jnp.pad(w_router, ((0, 0), (0, LANES - N_EXPERTS)))
    wr_hi = wr.astype(BF16)
    wr_lo = (wr - wr_hi.astype(F32)).astype(BF16)
    route = _router(h, wr_hi, wr_lo)
    top_e = route[:, :2].astype(jnp.int32)

    e_flat = top_e.reshape(-1)
    onehot = (e_flat[:, None] == jnp.arange(N_EXPERTS)[None, :]).astype(jnp.int32)
    csum = jnp.cumsum(onehot, axis=0)
    counts = csum[-1]
    rank = jnp.take_along_axis(csum, e_flat[:, None], axis=1)[:, 0] - 1
    padded = (counts + MOE_TM - 1) // MOE_TM * MOE_TM
    pad_end = jnp.cumsum(padded)
    pad_start = pad_end - padded
    dest = pad_start[e_flat] + rank
    cap = n_tok * 2 + N_EXPERTS * MOE_TM
    n_blk = cap // MOE_TM
    tok_flat = jnp.repeat(jnp.arange(n_tok, dtype=jnp.int32), 2)
    buf_tok = jnp.zeros((cap,), jnp.int32).at[dest].set(tok_flat)
    blk_e = jnp.clip(jnp.searchsorted(pad_end, jnp.arange(n_blk) * MOE_TM, side="right"), 0, N_EXPERTS - 1)
    blk_e = blk_e.astype(jnp.int32)

    xs = h.astype(BF16)[buf_tok]
    y = _moe_ffn(xs, blk_e, w_in.astype(BF16), w_out.astype(BF16))
    dest2 = dest.reshape(n_tok, 2)
    y1 = y[dest2[:, 0]]
    y2 = y[dest2[:, 1]]
    return y1, y2, route


def kernel(x, positions, w_in_a, w_out_a, cmp_pos, cmp_w1, cmp_w2, w_kv_shared, w_q_b, w_out_b, sinks_b,
           ln_g, ln_b, dense_w_in, dense_w_out, moe_router, moe_w_in, moe_w_out):
    bsz, seq, d = x.shape
    n_a = DEPTH // 2
    tabs = _rope_tables(positions)
    h = x.reshape(bsz * seq, d)
    kvproj = None
    for l in range(DEPTH):
        if l < n_a:
            h = _nsa_layer(h, tabs, w_in_a[l], w_out_a[l], cmp_pos[l], cmp_w1[l], cmp_w2[l],
                           ln_g[l, 0], ln_b[l, 0], bsz, seq)
        else:
            b = l - n_a
            h = _swa_layer(h, tabs, kvproj, w_q_b[b], w_out_b[b], sinks_b[b], ln_g[l, 0], ln_b[l, 0], bsz, seq)
        if l % 2 == 0:
            h = _ffn_dense(h, dense_w_in[l // 2].astype(BF16), dense_w_out[l // 2].astype(BF16),
                           ln_g[l, 1][None, :], ln_b[l, 1][None, :])
        else:
            h = _moe_layer(h, moe_router[l // 2], moe_w_in[l // 2], moe_w_out[l // 2], ln_g[l, 1], ln_b[l, 1])
        if l == n_a - 1:
            kvproj = _shared_kv(h, tabs, w_kv_shared)
    return h.reshape(bsz, seq, d)
```

```python
import functools

import numpy as np
import jax
import jax.numpy as jnp
from jax import lax
from jax.experimental import pallas as pl
from jax.experimental.pallas import tpu as pltpu

F32 = jnp.float32
BF16 = jnp.bfloat16

D_MODEL = 1024
DEPTH = 4
HEAD_DIM = 64
LANES = 128
ROT_DIM = HEAD_DIM // 4
ROPE_THETA = 500000.0
A_HEADS = 16
A_KV_HEADS = 4
A_GROUP = 4
CMP_BLOCK = 32
CMP_STRIDE = 16
CMP_HID = 256
SLC_BLOCK = 64
N_SEL = 16
N_FORCED = 3
NSA_WIN = 512
B_HEADS = 16
B_KV_HEADS = 2
B_GROUP = 8
SWA_WIN = 128
D_FF = 2816
N_EXPERTS = 8
D_FF_EXPERT = 3584
ALPHA = (2 * DEPTH) ** 0.25
LN_EPS = 1e-5
NEG = -1e30
FORCE = 1e9

TQB = 256
TQA = 256
CMP_CHUNK = 256
LOG2E = 1.4426950408889634
KT = 512
BLK_PER_KT = KT // SLC_BLOCK
MOE_TM = 512
VMEM_LIMIT = 56 * 1024 * 1024


def _cp(*sem):
    return pltpu.CompilerParams(dimension_semantics=sem, vmem_limit_bytes=VMEM_LIMIT)


def _dot(a, b):
    return jnp.dot(a, b, preferred_element_type=F32)


def _dot_nt(a, b):
    return lax.dot_general(a, b, (((1,), (1,)), ((), ())), preferred_element_type=F32)


def _sigmoid(x):
    return 1.0 / (1.0 + jnp.exp(-x))


def _layer_norm(z, g, b):
    mu = jnp.mean(z, axis=-1, keepdims=True)
    zc = z - mu
    var = jnp.mean(zc * zc, axis=-1, keepdims=True)
    return zc * lax.rsqrt(var + LN_EPS) * g + b


def _rope(x, c, sa, sb):
    w = x.shape[1]
    return x * c + pltpu.roll(x, w - ROT_DIM // 2, 1) * sa + pltpu.roll(x, ROT_DIM // 2, 1) * sb


def _tile_lanes(t, n):
    return t if n == 1 else jnp.concatenate([t] * n, axis=1)


def _tile_rows(t, n):
    return t if n == 1 else jnp.concatenate([t] * n, axis=0)


def _heads_to_lanes(o, heads):
    t = o.shape[0] // heads
    return jnp.concatenate([o[r * t:(r + 1) * t] for r in range(heads)], axis=1)


def _mm_kernel(x_ref, w_ref, o_ref, *, sig_cols):
    acc = _dot(x_ref[...].astype(BF16), w_ref[...])
    if sig_cols:
        o_ref[:, :sig_cols] = _sigmoid(acc[:, :sig_cols]).astype(o_ref.dtype)
        o_ref[:, sig_cols:] = acc[:, sig_cols:].astype(o_ref.dtype)
    else:
        o_ref[...] = acc.astype(o_ref.dtype)


def _mm(x, w, out_dtype, sig_cols=0, tm=512):
    m, k = x.shape
    n = w.shape[1]
    tm = min(tm, m)
    return pl.pallas_call(
        functools.partial(_mm_kernel, sig_cols=sig_cols),
        grid=(m // tm,),
        in_specs=[pl.BlockSpec((tm, k), lambda i: (i, 0)), pl.BlockSpec((k, n), lambda i: (0, 0))],
        out_specs=pl.BlockSpec((tm, n), lambda i: (i, 0)),
        out_shape=jax.ShapeDtypeStruct((m, n), out_dtype),
        compiler_params=_cp("parallel"),
        name="mm",
    )(x, w)


def _proj_kernel(x_ref, w_ref, c_ref, sa_ref, sb_ref, o_ref, *, block_modes, tm, tn):
    i = pl.program_id(0)
    j = pl.program_id(1)
    acc = _dot(x_ref[...], w_ref[...])

    def piece(mode, s0, ns):
        a = acc[:, s0 * LANES:(s0 + ns) * LANES]
        lane = lax.broadcasted_iota(jnp.int32, a.shape, 1) % LANES
        if mode in ("rope", "rope_onehot"):
            a = _rope(a, _tile_lanes(c_ref[...], ns), _tile_lanes(sa_ref[...], ns), _tile_lanes(sb_ref[...], ns))
        if mode == "rope_onehot":
            row = i * tm + lax.broadcasted_iota(jnp.int32, a.shape, 0)
            a = a + jnp.where(lane == HEAD_DIM + (row // SLC_BLOCK) % BLK_PER_KT, 1.0, 0.0)
        if mode == "ones":
            a = a + jnp.where(lane == HEAD_DIM, 1.0, 0.0)
        return a.astype(o_ref.dtype)

    for modes in sorted(set(block_modes)):
        pred = functools.reduce(jnp.logical_or, [j == jj for jj, mm in enumerate(block_modes) if mm == modes])

        @pl.when(pred)
        def _(modes=modes):
            runs = []
            for s, mode in enumerate(modes):
                if runs and runs[-1][0] == mode:
                    runs[-1][2] += 1
                else:
                    runs.append([mode, s, 1])
            pieces = [piece(*r) for r in runs]
            o_ref[...] = pieces[0] if len(pieces) == 1 else jnp.concatenate(pieces, axis=1)


def _proj(x, w, tabs, slab_modes, tn=1024, tm=1024):
    m, k = x.shape
    n = w.shape[1]
    tn = min(tn, n)
    spb = tn // LANES
    block_modes = tuple(tuple(slab_modes[jj * spb:(jj + 1) * spb]) for jj in range(n // tn))
    c, sa, sb = tabs
    return pl.pallas_call(
        functools.partial(_proj_kernel, block_modes=block_modes, tm=tm, tn=tn),
        grid=(m // tm, n // tn),
        in_specs=[
            pl.BlockSpec((tm, k), lambda i, j: (i, 0)),
            pl.BlockSpec((k, tn), lambda i, j: (0, j)),
            pl.BlockSpec((tm, LANES), lambda i, j: (i, 0)),
            pl.BlockSpec((tm, LANES), lambda i, j: (i, 0)),
            pl.BlockSpec((tm, LANES), lambda i, j: (i, 0)),
        ],
        out_specs=pl.BlockSpec((tm, tn), lambda i, j: (i, j)),
        out_shape=jax.ShapeDtypeStruct((m, n), BF16),
        compiler_params=_cp("parallel", "arbitrary"),
        name="proj",
    )(x, w, c, sa, sb)


def _out_ln_kernel(x_ref, w_ref, h_ref, g_ref, b_ref, o_ref, ob_ref):
    mix = _dot(x_ref[...], w_ref[...])
    y = _layer_norm(ALPHA * h_ref[...] + mix, g_ref[...], b_ref[...])
    o_ref[...] = y
    ob_ref[...] = y.astype(BF16)


def _dual_out(m, d, tm):
    spec = pl.BlockSpec((tm, d), lambda i, *_: (i, 0))
    return [spec, spec], [jax.ShapeDtypeStruct((m, d), F32), jax.ShapeDtypeStruct((m, d), BF16)]


def _out_ln(x, w, h, g, b, tm=512):
    m, k = x.shape
    d = w.shape[1]
    out_specs, out_shape = _dual_out(m, d, tm)
    return pl.pallas_call(
        _out_ln_kernel,
        grid=(m // tm,),
        in_specs=[
            pl.BlockSpec((tm, k), lambda i: (i, 0)),
            pl.BlockSpec((k, d), lambda i: (0, 0)),
            pl.BlockSpec((tm, d), lambda i: (i, 0)),
            pl.BlockSpec((1, d), lambda i: (0, 0)),
            pl.BlockSpec((1, d), lambda i: (0, 0)),
        ],
        out_specs=out_specs,
        out_shape=out_shape,
        compiler_params=_cp("parallel"),
        name="out_ln",
    )(x, w, h, g, b)


def _ffn_kernel(x_ref, wg_ref, wu_ref, wo_ref, g_ref, b_ref, o_ref, ob_ref, acc_ref):
    c = pl.program_id(1)

    @pl.when(c == 0)
    def _():
        acc_ref[...] = jnp.zeros_like(acc_ref)

    xb = x_ref[...].astype(BF16)
    gate = _dot(xb, wg_ref[...])
    up = _dot(xb, wu_ref[...])
    a = gate * _sigmoid(gate) * up
    acc_ref[...] += _dot(a.astype(BF16), wo_ref[...])

    @pl.when(c == pl.num_programs(1) - 1)
    def _():
        y = _layer_norm(ALPHA * x_ref[...] + acc_ref[...], g_ref[...], b_ref[...])
        o_ref[...] = y
        ob_ref[...] = y.astype(BF16)


def _ffn_dense(h, w_in, w_out, g, b, tm=512, fc=1408):
    m, d = h.shape
    ff = w_out.shape[0]
    nc = ff // fc
    out_specs, out_shape = _dual_out(m, d, tm)
    return pl.pallas_call(
        _ffn_kernel,
        grid=(m // tm, nc),
        in_specs=[
            pl.BlockSpec((tm, d), lambda i, c: (i, 0)),
            pl.BlockSpec((d, fc), lambda i, c: (0, c)),
            pl.BlockSpec((d, fc), lambda i, c: (0, nc + c)),
            pl.BlockSpec((fc, d), lambda i, c: (c, 0)),
            pl.BlockSpec((1, d), lambda i, c: (0, 0)),
            pl.BlockSpec((1, d), lambda i, c: (0, 0)),
        ],
        out_specs=out_specs,
        out_shape=out_shape,
        scratch_shapes=[pltpu.VMEM((tm, d), F32)],
        compiler_params=_cp("parallel", "arbitrary"),
        name="ffn_dense",
    )(h, w_in, w_in, w_out, g, b)


def _compress_kernel(p_ref, q_ref, pos_ref, w1_ref, w2_ref, o_ref):
    posb = _dot(pos_ref[...], w1_ref[...])[0:1, :]
    hid = p_ref[...] + q_ref[...] + posb
    act = jax.nn.gelu(hid, approximate=True)
    o_ref[...] = _dot(act.astype(BF16), w2_ref[...])


def _compress(pp, qq, pos8, w1, w2, tm=1024):
    m, hid = pp.shape
    dh = w2.shape[1]
    return pl.pallas_call(
        _compress_kernel,
        grid=(m // tm,),
        in_specs=[
            pl.BlockSpec((tm, hid), lambda i: (i, 0)),
            pl.BlockSpec((tm, hid), lambda i: (i, 0)),
            pl.BlockSpec(pos8.shape, lambda i: (0, 0)),
            pl.BlockSpec(w1.shape, lambda i: (0, 0)),
            pl.BlockSpec(w2.shape, lambda i: (0, 0)),
        ],
        out_specs=pl.BlockSpec((tm, dh), lambda i: (i, 0)),
        out_shape=jax.ShapeDtypeStruct((m, dh), F32),
        compiler_params=_cp("parallel"),
        name="compress",
    )(pp, qq, pos8, w1, w2)


def _cmp_select_kernel(q_ref, kc_ref, vc_ref, mapt_ref, oc_ref, sb_ref, imp_ref, *, ncp, nslc):
    qi = pl.program_id(2)
    t0 = qi * TQA
    rows = A_GROUP * TQA
    qb = q_ref[...]
    q4 = jnp.concatenate([qb[:, r * LANES:(r + 1) * LANES] for r in range(A_GROUP)], axis=0)

    def branch(nk):
        s = _dot_nt(q4, kc_ref[0:nk, :])
        t_tok = t0 + lax.broadcasted_iota(jnp.int32, (TQA, nk), 0)
        n_idx = lax.broadcasted_iota(jnp.int32, (TQA, nk), 1)
        valid = (n_idx * CMP_STRIDE + CMP_BLOCK - 1 <= t_tok) & (n_idx < ncp - 1)
        vmask = jnp.where(valid, 1.0, 0.0)
        s3 = s.reshape(A_GROUP, TQA, nk) + jnp.where(valid, 0.0, NEG)[None]
        m = jnp.max(s3, axis=2, keepdims=True)
        p = jnp.exp(s3 - m) * vmask[None]
        l = jnp.sum(p, axis=2, keepdims=True)
        pn = p * jnp.where(l > 0.0, 1.0 / l, 0.0)
        oc = _dot(pn.reshape(rows, nk).astype(BF16), vc_ref[0:nk, :])
        oc_ref[...] = oc[:, :HEAD_DIM].reshape(A_GROUP, TQA, HEAD_DIM)
        psum = pn[0] + pn[1] + pn[2] + pn[3]
        p_hi = psum.astype(BF16)
        p_lo = (psum - p_hi.astype(F32)).astype(BF16)
        mapt = mapt_ref[:, 0:nk]
        imp_ref[...] = _dot_nt(mapt, p_hi) + _dot_nt(mapt, p_lo)

    n_buckets = ncp // CMP_CHUNK
    bucket = ((qi + 1) * (TQA // CMP_STRIDE) - 1) // CMP_CHUNK
    for nb in range(n_buckets):
        pl.when(bucket == nb)(functools.partial(branch, (nb + 1) * CMP_CHUNK))
    imp = imp_ref[...]

    j_idx = lax.broadcasted_iota(jnp.int32, (nslc, TQA), 0)
    cur = (t0 + lax.broadcasted_iota(jnp.int32, (nslc, TQA), 1)) // SLC_BLOCK
    forced = (j_idx == 0) | (j_idx == cur) | (j_idx == cur - 1)
    score = jnp.where(j_idx > cur, -1.0, jnp.where(forced, -jnp.inf, imp))
    for _ in range(N_SEL - N_FORCED):
        mx = jnp.max(score, axis=0, keepdims=True)
        first = jnp.min(jnp.where(score == mx, j_idx, nslc), axis=0, keepdims=True)
        score = jnp.where(j_idx == first, -jnp.inf, score)
    sb_ref[...] = jnp.where((score == -jnp.inf) & (j_idx <= cur), 0.0, NEG).T


def _cmp_select(proj, kc_cmp, vc_cmp, cmp_map_t, bsz, seq):
    ncp = kc_cmp.shape[2]
    nslc = seq // SLC_BLOCK
    nq = seq // TQA
    g_ = A_KV_HEADS
    assert nslc >= N_SEL and ncp % CMP_CHUNK == 0
    return pl.pallas_call(
        functools.partial(_cmp_select_kernel, ncp=ncp, nslc=nslc),
        grid=(bsz, g_, nq),
        in_specs=[
            pl.BlockSpec((TQA, A_GROUP * LANES), lambda b, g, i: (b * nq + i, g)),
            pl.BlockSpec((None, None, ncp, LANES), lambda b, g, i: (b, g, 0, 0)),
            pl.BlockSpec((None, None, ncp, LANES), lambda b, g, i: (b, g, 0, 0)),
            pl.BlockSpec((nslc, ncp), lambda b, g, i: (0, 0)),
        ],
        out_specs=[
            pl.BlockSpec((None, None, A_GROUP, TQA, HEAD_DIM), lambda b, g, i: (b, g, 0, i, 0)),
            pl.BlockSpec((TQA, nslc), lambda b, g, i: (b * nq + i, g)),
        ],
        out_shape=[
            jax.ShapeDtypeStruct((bsz, g_, A_GROUP, seq, HEAD_DIM), F32),
            jax.ShapeDtypeStruct((bsz * seq, g_ * nslc), F32),
        ],
        scratch_shapes=[pltpu.VMEM((nslc, TQA), F32)],
        compiler_params=_cp("parallel", "parallel", "arbitrary"),
        name="cmp_select",
    )(proj, kc_cmp, vc_cmp, cmp_map_t)


def _sel_win_kernel(q_ref, c_ref, sa_ref, sb_ref, bias_ref, ks_ref, vs_ref, kw_ref, vw_ref, oc_ref, gate_ref,
                    o_ref, *, nslc):
    qi = pl.program_id(2)
    t0 = qi * TQA
    rows = A_GROUP * TQA
    qb = q_ref[...]
    q4 = jnp.concatenate([qb[:, r * LANES:(r + 1) * LANES] for r in range(A_GROUP)], axis=0).astype(F32)
    q_rot = _rope(q4, _tile_rows(c_ref[...], A_GROUP), _tile_rows(sa_ref[...], A_GROUP),
                  _tile_rows(sb_ref[...], A_GROUP)) * LOG2E
    q_rot_b = q_rot.astype(BF16)
    bias = bias_ref[...]
    lane = lax.broadcasted_iota(jnp.int32, (TQA, LANES), 1)
    bias_lanes = (lane >= HEAD_DIM) & (lane < HEAD_DIM + BLK_PER_KT)
    t_q = t0 + lax.broadcasted_iota(jnp.int32, (TQA, KT), 0)
    k_off = lax.broadcasted_iota(jnp.int32, (TQA, KT), 1)

    def add_mask(s, mask_bias):
        return (s.reshape(A_GROUP, TQA, -1) + mask_bias[None]).reshape(s.shape)

    def scores(kt, causal):
        shift = (HEAD_DIM + nslc - BLK_PER_KT * kt) % nslc
        rolled = pltpu.roll(bias, shift, 1)[:, :LANES]
        qa = (q_rot + _tile_rows(jnp.where(bias_lanes, rolled, 0.0), A_GROUP)).astype(BF16)
        k0 = pl.multiple_of(kt * KT, KT)
        s = _dot_nt(qa, ks_ref[pl.ds(k0, KT), :])
        if causal:
            s = add_mask(s, jnp.where(k0 + k_off <= t_q, 0.0, NEG))
        return s

    def pair(j, carry, causal):
        m, acc = carry
        s0 = scores(2 * j, causal)
        s1 = scores(2 * j + 1, causal)
        m_new = jnp.maximum(m, jnp.maximum(jnp.max(s0, axis=1, keepdims=True), jnp.max(s1, axis=1, keepdims=True)))
        p0 = jnp.exp2(s0 - m_new).astype(BF16)
        p1 = jnp.exp2(s1 - m_new).astype(BF16)
        k0 = pl.multiple_of(2 * j * KT, 2 * KT)
        pv = _dot(p0, vs_ref[pl.ds(k0, KT), :]) + _dot(p1, vs_ref[pl.ds(k0 + KT, KT), :])
        return m_new, jnp.exp2(m - m_new) * acc + pv

    n_pairs = t0 // (2 * KT) + 1
    init = (jnp.full((rows, 1), NEG, F32), jnp.zeros((rows, LANES), F32))
    carry = lax.fori_loop(0, n_pairs - 1, lambda j, c: pair(j, c, False), init)
    _, acc_s = pair(n_pairs - 1, carry, True)
    o_s = acc_s[:, :HEAD_DIM] / acc_s[:, HEAD_DIM:HEAD_DIM + 1]

    wspan = NSA_WIN + TQA
    w0 = pl.multiple_of(jnp.maximum(t0 - NSA_WIN, 0), TQA)
    s_w = _dot_nt(q_rot_b, kw_ref[pl.ds(w0, wspan), :])
    tw = t0 + lax.broadcasted_iota(jnp.int32, (TQA, wspan), 0)
    diff = tw - (w0 + lax.broadcasted_iota(jnp.int32, (TQA, wspan), 1))
    s_w = add_mask(s_w, jnp.where((diff >= 0) & (diff < NSA_WIN), 0.0, NEG))
    p_w = jnp.exp2(s_w - jnp.max(s_w, axis=1, keepdims=True))
    acc_w = _dot(p_w.astype(BF16), vw_ref[pl.ds(w0, wspan), :])
    o_w = acc_w[:, :HEAD_DIM] / acc_w[:, HEAD_DIM:HEAD_DIM + 1]

    gates = gate_ref[...]

    def gcol(c):
        return jnp.concatenate([gates[:, c * A_GROUP + r:c * A_GROUP + r + 1] for r in range(A_GROUP)], axis=0)

    o_c = oc_ref[...].reshape(rows, HEAD_DIM)
    o = gcol(0) * o_c + gcol(1) * o_s + gcol(2) * o_w
    o_ref[...] = _heads_to_lanes(o, A_GROUP).astype(o_ref.dtype)


def _sel_win(proj, tabs, selbias, o_c, gates, bsz, seq):
    nslc = seq // SLC_BLOCK
    nq = seq // TQA
    g_ = A_KV_HEADS
    c, sa, sb = tabs
    tab_spec = pl.BlockSpec((TQA, LANES), lambda b, g, i: (b * nq + i, 0))
    q_slabs = A_HEADS

    def kv_spec(base):
        return pl.BlockSpec((seq, LANES), lambda b, g, i: (b, base + g))

    return pl.pallas_call(
        functools.partial(_sel_win_kernel, nslc=nslc),
        grid=(bsz, g_, nq),
        in_specs=[
            pl.BlockSpec((TQA, A_GROUP * LANES), lambda b, g, i: (b * nq + i, g)),
            tab_spec, tab_spec, tab_spec,
            pl.BlockSpec((TQA, nslc), lambda b, g, i: (b * nq + i, g)),
            kv_spec(q_slabs), kv_spec(q_slabs + g_), kv_spec(q_slabs + 2 * g_), kv_spec(q_slabs + 3 * g_),
            pl.BlockSpec((None, None, A_GROUP, TQA, HEAD_DIM), lambda b, g, i: (b, g, 0, i, 0)),
            pl.BlockSpec((TQA, LANES), lambda b, g, i: (b * nq + i, g)),
        ],
        out_specs=pl.BlockSpec((TQA, A_GROUP * HEAD_DIM), lambda b, g, i: (b * nq + i, g)),
        out_shape=jax.ShapeDtypeStruct((bsz * seq, A_HEADS * HEAD_DIM), BF16),
        compiler_params=_cp("parallel", "parallel", "arbitrary"),
        name="sel_win",
    )(proj, c, sa, sb, selbias, proj, proj, proj, proj, o_c, gates)


def _swa_kernel(q_ref, k_ref, v_ref, sink_ref, o_ref):
    qi = pl.program_id(2)
    t0 = qi * TQB
    rows = B_GROUP * TQB
    span = SWA_WIN + TQB
    qb = q_ref[...]
    q8 = jnp.concatenate([qb[:, r * LANES:(r + 1) * LANES] for r in range(B_GROUP)], axis=0)
    w0 = pl.multiple_of(jnp.maximum(t0 - SWA_WIN, 0), SWA_WIN)
    s = _dot_nt(q8, k_ref[pl.ds(w0, span), :])
    tq = t0 + lax.broadcasted_iota(jnp.int32, (TQB, span), 0)
    diff = tq - (w0 + lax.broadcasted_iota(jnp.int32, (TQB, span), 1))
    band = jnp.where((diff >= 0) & (diff < SWA_WIN), 0.0, NEG)
    s = (s.reshape(B_GROUP, TQB, span) + band[None]).reshape(rows, span)
    sk = sink_ref[...]
    sink = jnp.concatenate([jnp.broadcast_to(sk[r:r + 1, 0:1], (TQB, 1)) for r in range(B_GROUP)], axis=0)
    m = jnp.maximum(jnp.max(s, axis=1, keepdims=True), sink)
    e = jnp.exp(s - m)
    acc = _dot(e.astype(BF16), v_ref[pl.ds(w0, span), :])
    o = acc[:, :HEAD_DIM] / (acc[:, HEAD_DIM:HEAD_DIM + 1] + jnp.exp(sink - m))
    o_ref[...] = _heads_to_lanes(o, B_GROUP).astype(o_ref.dtype)


def _swa(qproj, kvproj, sinks, bsz, seq):
    nq = seq // TQB
    g_ = B_KV_HEADS
    return pl.pallas_call(
        _swa_kernel,
        grid=(bsz, g_, nq),
        in_specs=[
            pl.BlockSpec((TQB, B_GROUP * LANES), lambda b, g, i: (b * nq + i, g)),
            pl.BlockSpec((seq, LANES), lambda b, g, i: (b, g)),
            pl.BlockSpec((seq, LANES), lambda b, g, i: (b, g_ + g)),
            pl.BlockSpec((None, B_GROUP, LANES), lambda b, g, i: (g, 0, 0)),
        ],
        out_specs=pl.BlockSpec((TQB, B_GROUP * HEAD_DIM), lambda b, g, i: (b * nq + i, g)),
        out_shape=jax.ShapeDtypeStruct((bsz * seq, B_HEADS * HEAD_DIM), BF16),
        compiler_params=_cp("parallel", "parallel", "arbitrary"),
        name="swa",
    )(qproj, kvproj, kvproj, sinks)


def _router_kernel(x_ref, wh_ref, wl_ref, o_ref):
    x = x_ref[...]
    xh = x.astype(BF16)
    xl = (x - xh.astype(F32)).astype(BF16)
    logits = _dot(xh, wh_ref[...]) + _dot(xh, wl_ref[...]) + _dot(xl, wh_ref[...])
    lane = lax.broadcasted_iota(jnp.int32, logits.shape, 1)
    lg = jnp.where(lane < N_EXPERTS, logits, -jnp.inf)
    m1 = jnp.max(lg, axis=1, keepdims=True)
    i1 = jnp.min(jnp.where(lg == m1, lane, LANES), axis=1, keepdims=True)
    lg2 = jnp.where(lane == i1, -jnp.inf, lg)
    m2 = jnp.max(lg2, axis=1, keepdims=True)
    i2 = jnp.min(jnp.where(lg2 == m2, lane, LANES), axis=1, keepdims=True)
    e2 = jnp.exp(m2 - m1)
    g1 = 1.0 / (1.0 + e2)
    g2 = e2 * g1
    out = jnp.where(lane == 0, i1.astype(F32), jnp.where(lane == 1, i2.astype(F32),
                    jnp.where(lane == 2, g1, jnp.where(lane == 3, g2, 0.0))))
    o_ref[...] = out


def _router(h, w_hi, w_lo, tm=512):
    m, d = h.shape
    return pl.pallas_call(
        _router_kernel,
        grid=(m // tm,),
        in_specs=[pl.BlockSpec((tm, d), lambda i: (i, 0)), pl.BlockSpec((d, LANES), lambda i: (0, 0)),
                  pl.BlockSpec((d, LANES), lambda i: (0, 0))],
        out_specs=pl.BlockSpec((tm, LANES), lambda i: (i, 0)),
        out_shape=jax.ShapeDtypeStruct((m, LANES), F32),
        compiler_params=_cp("parallel"),
        name="router",
    )(h, w_hi, w_lo)


def _moe_ffn_kernel(blk_e_ref, x_ref, wg_ref, wu_ref, wo_ref, o_ref, acc_ref):
    c = pl.program_id(1)

    @pl.when(c == 0)
    def _():
        acc_ref[...] = jnp.zeros_like(acc_ref)

    xb = x_ref[...]
    gate = _dot(xb, wg_ref[...])
    up = _dot(xb, wu_ref[...])
    a = gate * _sigmoid(gate) * up
    acc_ref[...] += _dot(a.astype(BF16), wo_ref[...])

    @pl.when(c == pl.num_programs(1) - 1)
    def _():
        o_ref[...] = acc_ref[...]


def _moe_ffn(xs, blk_e, w_in, w_out, tm=MOE_TM, fc=896):
    cap, d = xs.shape
    ff = w_out.shape[1]
    nc = ff // fc
    grid_spec = pltpu.PrefetchScalarGridSpec(
        num_scalar_prefetch=1,
        grid=(cap // tm, nc),
        in_specs=[
            pl.BlockSpec((tm, d), lambda i, c, e: (i, 0)),
            pl.BlockSpec((None, d, fc), lambda i, c, e: (e[i], 0, c)),
            pl.BlockSpec((None, d, fc), lambda i, c, e: (e[i], 0, nc + c)),
            pl.BlockSpec((None, fc, d), lambda i, c, e: (e[i], c, 0)),
        ],
        out_specs=pl.BlockSpec((tm, d), lambda i, c, e: (i, 0)),
        scratch_shapes=[pltpu.VMEM((tm, d), F32)],
    )
    return pl.pallas_call(
        _moe_ffn_kernel,
        grid_spec=grid_spec,
        out_shape=jax.ShapeDtypeStruct((cap, d), F32),
        compiler_params=_cp("parallel", "arbitrary"),
        name="moe_ffn",
    )(blk_e, xs, w_in, w_in, w_out)


def _combine_ln_kernel(h_ref, y1_ref, y2_ref, r_ref, g_ref, b_ref, o_ref, ob_ref):
    r = r_ref[...]
    ffn = r[:, 2:3] * y1_ref[...] + r[:, 3:4] * y2_ref[...]
    y = _layer_norm(ALPHA * h_ref[...] + ffn, g_ref[...], b_ref[...])
    o_ref[...] = y
    ob_ref[...] = y.astype(BF16)


def _combine_ln(h, y1, y2, route, g, b, tm=512):
    m, d = h.shape
    row = pl.BlockSpec((tm, d), lambda i: (i, 0))
    vec = pl.BlockSpec((1, d), lambda i: (0, 0))
    out_specs, out_shape = _dual_out(m, d, tm)
    return pl.pallas_call(
        _combine_ln_kernel,
        grid=(m // tm,),
        in_specs=[row, row, row, pl.BlockSpec((tm, LANES), lambda i: (i, 0)), vec, vec],
        out_specs=out_specs,
        out_shape=out_shape,
        compiler_params=_cp("parallel"),
        name="combine_ln",
    )(h, y1, y2, route, g, b)


def _pad_heads(w, heads):
    d = w.shape[0]
    w = w.reshape(d, heads, HEAD_DIM)
    return jnp.pad(w, ((0, 0), (0, 0), (0, LANES - HEAD_DIM))).reshape(d, heads * LANES)


def _rope_tables(positions):
    half = ROT_DIM // 2
    inv = ROPE_THETA ** (-jnp.arange(0, ROT_DIM, 2, dtype=F32) / ROT_DIM)
    ang = positions.astype(F32).reshape(-1, 1) * inv
    cos, sin = jnp.cos(ang), jnp.sin(ang)
    n = ang.shape[0]
    ones = jnp.ones((n, LANES - ROT_DIM), F32)
    zeros = jnp.zeros((n, LANES - half), F32)
    c = jnp.concatenate([cos, cos, ones], axis=1)
    sa = jnp.concatenate([-sin, zeros], axis=1)
    sb = jnp.concatenate([jnp.zeros((n, half), F32), sin, jnp.zeros((n, LANES - ROT_DIM), F32)], axis=1)
    return c, sa, sb


def _cmp_to_slc(ncp, nslc):
    n = np.arange(ncp)[:, None]
    j = np.arange(nslc)[None, :]
    overlap = (np.minimum(n * CMP_STRIDE + CMP_BLOCK, j * SLC_BLOCK + SLC_BLOCK)
               - np.maximum(n * CMP_STRIDE, j * SLC_BLOCK))
    m = np.clip(overlap, 0, None).astype(np.float32) / CMP_BLOCK
    m[ncp - 1, :] = 0.0
    return jnp.asarray(m, BF16)


def _compress_branch(t, pos, w1, w2, bsz, seq):
    g_ = A_KV_HEADS
    nch = seq // CMP_STRIDE
    half = CMP_STRIDE * HEAD_DIM
    a = t.reshape(bsz, nch, CMP_STRIDE, g_, HEAD_DIM).transpose(0, 1, 3, 2, 4).reshape(bsz * nch * g_, half)
    w1b = w1.astype(BF16)
    w1cat = jnp.concatenate([w1b[:half], w1b[half:]], axis=1)
    pq = _mm(a, w1cat, F32, tm=1024).reshape(bsz, nch, g_, 2 * CMP_HID)
    pp = pq[..., :CMP_HID].reshape(-1, CMP_HID)
    qq = jnp.concatenate([pq[:, 1:, :, CMP_HID:], jnp.zeros((bsz, 1, g_, CMP_HID), F32)], axis=1).reshape(-1, CMP_HID)
    pos8 = jnp.zeros((8, CMP_BLOCK * HEAD_DIM), BF16).at[0].set(pos.reshape(-1).astype(BF16))
    out = _compress(pp, qq, pos8, w1b, w2.astype(BF16))
    out = out.reshape(bsz, nch, g_, HEAD_DIM).transpose(0, 2, 1, 3)
    return out


def _nsa_layer(h, hb, tabs, w_in, w_out, cmp_pos, cmp_w1, cmp_w2, ln_g, ln_b, bsz, seq):
    o, _, _ = _nsa_attn(hb, tabs, w_in, cmp_pos, cmp_w1, cmp_w2, bsz, seq)
    return _out_ln(o, w_out.astype(BF16), h, ln_g[None, :], ln_b[None, :])


def _nsa_attn(hb, tabs, w_in, cmp_pos, cmp_w1, cmp_w2, bsz, seq):
    hb = hb.astype(BF16)
    g_ = A_KV_HEADS
    aq = A_HEADS * HEAD_DIM
    akv = g_ * HEAD_DIM
    wq = w_in[:, :aq] * (HEAD_DIM ** -0.5)
    w_kc, w_vc, w_ks, w_vs, w_kw, w_vw = (w_in[:, aq + i * akv: aq + (i + 1) * akv] for i in range(6))
    w_gl = w_in[:, aq + 6 * akv:]
    w_big = jnp.concatenate([_pad_heads(wq, A_HEADS), _pad_heads(w_ks, g_), _pad_heads(w_vs, g_),
                             _pad_heads(w_kw, g_), _pad_heads(w_vw, g_)], axis=1).astype(BF16)
    slab_modes = ["plain"] * A_HEADS + ["rope_onehot"] * g_ + ["ones"] * g_ + ["rope"] * g_ + ["ones"] * g_
    proj = _proj(hb, w_big, tabs, slab_modes)

    w_gl_g = w_gl.reshape(-1, 3, g_, A_GROUP).transpose(0, 2, 1, 3).reshape(-1, g_, 3 * A_GROUP)
    w_gl_g = jnp.pad(w_gl_g, ((0, 0), (0, 0), (0, LANES - 3 * A_GROUP))).reshape(-1, g_ * LANES)
    side = _mm(hb, jnp.concatenate([w_gl_g, w_kc, w_vc], axis=1).astype(BF16), F32, sig_cols=g_ * LANES)
    gates = side[:, :g_ * LANES]
    kc = _compress_branch(side[:, g_ * LANES:g_ * LANES + akv], cmp_pos[0], cmp_w1[0], cmp_w2[0], bsz, seq)
    vc = _compress_branch(side[:, g_ * LANES + akv:], cmp_pos[1], cmp_w1[1], cmp_w2[1], bsz, seq)
    pad = ((0, 0), (0, 0), (0, 0), (0, LANES - HEAD_DIM))
    kc = jnp.pad(kc, pad).astype(BF16)
    vc = jnp.pad(vc, pad).astype(BF16)
    ncp = seq // CMP_STRIDE
    cmp_map_t = _cmp_to_slc(ncp, seq // SLC_BLOCK).T

    o_c, selbias = _cmp_select(proj, kc, vc, cmp_map_t, bsz, seq)
    o = _sel_win(proj, tabs, selbias, o_c, gates, bsz, seq)
    return o, selbias, o_c


def _shared_kv(hb, tabs, w_kv):
    g_ = B_KV_HEADS
    bkv = g_ * HEAD_DIM
    w = jnp.concatenate([_pad_heads(w_kv[:, :bkv], g_), _pad_heads(w_kv[:, bkv:], g_)], axis=1).astype(BF16)
    return _proj(hb.astype(BF16), w, tabs, ["rope"] * g_ + ["ones"] * g_)


def _swa_layer(h, hb, tabs, kvproj, w_q, w_out, sinks, ln_g, ln_b, bsz, seq):
    o = _swa_attn(hb, tabs, kvproj, w_q, sinks, bsz, seq)
    return _out_ln(o, w_out.astype(BF16), h, ln_g[None, :], ln_b[None, :])


def _swa_attn(hb, tabs, kvproj, w_q, sinks, bsz, seq):
    wq = _pad_heads(w_q * (HEAD_DIM ** -0.5), B_HEADS).astype(BF16)
    qproj = _proj(hb.astype(BF16), wq, tabs, ["rope"] * B_HEADS)
    sk = jnp.broadcast_to(sinks.astype(F32).reshape(B_KV_HEADS, B_GROUP, 1), (B_KV_HEADS, B_GROUP, LANES))
    return _swa(qproj, kvproj, sk, bsz, seq)


def _moe_layer(h, hb, w_router, w_in, w_out, ln_g, ln_b):
    y1, y2, route = _moe_experts(h, hb, w_router, w_in, w_out)
    return _combine_ln(h, y1, y2, route, ln_g[None, :], ln_b[None, :])


def _moe_ffn_out(h, w_router, w_in, w_out):
    y1, y2, route = _moe_experts(h, h.astype(BF16), w_router, w_in, w_out)
    return route[:, 2:3] * y1 + route[:, 3:4] * y2


def _moe_experts(h, hb, w_router, w_in, w_out):
    n_tok, d = h.shape
    wr = jnp.pad(w_router, ((0, 0), (0, LANES - N_EXPERTS)))
    wr_hi = wr.astype(BF16)
    wr_lo = (wr - wr_hi.astype(F32)).astype(BF16)
    route = _router(h, wr_hi, wr_lo)
    top_e = route[:, :2].astype(jnp.int32)

    e_flat = top_e.reshape(-1)
    onehot = (e_flat[:, None] == jnp.arange(N_EXPERTS)[None, :]).astype(jnp.int32)
    csum = jnp.cumsum(onehot, axis=0)
    counts = csum[-1]
    rank = jnp.take_along_axis(csum, e_flat[:, None], axis=1)[:, 0] - 1
    padded = (counts + MOE_TM - 1) // MOE_TM * MOE_TM
    pad_end = jnp.cumsum(padded)
    pad_start = pad_end - padded
    dest = pad_start[e_flat] + rank
    cap = n_tok * 2 + N_EXPERTS * MOE_TM
    n_blk = cap // MOE_TM
    tok_flat = jnp.repeat(jnp.arange(n_tok, dtype=jnp.int32), 2)
    buf_tok = jnp.zeros((cap,), jnp.int32).at[dest].set(tok_flat)
    blk_start = jnp.arange(n_blk, dtype=jnp.int32) * MOE_TM
    blk_e = jnp.sum((pad_end[None, :] <= blk_start[:, None]).astype(jnp.int32), axis=1)
    blk_e = jnp.minimum(blk_e, N_EXPERTS - 1).astype(jnp.int32)

    xs = hb[buf_tok]
    y = _moe_ffn(xs, blk_e, w_in.astype(BF16), w_out.astype(BF16))
    dest2 = dest.reshape(n_tok, 2)
    y1 = y[dest2[:, 0]]
    y2 = y[dest2[:, 1]]
    return y1, y2, route


def kernel(x, positions, w_in_a, w_out_a, cmp_pos, cmp_w1, cmp_w2, w_kv_shared, w_q_b, w_out_b, sinks_b,
           ln_g, ln_b, dense_w_in, dense_w_out, moe_router, moe_w_in, moe_w_out):
    bsz, seq, d = x.shape
    n_a = DEPTH // 2
    tabs = _rope_tables(positions)
    h = x.reshape(bsz * seq, d)
    hb = h.astype(BF16)
    kvproj = None
    for l in range(DEPTH):
        if l < n_a:
            h, hb = _nsa_layer(h, hb, tabs, w_in_a[l], w_out_a[l], cmp_pos[l], cmp_w1[l], cmp_w2[l],
                               ln_g[l, 0], ln_b[l, 0], bsz, seq)
        else:
            b = l - n_a
            h, hb = _swa_layer(h, hb, tabs, kvproj, w_q_b[b], w_out_b[b], sinks_b[b], ln_g[l, 0], ln_b[l, 0],
                               bsz, seq)
        if l % 2 == 0:
            h, hb = _ffn_dense(h, dense_w_in[l // 2].astype(BF16), dense_w_out[l // 2].astype(BF16),
                               ln_g[l, 1][None, :], ln_b[l, 1][None, :])
        else:
            h, hb = _moe_layer(h, hb, moe_router[l // 2], moe_w_in[l // 2], moe_w_out[l // 2],
                               ln_g[l, 1], ln_b[l, 1])
        if l == n_a - 1:
            kvproj = _shared_kv(hb, tabs, w_kv_shared)
    return h.reshape(bsz, seq, d)
```

```python
import functools

import numpy as np
import jax
import jax.numpy as jnp
from jax import lax
from jax.experimental import pallas as pl
from jax.experimental.pallas import tpu as pltpu

F32 = jnp.float32
BF16 = jnp.bfloat16

D_MODEL = 1024
DEPTH = 4
HEAD_DIM = 64
LANES = 128
ROT_DIM = HEAD_DIM // 4
ROPE_THETA = 500000.0
A_HEADS = 16
A_KV_HEADS = 4
A_GROUP = 4
CMP_BLOCK = 32
CMP_STRIDE = 16
CMP_HID = 256
SLC_BLOCK = 64
N_SEL = 16
N_FORCED = 3
NSA_WIN = 512
B_HEADS = 16
B_KV_HEADS = 2
B_GROUP = 8
SWA_WIN = 128
D_FF = 2816
N_EXPERTS = 8
D_FF_EXPERT = 3584
ALPHA = (2 * DEPTH) ** 0.25
LN_EPS = 1e-5
NEG = -1e30
FORCE = 1e9

TQB = 256
TQA = 256
CMP_CHUNK = 256
LOG2E = 1.4426950408889634
KT = 512
BLK_PER_KT = KT // SLC_BLOCK
KT_GROUP = 4
MOE_TM = 512
VMEM_LIMIT = 56 * 1024 * 1024


def _cp(*sem):
    return pltpu.CompilerParams(dimension_semantics=sem, vmem_limit_bytes=VMEM_LIMIT)


def _dot(a, b):
    return jnp.dot(a, b, preferred_element_type=F32)


def _dot_nt(a, b):
    return lax.dot_general(a, b, (((1,), (1,)), ((), ())), preferred_element_type=F32)


def _sigmoid(x):
    return 1.0 / (1.0 + jnp.exp(-x))


def _layer_norm(z, g, b):
    mu = jnp.mean(z, axis=-1, keepdims=True)
    zc = z - mu
    var = jnp.mean(zc * zc, axis=-1, keepdims=True)
    return zc * lax.rsqrt(var + LN_EPS) * g + b


def _rope(x, c, sa, sb):
    w = x.shape[1]
    return x * c + pltpu.roll(x, w - ROT_DIM // 2, 1) * sa + pltpu.roll(x, ROT_DIM // 2, 1) * sb


def _tile_lanes(t, n):
    return t if n == 1 else jnp.concatenate([t] * n, axis=1)


def _tile_rows(t, n):
    return t if n == 1 else jnp.concatenate([t] * n, axis=0)


def _normalize(acc, extra=0.0):
    den = pltpu.roll(acc, HEAD_DIM, 1) + extra
    return (acc / den)[:, :HEAD_DIM]


def _heads_to_lanes(o, heads):
    t = o.shape[0] // heads
    return jnp.concatenate([o[r * t:(r + 1) * t] for r in range(heads)], axis=1)


def _mm_kernel(x_ref, w_ref, o_ref, *, sig_cols):
    acc = _dot(x_ref[...].astype(BF16), w_ref[...])
    if sig_cols:
        o_ref[:, :sig_cols] = _sigmoid(acc[:, :sig_cols]).astype(o_ref.dtype)
        o_ref[:, sig_cols:] = acc[:, sig_cols:].astype(o_ref.dtype)
    else:
        o_ref[...] = acc.astype(o_ref.dtype)


def _mm(x, w, out_dtype, sig_cols=0, tm=512):
    m, k = x.shape
    n = w.shape[1]
    tm = min(tm, m)
    return pl.pallas_call(
        functools.partial(_mm_kernel, sig_cols=sig_cols),
        grid=(m // tm,),
        in_specs=[pl.BlockSpec((tm, k), lambda i: (i, 0)), pl.BlockSpec((k, n), lambda i: (0, 0))],
        out_specs=pl.BlockSpec((tm, n), lambda i: (i, 0)),
        out_shape=jax.ShapeDtypeStruct((m, n), out_dtype),
        compiler_params=_cp("parallel"),
        name="mm",
    )(x, w)


def _proj_kernel(x_ref, w_ref, c_ref, sa_ref, sb_ref, o_ref, *, block_modes, tm, tn):
    i = pl.program_id(0)
    j = pl.program_id(1)
    acc = _dot(x_ref[...], w_ref[...])

    def piece(mode, s0, ns):
        a = acc[:, s0 * LANES:(s0 + ns) * LANES]
        lane = lax.broadcasted_iota(jnp.int32, a.shape, 1) % LANES
        if mode in ("rope", "rope_onehot"):
            a = _rope(a, _tile_lanes(c_ref[...], ns), _tile_lanes(sa_ref[...], ns), _tile_lanes(sb_ref[...], ns))
        if mode == "rope_onehot":
            row = i * tm + lax.broadcasted_iota(jnp.int32, a.shape, 0)
            a = a + jnp.where(lane == HEAD_DIM + (row // SLC_BLOCK) % BLK_PER_KT, 1.0, 0.0)
        if mode == "ones":
            a = a + jnp.where(lane >= HEAD_DIM, 1.0, 0.0)
        return a.astype(o_ref.dtype)

    for modes in sorted(set(block_modes)):
        pred = functools.reduce(jnp.logical_or, [j == jj for jj, mm in enumerate(block_modes) if mm == modes])

        @pl.when(pred)
        def _(modes=modes):
            runs = []
            for s, mode in enumerate(modes):
                if runs and runs[-1][0] == mode:
                    runs[-1][2] += 1
                else:
                    runs.append([mode, s, 1])
            pieces = [piece(*r) for r in runs]
            o_ref[...] = pieces[0] if len(pieces) == 1 else jnp.concatenate(pieces, axis=1)


def _proj(x, w, tabs, slab_modes, tn=1024, tm=1024):
    m, k = x.shape
    n = w.shape[1]
    tn = min(tn, n)
    spb = tn // LANES
    block_modes = tuple(tuple(slab_modes[jj * spb:(jj + 1) * spb]) for jj in range(n // tn))
    c, sa, sb = tabs
    return pl.pallas_call(
        functools.partial(_proj_kernel, block_modes=block_modes, tm=tm, tn=tn),
        grid=(m // tm, n // tn),
        in_specs=[
            pl.BlockSpec((tm, k), lambda i, j: (i, 0)),
            pl.BlockSpec((k, tn), lambda i, j: (0, j)),
            pl.BlockSpec((tm, LANES), lambda i, j: (i, 0)),
            pl.BlockSpec((tm, LANES), lambda i, j: (i, 0)),
            pl.BlockSpec((tm, LANES), lambda i, j: (i, 0)),
        ],
        out_specs=pl.BlockSpec((tm, tn), lambda i, j: (i, j)),
        out_shape=jax.ShapeDtypeStruct((m, n), BF16),
        compiler_params=_cp("parallel", "arbitrary"),
        name="proj",
    )(x, w, c, sa, sb)


def _out_ln_kernel(x_ref, w_ref, h_ref, g_ref, b_ref, o_ref, ob_ref):
    mix = _dot(x_ref[...], w_ref[...])
    y = _layer_norm(ALPHA * h_ref[...] + mix, g_ref[...], b_ref[...])
    o_ref[...] = y
    ob_ref[...] = y.astype(BF16)


def _dual_out(m, d, tm):
    spec = pl.BlockSpec((tm, d), lambda i, *_: (i, 0))
    return [spec, spec], [jax.ShapeDtypeStruct((m, d), F32), jax.ShapeDtypeStruct((m, d), BF16)]


def _out_ln(x, w, h, g, b, tm=512):
    m, k = x.shape
    d = w.shape[1]
    out_specs, out_shape = _dual_out(m, d, tm)
    return pl.pallas_call(
        _out_ln_kernel,
        grid=(m // tm,),
        in_specs=[
            pl.BlockSpec((tm, k), lambda i: (i, 0)),
            pl.BlockSpec((k, d), lambda i: (0, 0)),
            pl.BlockSpec((tm, d), lambda i: (i, 0)),
            pl.BlockSpec((1, d), lambda i: (0, 0)),
            pl.BlockSpec((1, d), lambda i: (0, 0)),
        ],
        out_specs=out_specs,
        out_shape=out_shape,
        compiler_params=_cp("parallel"),
        name="out_ln",
    )(x, w, h, g, b)


def _ffn_kernel(x_ref, wg_ref, wu_ref, wo_ref, g_ref, b_ref, o_ref, ob_ref, acc_ref):
    c = pl.program_id(1)

    @pl.when(c == 0)
    def _():
        acc_ref[...] = jnp.zeros_like(acc_ref)

    xb = x_ref[...].astype(BF16)
    gate = _dot(xb, wg_ref[...])
    up = _dot(xb, wu_ref[...])
    a = gate * _sigmoid(gate) * up
    acc_ref[...] += _dot(a.astype(BF16), wo_ref[...])

    @pl.when(c == pl.num_programs(1) - 1)
    def _():
        y = _layer_norm(ALPHA * x_ref[...] + acc_ref[...], g_ref[...], b_ref[...])
        o_ref[...] = y
        ob_ref[...] = y.astype(BF16)


def _ffn_dense(h, w_in, w_out, g, b, tm=512, fc=1408):
    m, d = h.shape
    ff = w_out.shape[0]
    nc = ff // fc
    out_specs, out_shape = _dual_out(m, d, tm)
    return pl.pallas_call(
        _ffn_kernel,
        grid=(m // tm, nc),
        in_specs=[
            pl.BlockSpec((tm, d), lambda i, c: (i, 0)),
            pl.BlockSpec((d, fc), lambda i, c: (0, c)),
            pl.BlockSpec((d, fc), lambda i, c: (0, nc + c)),
            pl.BlockSpec((fc, d), lambda i, c: (c, 0)),
            pl.BlockSpec((1, d), lambda i, c: (0, 0)),
            pl.BlockSpec((1, d), lambda i, c: (0, 0)),
        ],
        out_specs=out_specs,
        out_shape=out_shape,
        scratch_shapes=[pltpu.VMEM((tm, d), F32)],
        compiler_params=_cp("parallel", "arbitrary"),
        name="ffn_dense",
    )(h, w_in, w_in, w_out, g, b)


def _compress_kernel(p_ref, q_ref, pos_ref, w1_ref, w2_ref, o_ref):
    posb = _dot(pos_ref[...], w1_ref[...])[0:1, :]
    hid = p_ref[...] + q_ref[...] + posb
    act = jax.nn.gelu(hid, approximate=True)
    o_ref[...] = _dot(act.astype(BF16), w2_ref[...])


def _compress(pp, qq, pos8, w1, w2, tm=1024):
    m, hid = pp.shape
    dh = w2.shape[1]
    return pl.pallas_call(
        _compress_kernel,
        grid=(m // tm,),
        in_specs=[
            pl.BlockSpec((tm, hid), lambda i: (i, 0)),
            pl.BlockSpec((tm, hid), lambda i: (i, 0)),
            pl.BlockSpec(pos8.shape, lambda i: (0, 0)),
            pl.BlockSpec(w1.shape, lambda i: (0, 0)),
            pl.BlockSpec(w2.shape, lambda i: (0, 0)),
        ],
        out_specs=pl.BlockSpec((tm, dh), lambda i: (i, 0)),
        out_shape=jax.ShapeDtypeStruct((m, dh), F32),
        compiler_params=_cp("parallel"),
        name="compress",
    )(pp, qq, pos8, w1, w2)


def _cmp_select_kernel(q_ref, kc_ref, vc_ref, mapt_ref, oc_ref, sb_ref, imp_ref, *, ncp, nslc):
    qi = pl.program_id(2)
    t0 = qi * TQA
    rows = A_GROUP * TQA
    qb = q_ref[...]
    q4 = jnp.concatenate([qb[:, r * LANES:(r + 1) * LANES] for r in range(A_GROUP)], axis=0)

    def branch(nk):
        s = _dot_nt(q4, kc_ref[0:nk, :])
        t_tok = t0 + lax.broadcasted_iota(jnp.int32, (TQA, nk), 0)
        n_idx = lax.broadcasted_iota(jnp.int32, (TQA, nk), 1)
        valid = (n_idx * CMP_STRIDE + CMP_BLOCK - 1 <= t_tok) & (n_idx < ncp - 1)
        vmask = jnp.where(valid, 1.0, 0.0)
        s3 = s.reshape(A_GROUP, TQA, nk) + jnp.where(valid, 0.0, NEG)[None]
        m = jnp.max(s3, axis=2, keepdims=True)
        p = jnp.exp(s3 - m) * vmask[None]
        l = jnp.sum(p, axis=2, keepdims=True)
        pn = p * jnp.where(l > 0.0, 1.0 / l, 0.0)
        oc = _dot(pn.reshape(rows, nk).astype(BF16), vc_ref[0:nk, :])
        oc_ref[...] = oc[:, :HEAD_DIM].reshape(A_GROUP, TQA, HEAD_DIM)
        psum = pn[0] + pn[1] + pn[2] + pn[3]
        p_hi = psum.astype(BF16)
        p_lo = (psum - p_hi.astype(F32)).astype(BF16)
        mapt = mapt_ref[:, 0:nk]
        imp_ref[...] = _dot_nt(mapt, p_hi) + _dot_nt(mapt, p_lo)

    n_buckets = ncp // CMP_CHUNK
    bucket = ((qi + 1) * (TQA // CMP_STRIDE) - 1) // CMP_CHUNK
    for nb in range(n_buckets):
        pl.when(bucket == nb)(functools.partial(branch, (nb + 1) * CMP_CHUNK))
    imp = imp_ref[...]

    j_idx = lax.broadcasted_iota(jnp.int32, (nslc, TQA), 0)
    cur = (t0 + lax.broadcasted_iota(jnp.int32, (nslc, TQA), 1)) // SLC_BLOCK
    forced = (j_idx == 0) | (j_idx == cur) | (j_idx == cur - 1)
    score = jnp.where(j_idx > cur, -1.0, jnp.where(forced, -jnp.inf, imp))
    for _ in range(N_SEL - N_FORCED):
        mx = jnp.max(score, axis=0, keepdims=True)
        first = jnp.min(jnp.where(score == mx, j_idx, nslc), axis=0, keepdims=True)
        score = jnp.where(j_idx == first, -jnp.inf, score)
    sb_ref[...] = jnp.where((score == -jnp.inf) & (j_idx <= cur), 0.0, NEG).T


def _cmp_select(proj, kc_cmp, vc_cmp, cmp_map_t, bsz, seq):
    ncp = kc_cmp.shape[2]
    nslc = seq // SLC_BLOCK
    nq = seq // TQA
    g_ = A_KV_HEADS
    assert nslc >= N_SEL and ncp % CMP_CHUNK == 0
    return pl.pallas_call(
        functools.partial(_cmp_select_kernel, ncp=ncp, nslc=nslc),
        grid=(bsz, g_, nq),
        in_specs=[
            pl.BlockSpec((TQA, A_GROUP * LANES), lambda b, g, i: (b * nq + i, g)),
            pl.BlockSpec((None, None, ncp, LANES), lambda b, g, i: (b, g, 0, 0)),
            pl.BlockSpec((None, None, ncp, LANES), lambda b, g, i: (b, g, 0, 0)),
            pl.BlockSpec((nslc, ncp), lambda b, g, i: (0, 0)),
        ],
        out_specs=[
            pl.BlockSpec((None, None, A_GROUP, TQA, HEAD_DIM), lambda b, g, i: (b, g, 0, i, 0)),
            pl.BlockSpec((TQA, nslc), lambda b, g, i: (b * nq + i, g)),
        ],
        out_shape=[
            jax.ShapeDtypeStruct((bsz, g_, A_GROUP, seq, HEAD_DIM), F32),
            jax.ShapeDtypeStruct((bsz * seq, g_ * nslc), F32),
        ],
        scratch_shapes=[pltpu.VMEM((nslc, TQA), F32)],
        compiler_params=_cp("parallel", "parallel", "arbitrary"),
        name="cmp_select",
    )(proj, kc_cmp, vc_cmp, cmp_map_t)


def _sel_win_kernel(q_ref, c_ref, sa_ref, sb_ref, bias_ref, ks_ref, vs_ref, kw_ref, vw_ref, oc_ref, gate_ref,
                    o_ref, *, nslc):
    qi = pl.program_id(2)
    t0 = qi * TQA
    rows = A_GROUP * TQA
    qb = q_ref[...]
    q4 = jnp.concatenate([qb[:, r * LANES:(r + 1) * LANES] for r in range(A_GROUP)], axis=0).astype(F32)
    q_rot = _rope(q4, _tile_rows(c_ref[...], A_GROUP), _tile_rows(sa_ref[...], A_GROUP),
                  _tile_rows(sb_ref[...], A_GROUP)) * LOG2E
    q_rot_b = q_rot.astype(BF16)
    bias = bias_ref[...]
    lane = lax.broadcasted_iota(jnp.int32, (TQA, LANES), 1)
    bias_lanes = (lane >= HEAD_DIM) & (lane < HEAD_DIM + BLK_PER_KT)
    t_q = t0 + lax.broadcasted_iota(jnp.int32, (TQA, KT), 0)
    k_off = lax.broadcasted_iota(jnp.int32, (TQA, KT), 1)

    def add_mask(s, mask_bias):
        return (s.reshape(A_GROUP, TQA, -1) + mask_bias[None]).reshape(s.shape)

    def scores(kt):
        shift = (HEAD_DIM + nslc - BLK_PER_KT * kt) % nslc
        rolled = pltpu.roll(bias, shift, 1)[:, :LANES]
        qa = (q_rot + _tile_rows(jnp.where(bias_lanes, rolled, 0.0), A_GROUP)).astype(BF16)
        k0 = pl.multiple_of(kt * KT, KT)
        return _dot_nt(qa, ks_ref[pl.ds(k0, KT), :])

    def causal_bias(kt):
        return jnp.where(kt * KT + k_off <= t_q, 0.0, NEG)

    def group(kt0, n, carry, causal):
        m, acc = carry
        ss = [scores(kt0 + i) for i in range(n)]
        if causal:
            ss = [add_mask(s, causal_bias(kt0 + i)) for i, s in enumerate(ss)]
        m_row = functools.reduce(jnp.maximum, [jnp.max(s, axis=1, keepdims=True) for s in ss])
        m_new = jnp.maximum(m, jnp.broadcast_to(m_row, (rows, LANES)))
        m_keys = _tile_lanes(m_new, KT // LANES)
        pv = None
        for i, s in enumerate(ss):
            k0 = pl.multiple_of((kt0 + i) * KT, KT)
            d = _dot(jnp.exp2(s - m_keys).astype(BF16), vs_ref[pl.ds(k0, KT), :])
            pv = d if pv is None else pv + d
        return m_new, jnp.exp2(m - m_new) * acc + pv

    kt_d = t0 // KT
    n_quads = kt_d // KT_GROUP
    carry = (jnp.full((rows, LANES), NEG, F32), jnp.zeros((rows, LANES), F32))
    carry = lax.fori_loop(0, n_quads, lambda j, c: group(j * KT_GROUP, KT_GROUP, c, False), carry)
    carry = lax.fori_loop(n_quads * KT_GROUP, kt_d, lambda kt, c: group(kt, 1, c, False), carry)
    _, acc_s = group(kt_d, 1, carry, True)
    o_s = _normalize(acc_s)

    wspan = NSA_WIN + TQA
    w0 = pl.multiple_of(jnp.maximum(t0 - NSA_WIN, 0), TQA)
    s_w = _dot_nt(q_rot_b, kw_ref[pl.ds(w0, wspan), :])
    tw = t0 + lax.broadcasted_iota(jnp.int32, (TQA, wspan), 0)
    diff = tw - (w0 + lax.broadcasted_iota(jnp.int32, (TQA, wspan), 1))
    s_w = add_mask(s_w, jnp.where((diff >= 0) & (diff < NSA_WIN), 0.0, NEG))
    p_w = jnp.exp2(s_w - jnp.max(s_w, axis=1, keepdims=True))
    acc_w = _dot(p_w.astype(BF16), vw_ref[pl.ds(w0, wspan), :])
    o_w = _normalize(acc_w)

    gates = gate_ref[...]

    def gcol(c):
        return jnp.concatenate([gates[:, c * A_GROUP + r:c * A_GROUP + r + 1] for r in range(A_GROUP)], axis=0)

    o_c = oc_ref[...].reshape(rows, HEAD_DIM)
    o = gcol(0) * o_c + gcol(1) * o_s + gcol(2) * o_w
    o_ref[...] = _heads_to_lanes(o, A_GROUP).astype(o_ref.dtype)


def _sel_win(proj, tabs, selbias, o_c, gates, bsz, seq):
    nslc = seq // SLC_BLOCK
    nq = seq // TQA
    g_ = A_KV_HEADS
    c, sa, sb = tabs
    tab_spec = pl.BlockSpec((TQA, LANES), lambda b, g, i: (b * nq + i, 0))
    q_slabs = A_HEADS

    def kv_spec(base):
        return pl.BlockSpec((seq, LANES), lambda b, g, i: (b, base + g))

    return pl.pallas_call(
        functools.partial(_sel_win_kernel, nslc=nslc),
        grid=(bsz, g_, nq),
        in_specs=[
            pl.BlockSpec((TQA, A_GROUP * LANES), lambda b, g, i: (b * nq + i, g)),
            tab_spec, tab_spec, tab_spec,
            pl.BlockSpec((TQA, nslc), lambda b, g, i: (b * nq + i, g)),
            kv_spec(q_slabs), kv_spec(q_slabs + g_), kv_spec(q_slabs + 2 * g_), kv_spec(q_slabs + 3 * g_),
            pl.BlockSpec((None, None, A_GROUP, TQA, HEAD_DIM), lambda b, g, i: (b, g, 0, i, 0)),
            pl.BlockSpec((TQA, LANES), lambda b, g, i: (b * nq + i, g)),
        ],
        out_specs=pl.BlockSpec((TQA, A_GROUP * HEAD_DIM), lambda b, g, i: (b * nq + i, g)),
        out_shape=jax.ShapeDtypeStruct((bsz * seq, A_HEADS * HEAD_DIM), BF16),
        compiler_params=_cp("parallel", "parallel", "arbitrary"),
        name="sel_win",
    )(proj, c, sa, sb, selbias, proj, proj, proj, proj, o_c, gates)


def _swa_kernel(q_ref, k_ref, v_ref, sink_ref, o_ref):
    qi = pl.program_id(2)
    t0 = qi * TQB
    rows = B_GROUP * TQB
    span = SWA_WIN + TQB
    qb = q_ref[...]
    q8 = jnp.concatenate([qb[:, r * LANES:(r + 1) * LANES] for r in range(B_GROUP)], axis=0)
    w0 = pl.multiple_of(jnp.maximum(t0 - SWA_WIN, 0), SWA_WIN)
    s = _dot_nt(q8, k_ref[pl.ds(w0, span), :])
    tq = t0 + lax.broadcasted_iota(jnp.int32, (TQB, span), 0)
    diff = tq - (w0 + lax.broadcasted_iota(jnp.int32, (TQB, span), 1))
    band = jnp.where((diff >= 0) & (diff < SWA_WIN), 0.0, NEG)
    s = (s.reshape(B_GROUP, TQB, span) + band[None]).reshape(rows, span)
    sk = sink_ref[...]
    sink = jnp.concatenate([jnp.broadcast_to(sk[r:r + 1, :], (TQB, LANES)) for r in range(B_GROUP)], axis=0)
    m = jnp.maximum(jnp.broadcast_to(jnp.max(s, axis=1, keepdims=True), (rows, LANES)), sink)
    e = jnp.exp(s - _tile_lanes(m, span // LANES))
    acc = _dot(e.astype(BF16), v_ref[pl.ds(w0, span), :])
    o = _normalize(acc, jnp.exp(sink - m))
    o_ref[...] = _heads_to_lanes(o, B_GROUP).astype(o_ref.dtype)


def _swa(qproj, kvproj, sinks, bsz, seq):
    nq = seq // TQB
    g_ = B_KV_HEADS
    return pl.pallas_call(
        _swa_kernel,
        grid=(bsz, g_, nq),
        in_specs=[
            pl.BlockSpec((TQB, B_GROUP * LANES), lambda b, g, i: (b * nq + i, g)),
            pl.BlockSpec((seq, LANES), lambda b, g, i: (b, g)),
            pl.BlockSpec((seq, LANES), lambda b, g, i: (b, g_ + g)),
            pl.BlockSpec((None, B_GROUP, LANES), lambda b, g, i: (g, 0, 0)),
        ],
        out_specs=pl.BlockSpec((TQB, B_GROUP * HEAD_DIM), lambda b, g, i: (b * nq + i, g)),
        out_shape=jax.ShapeDtypeStruct((bsz * seq, B_HEADS * HEAD_DIM), BF16),
        compiler_params=_cp("parallel", "parallel", "arbitrary"),
        name="swa",
    )(qproj, kvproj, kvproj, sinks)


def _router_kernel(x_ref, wh_ref, wl_ref, o_ref):
    x = x_ref[...]
    xh = x.astype(BF16)
    xl = (x - xh.astype(F32)).astype(BF16)
    logits = _dot(xh, wh_ref[...]) + _dot(xh, wl_ref[...]) + _dot(xl, wh_ref[...])
    lane = lax.broadcasted_iota(jnp.int32, logits.shape, 1)
    lg = jnp.where(lane < N_EXPERTS, logits, -jnp.inf)
    m1 = jnp.max(lg, axis=1, keepdims=True)
    i1 = jnp.min(jnp.where(lg == m1, lane, LANES), axis=1, keepdims=True)
    lg2 = jnp.where(lane == i1, -jnp.inf, lg)
    m2 = jnp.max(lg2, axis=1, keepdims=True)
    i2 = jnp.min(jnp.where(lg2 == m2, lane, LANES), axis=1, keepdims=True)
    e2 = jnp.exp(m2 - m1)
    g1 = 1.0 / (1.0 + e2)
    g2 = e2 * g1
    out = jnp.where(lane == 0, i1.astype(F32), jnp.where(lane == 1, i2.astype(F32),
                    jnp.where(lane == 2, g1, jnp.where(lane == 3, g2, 0.0))))
    o_ref[...] = out


def _router(h, w_hi, w_lo, tm=512):
    m, d = h.shape
    return pl.pallas_call(
        _router_kernel,
        grid=(m // tm,),
        in_specs=[pl.BlockSpec((tm, d), lambda i: (i, 0)), pl.BlockSpec((d, LANES), lambda i: (0, 0)),
                  pl.BlockSpec((d, LANES), lambda i: (0, 0))],
        out_specs=pl.BlockSpec((tm, LANES), lambda i: (i, 0)),
        out_shape=jax.ShapeDtypeStruct((m, LANES), F32),
        compiler_params=_cp("parallel"),
        name="router",
    )(h, w_hi, w_lo)


def _moe_ffn_kernel(blk_e_ref, x_ref, wg_ref, wu_ref, wo_ref, o_ref, acc_ref):
    c = pl.program_id(1)

    @pl.when(c == 0)
    def _():
        acc_ref[...] = jnp.zeros_like(acc_ref)

    xb = x_ref[...]
    gate = _dot(xb, wg_ref[...])
    up = _dot(xb, wu_ref[...])
    a = gate * _sigmoid(gate) * up
    acc_ref[...] += _dot(a.astype(BF16), wo_ref[...])

    @pl.when(c == pl.num_programs(1) - 1)
    def _():
        o_ref[...] = acc_ref[...]


def _moe_ffn(xs, blk_e, w_in, w_out, tm=MOE_TM, fc=896):
    cap, d = xs.shape
    ff = w_out.shape[1]
    nc = ff // fc
    grid_spec = pltpu.PrefetchScalarGridSpec(
        num_scalar_prefetch=1,
        grid=(cap // tm, nc),
        in_specs=[
            pl.BlockSpec((tm, d), lambda i, c, e: (i, 0)),
            pl.BlockSpec((None, d, fc), lambda i, c, e: (e[i], 0, c)),
            pl.BlockSpec((None, d, fc), lambda i, c, e: (e[i], 0, nc + c)),
            pl.BlockSpec((None, fc, d), lambda i, c, e: (e[i], c, 0)),
        ],
        out_specs=pl.BlockSpec((tm, d), lambda i, c, e: (i, 0)),
        scratch_shapes=[pltpu.VMEM((tm, d), F32)],
    )
    return pl.pallas_call(
        _moe_ffn_kernel,
        grid_spec=grid_spec,
        out_shape=jax.ShapeDtypeStruct((cap, d), F32),
        compiler_params=_cp("parallel", "arbitrary"),
        name="moe_ffn",
    )(blk_e, xs, w_in, w_in, w_out)


def _combine_ln_kernel(h_ref, y1_ref, y2_ref, r_ref, g_ref, b_ref, o_ref, ob_ref):
    r = r_ref[...]
    ffn = r[:, 2:3] * y1_ref[...] + r[:, 3:4] * y2_ref[...]
    y = _layer_norm(ALPHA * h_ref[...] + ffn, g_ref[...], b_ref[...])
    o_ref[...] = y
    ob_ref[...] = y.astype(BF16)


def _combine_ln(h, y1, y2, route, g, b, tm=512):
    m, d = h.shape
    row = pl.BlockSpec((tm, d), lambda i: (i, 0))
    vec = pl.BlockSpec((1, d), lambda i: (0, 0))
    out_specs, out_shape = _dual_out(m, d, tm)
    return pl.pallas_call(
        _combine_ln_kernel,
        grid=(m // tm,),
        in_specs=[row, row, row, pl.BlockSpec((tm, LANES), lambda i: (i, 0)), vec, vec],
        out_specs=out_specs,
        out_shape=out_shape,
        compiler_params=_cp("parallel"),
        name="combine_ln",
    )(h, y1, y2, route, g, b)


def _pad_heads(w, heads):
    d = w.shape[0]
    w = w.reshape(d, heads, HEAD_DIM)
    return jnp.pad(w, ((0, 0), (0, 0), (0, LANES - HEAD_DIM))).reshape(d, heads * LANES)


def _rope_tables(positions):
    half = ROT_DIM // 2
    inv = ROPE_THETA ** (-jnp.arange(0, ROT_DIM, 2, dtype=F32) / ROT_DIM)
    ang = positions.astype(F32).reshape(-1, 1) * inv
    cos, sin = jnp.cos(ang), jnp.sin(ang)
    n = ang.shape[0]
    ones = jnp.ones((n, LANES - ROT_DIM), F32)
    zeros = jnp.zeros((n, LANES - half), F32)
    c = jnp.concatenate([cos, cos, ones], axis=1)
    sa = jnp.concatenate([-sin, zeros], axis=1)
    sb = jnp.concatenate([jnp.zeros((n, half), F32), sin, jnp.zeros((n, LANES - ROT_DIM), F32)], axis=1)
    return c, sa, sb


def _cmp_to_slc(ncp, nslc):
    n = np.arange(ncp)[:, None]
    j = np.arange(nslc)[None, :]
    overlap = (np.minimum(n * CMP_STRIDE + CMP_BLOCK, j * SLC_BLOCK + SLC_BLOCK)
               - np.maximum(n * CMP_STRIDE, j * SLC_BLOCK))
    m = np.clip(overlap, 0, None).astype(np.float32) / CMP_BLOCK
    m[ncp - 1, :] = 0.0
    return jnp.asarray(m, BF16)


def _compress_branch(t, pos, w1, w2, bsz, seq):
    g_ = A_KV_HEADS
    nch = seq // CMP_STRIDE
    half = CMP_STRIDE * HEAD_DIM
    a = t.reshape(bsz, nch, CMP_STRIDE, g_, HEAD_DIM).transpose(0, 1, 3, 2, 4).reshape(bsz * nch * g_, half)
    w1b = w1.astype(BF16)
    w1cat = jnp.concatenate([w1b[:half], w1b[half:]], axis=1)
    pq = _mm(a, w1cat, F32, tm=1024).reshape(bsz, nch, g_, 2 * CMP_HID)
    pp = pq[..., :CMP_HID].reshape(-1, CMP_HID)
    qq = jnp.concatenate([pq[:, 1:, :, CMP_HID:], jnp.zeros((bsz, 1, g_, CMP_HID), F32)], axis=1).reshape(-1, CMP_HID)
    pos8 = jnp.zeros((8, CMP_BLOCK * HEAD_DIM), BF16).at[0].set(pos.reshape(-1).astype(BF16))
    out = _compress(pp, qq, pos8, w1b, w2.astype(BF16))
    out = out.reshape(bsz, nch, g_, HEAD_DIM).transpose(0, 2, 1, 3)
    return out


def _nsa_layer(h, hb, tabs, w_in, w_out, cmp_pos, cmp_w1, cmp_w2, ln_g, ln_b, bsz, seq):
    o, _, _ = _nsa_attn(hb, tabs, w_in, cmp_pos, cmp_w1, cmp_w2, bsz, seq)
    return _out_ln(o, w_out.astype(BF16), h, ln_g[None, :], ln_b[None, :])


def _nsa_attn(hb, tabs, w_in, cmp_pos, cmp_w1, cmp_w2, bsz, seq):
    hb = hb.astype(BF16)
    g_ = A_KV_HEADS
    aq = A_HEADS * HEAD_DIM
    akv = g_ * HEAD_DIM
    wq = w_in[:, :aq] * (HEAD_DIM ** -0.5)
    w_kc, w_vc, w_ks, w_vs, w_kw, w_vw = (w_in[:, aq + i * akv: aq + (i + 1) * akv] for i in range(6))
    w_gl = w_in[:, aq + 6 * akv:]
    w_big = jnp.concatenate([_pad_heads(wq, A_HEADS), _pad_heads(w_ks, g_), _pad_heads(w_vs, g_),
                             _pad_heads(w_kw, g_), _pad_heads(w_vw, g_)], axis=1).astype(BF16)
    slab_modes = ["plain"] * A_HEADS + ["rope_onehot"] * g_ + ["ones"] * g_ + ["rope"] * g_ + ["ones"] * g_
    proj = _proj(hb, w_big, tabs, slab_modes)

    w_gl_g = w_gl.reshape(-1, 3, g_, A_GROUP).transpose(0, 2, 1, 3).reshape(-1, g_, 3 * A_GROUP)
    w_gl_g = jnp.pad(w_gl_g, ((0, 0), (0, 0), (0, LANES - 3 * A_GROUP))).reshape(-1, g_ * LANES)
    side = _mm(hb, jnp.concatenate([w_gl_g, w_kc, w_vc], axis=1).astype(BF16), F32, sig_cols=g_ * LANES)
    gates = side[:, :g_ * LANES]
    kc = _compress_branch(side[:, g_ * LANES:g_ * LANES + akv], cmp_pos[0], cmp_w1[0], cmp_w2[0], bsz, seq)
    vc = _compress_branch(side[:, g_ * LANES + akv:], cmp_pos[1], cmp_w1[1], cmp_w2[1], bsz, seq)
    pad = ((0, 0), (0, 0), (0, 0), (0, LANES - HEAD_DIM))
    kc = jnp.pad(kc, pad).astype(BF16)
    vc = jnp.pad(vc, pad).astype(BF16)
    ncp = seq // CMP_STRIDE
    cmp_map_t = _cmp_to_slc(ncp, seq // SLC_BLOCK).T

    o_c, selbias = _cmp_select(proj, kc, vc, cmp_map_t, bsz, seq)
    o = _sel_win(proj, tabs, selbias, o_c, gates, bsz, seq)
    return o, selbias, o_c


def _shared_kv(hb, tabs, w_kv):
    g_ = B_KV_HEADS
    bkv = g_ * HEAD_DIM
    w = jnp.concatenate([_pad_heads(w_kv[:, :bkv], g_), _pad_heads(w_kv[:, bkv:], g_)], axis=1).astype(BF16)
    return _proj(hb.astype(BF16), w, tabs, ["rope"] * g_ + ["ones"] * g_)


def _swa_layer(h, hb, tabs, kvproj, w_q, w_out, sinks, ln_g, ln_b, bsz, seq):
    o = _swa_attn(hb, tabs, kvproj, w_q, sinks, bsz, seq)
    return _out_ln(o, w_out.astype(BF16), h, ln_g[None, :], ln_b[None, :])


def _swa_attn(hb, tabs, kvproj, w_q, sinks, bsz, seq):
    wq = _pad_heads(w_q * (HEAD_DIM ** -0.5), B_HEADS).astype(BF16)
    qproj = _proj(hb.astype(BF16), wq, tabs, ["rope"] * B_HEADS)
    sk = jnp.broadcast_to(sinks.astype(F32).reshape(B_KV_HEADS, B_GROUP, 1), (B_KV_HEADS, B_GROUP, LANES))
    return _swa(qproj, kvproj, sk, bsz, seq)


def _moe_layer(h, hb, w_router, w_in, w_out, ln_g, ln_b):
    y1, y2, route = _moe_experts(h, hb, w_router, w_in, w_out)
    return _combine_ln(h, y1, y2, route, ln_g[None, :], ln_b[None, :])


def _moe_ffn_out(h, w_router, w_in, w_out):
    y1, y2, route = _moe_experts(h, h.astype(BF16), w_router, w_in, w_out)
    return route[:, 2:3] * y1 + route[:, 3:4] * y2


def _moe_experts(h, hb, w_router, w_in, w_out):
    n_tok, d = h.shape
    wr = jnp.pad(w_router, ((0, 0), (0, LANES - N_EXPERTS)))
    wr_hi = wr.astype(BF16)
    wr_lo = (wr - wr_hi.astype(F32)).astype(BF16)
    route = _router(h, wr_hi, wr_lo)
    top_e = route[:, :2].astype(jnp.int32)

    e_flat = top_e.reshape(-1)
    onehot = (e_flat[:, None] == jnp.arange(N_EXPERTS)[None, :]).astype(jnp.int32)
    csum = jnp.cumsum(onehot, axis=0)
    counts = csum[-1]
    rank = jnp.take_along_axis(csum, e_flat[:, None], axis=1)[:, 0] - 1
    padded = (counts + MOE_TM - 1) // MOE_TM * MOE_TM
    pad_end = jnp.cumsum(padded)
    pad_start = pad_end - padded
    dest = pad_start[e_flat] + rank
    cap = n_tok * 2 + N_EXPERTS * MOE_TM
    n_blk = cap // MOE_TM
    tok_flat = jnp.repeat(jnp.arange(n_tok, dtype=jnp.int32), 2)
    buf_tok = jnp.zeros((cap,), jnp.int32).at[dest].set(tok_flat)
    blk_start = jnp.arange(n_blk, dtype=jnp.int32) * MOE_TM
    blk_e = jnp.sum((pad_end[None, :] <= blk_start[:, None]).astype(jnp.int32), axis=1)
    blk_e = jnp.minimum(blk_e, N_EXPERTS - 1).astype(jnp.int32)

    xs = hb[buf_tok]
    y = _moe_ffn(xs, blk_e, w_in.astype(BF16), w_out.astype(BF16))
    dest2 = dest.reshape(n_tok, 2)
    y1 = y[dest2[:, 0]]
    y2 = y[dest2[:, 1]]
    return y1, y2, route


def kernel(x, positions, w_in_a, w_out_a, cmp_pos, cmp_w1, cmp_w2, w_kv_shared, w_q_b, w_out_b, sinks_b,
           ln_g, ln_b, dense_w_in, dense_w_out, moe_router, moe_w_in, moe_w_out):
    bsz, seq, d = x.shape
    n_a = DEPTH // 2
    tabs = _rope_tables(positions)
    h = x.reshape(bsz * seq, d)
    hb = h.astype(BF16)
    kvproj = None
    for l in range(DEPTH):
        if l < n_a:
            h, hb = _nsa_layer(h, hb, tabs, w_in_a[l], w_out_a[l], cmp_pos[l], cmp_w1[l], cmp_w2[l],
                               ln_g[l, 0], ln_b[l, 0], bsz, seq)
        else:
            b = l - n_a
            h, hb = _swa_layer(h, hb, tabs, kvproj, w_q_b[b], w_out_b[b], sinks_b[b], ln_g[l, 0], ln_b[l, 0],
                               bsz, seq)
        if l % 2 == 0:
            h, hb = _ffn_dense(h, dense_w_in[l // 2].astype(BF16), dense_w_out[l // 2].astype(BF16),
                               ln_g[l, 1][None, :], ln_b[l, 1][None, :])
        else:
            h, hb = _moe_layer(h, hb, moe_router[l // 2], moe_w_in[l // 2], moe_w_out[l // 2],
                               ln_g[l, 1], ln_b[l, 1])
        if l == n_a - 1:
            kvproj = _shared_kv(hb, tabs, w_kv_shared)
    return h.reshape(bsz, seq, d)
```

```python
import functools

import numpy as np
import jax
import jax.numpy as jnp
from jax import lax
from jax.experimental import pallas as pl
from jax.experimental.pallas import tpu as pltpu

F32 = jnp.float32
BF16 = jnp.bfloat16

D_MODEL = 1024
DEPTH = 4
HEAD_DIM = 64
LANES = 128
ROT_DIM = HEAD_DIM // 4
ROPE_THETA = 500000.0
A_HEADS = 16
A_KV_HEADS = 4
A_GROUP = 4
CMP_BLOCK = 32
CMP_STRIDE = 16
CMP_HID = 256
SLC_BLOCK = 64
N_SEL = 16
N_FORCED = 3
NSA_WIN = 512
B_HEADS = 16
B_KV_HEADS = 2
B_GROUP = 8
SWA_WIN = 128
D_FF = 2816
N_EXPERTS = 8
D_FF_EXPERT = 3584
ALPHA = (2 * DEPTH) ** 0.25
LN_EPS = 1e-5
NEG = -1e30
FORCE = 1e9

TQB = 256
TQA = 256
CMP_CHUNK = 256
LOG2E = 1.4426950408889634
KT = 512
BLK_PER_KT = KT // SLC_BLOCK
KT_GROUP = 4
MOE_TM = 512
VMEM_LIMIT = 56 * 1024 * 1024


def _cp(*sem):
    return pltpu.CompilerParams(dimension_semantics=sem, vmem_limit_bytes=VMEM_LIMIT)


def _dot(a, b):
    return jnp.dot(a, b, preferred_element_type=F32)


def _dot_nt(a, b):
    return lax.dot_general(a, b, (((1,), (1,)), ((), ())), preferred_element_type=F32)


def _sigmoid(x):
    return 1.0 / (1.0 + jnp.exp(-x))


def _layer_norm(z, g, b):
    mu = jnp.mean(z, axis=-1, keepdims=True)
    zc = z - mu
    var = jnp.mean(zc * zc, axis=-1, keepdims=True)
    return zc * lax.rsqrt(var + LN_EPS) * g + b


def _rope(x, c, sa, sb):
    w = x.shape[1]
    return x * c + pltpu.roll(x, w - ROT_DIM // 2, 1) * sa + pltpu.roll(x, ROT_DIM // 2, 1) * sb


def _tile_lanes(t, n):
    return t if n == 1 else jnp.concatenate([t] * n, axis=1)


def _tile_rows(t, n):
    return t if n == 1 else jnp.concatenate([t] * n, axis=0)


def _normalize(acc, extra=0.0):
    den = pltpu.roll(acc, HEAD_DIM, 1) + extra
    return (acc / den)[:, :HEAD_DIM]


def _heads_to_lanes(o, heads):
    t = o.shape[0] // heads
    return jnp.concatenate([o[r * t:(r + 1) * t] for r in range(heads)], axis=1)


def _mm_kernel(x_ref, w_ref, o_ref, *, sig_cols):
    acc = _dot(x_ref[...].astype(BF16), w_ref[...])
    if sig_cols:
        o_ref[:, :sig_cols] = _sigmoid(acc[:, :sig_cols]).astype(o_ref.dtype)
        o_ref[:, sig_cols:] = acc[:, sig_cols:].astype(o_ref.dtype)
    else:
        o_ref[...] = acc.astype(o_ref.dtype)


def _mm(x, w, out_dtype, sig_cols=0, tm=512):
    m, k = x.shape
    n = w.shape[1]
    tm = min(tm, m)
    return pl.pallas_call(
        functools.partial(_mm_kernel, sig_cols=sig_cols),
        grid=(m // tm,),
        in_specs=[pl.BlockSpec((tm, k), lambda i: (i, 0)), pl.BlockSpec((k, n), lambda i: (0, 0))],
        out_specs=pl.BlockSpec((tm, n), lambda i: (i, 0)),
        out_shape=jax.ShapeDtypeStruct((m, n), out_dtype),
        compiler_params=_cp("parallel"),
        name="mm",
    )(x, w)


def _proj_kernel(x_ref, w_ref, c_ref, sa_ref, sb_ref, o_ref, *, block_modes, tm, tn):
    i = pl.program_id(0)
    j = pl.program_id(1)
    acc = _dot(x_ref[...], w_ref[...])

    def piece(mode, s0, ns):
        a = acc[:, s0 * LANES:(s0 + ns) * LANES]
        lane = lax.broadcasted_iota(jnp.int32, a.shape, 1) % LANES
        if mode in ("rope", "rope_onehot"):
            a = _rope(a, _tile_lanes(c_ref[...], ns), _tile_lanes(sa_ref[...], ns), _tile_lanes(sb_ref[...], ns))
        if mode == "rope_onehot":
            row = i * tm + lax.broadcasted_iota(jnp.int32, a.shape, 0)
            a = a + jnp.where(lane == HEAD_DIM + (row // SLC_BLOCK) % BLK_PER_KT, 1.0, 0.0)
        if mode == "ones":
            a = a + jnp.where(lane >= HEAD_DIM, 1.0, 0.0)
        return a.astype(o_ref.dtype)

    for modes in sorted(set(block_modes)):
        pred = functools.reduce(jnp.logical_or, [j == jj for jj, mm in enumerate(block_modes) if mm == modes])

        @pl.when(pred)
        def _(modes=modes):
            runs = []
            for s, mode in enumerate(modes):
                if runs and runs[-1][0] == mode:
                    runs[-1][2] += 1
                else:
                    runs.append([mode, s, 1])
            pieces = [piece(*r) for r in runs]
            o_ref[...] = pieces[0] if len(pieces) == 1 else jnp.concatenate(pieces, axis=1)


def _proj(x, w, tabs, slab_modes, tn=1024, tm=1024):
    m, k = x.shape
    n = w.shape[1]
    tn = min(tn, n)
    spb = tn // LANES
    block_modes = tuple(tuple(slab_modes[jj * spb:(jj + 1) * spb]) for jj in range(n // tn))
    c, sa, sb = tabs
    return pl.pallas_call(
        functools.partial(_proj_kernel, block_modes=block_modes, tm=tm, tn=tn),
        grid=(m // tm, n // tn),
        in_specs=[
            pl.BlockSpec((tm, k), lambda i, j: (i, 0)),
            pl.BlockSpec((k, tn), lambda i, j: (0, j)),
            pl.BlockSpec((tm, LANES), lambda i, j: (i, 0)),
            pl.BlockSpec((tm, LANES), lambda i, j: (i, 0)),
            pl.BlockSpec((tm, LANES), lambda i, j: (i, 0)),
        ],
        out_specs=pl.BlockSpec((tm, tn), lambda i, j: (i, j)),
        out_shape=jax.ShapeDtypeStruct((m, n), BF16),
        compiler_params=_cp("parallel", "arbitrary"),
        name="proj",
    )(x, w, c, sa, sb)


def _out_ln_kernel(x_ref, w_ref, h_ref, g_ref, b_ref, o_ref, ob_ref):
    mix = _dot(x_ref[...], w_ref[...])
    y = _layer_norm(ALPHA * h_ref[...] + mix, g_ref[...], b_ref[...])
    o_ref[...] = y
    ob_ref[...] = y.astype(BF16)


def _dual_out(m, d, tm):
    spec = pl.BlockSpec((tm, d), lambda i, *_: (i, 0))
    return [spec, spec], [jax.ShapeDtypeStruct((m, d), F32), jax.ShapeDtypeStruct((m, d), BF16)]


def _out_ln(x, w, h, g, b, tm=512):
    m, k = x.shape
    d = w.shape[1]
    out_specs, out_shape = _dual_out(m, d, tm)
    return pl.pallas_call(
        _out_ln_kernel,
        grid=(m // tm,),
        in_specs=[
            pl.BlockSpec((tm, k), lambda i: (i, 0)),
            pl.BlockSpec((k, d), lambda i: (0, 0)),
            pl.BlockSpec((tm, d), lambda i: (i, 0)),
            pl.BlockSpec((1, d), lambda i: (0, 0)),
            pl.BlockSpec((1, d), lambda i: (0, 0)),
        ],
        out_specs=out_specs,
        out_shape=out_shape,
        compiler_params=_cp("parallel"),
        name="out_ln",
    )(x, w, h, g, b)


def _ffn_kernel(x_ref, wg_ref, wu_ref, wo_ref, g_ref, b_ref, o_ref, ob_ref, acc_ref):
    c = pl.program_id(1)

    @pl.when(c == 0)
    def _():
        acc_ref[...] = jnp.zeros_like(acc_ref)

    xb = x_ref[...].astype(BF16)
    gate = _dot(xb, wg_ref[...])
    up = _dot(xb, wu_ref[...])
    a = gate * _sigmoid(gate) * up
    acc_ref[...] += _dot(a.astype(BF16), wo_ref[...])

    @pl.when(c == pl.num_programs(1) - 1)
    def _():
        y = _layer_norm(ALPHA * x_ref[...] + acc_ref[...], g_ref[...], b_ref[...])
        o_ref[...] = y
        ob_ref[...] = y.astype(BF16)


def _ffn_dense(h, w_in, w_out, g, b, tm=512):
    m, d = h.shape
    ff = w_out.shape[0]
    fc, nc = ff, 1
    once = pl.Buffered(1)
    out_specs, out_shape = _dual_out(m, d, tm)
    return pl.pallas_call(
        _ffn_kernel,
        grid=(m // tm, nc),
        in_specs=[
            pl.BlockSpec((tm, d), lambda i, c: (i, 0)),
            pl.BlockSpec((d, fc), lambda i, c: (0, c), pipeline_mode=once),
            pl.BlockSpec((d, fc), lambda i, c: (0, nc + c), pipeline_mode=once),
            pl.BlockSpec((fc, d), lambda i, c: (c, 0), pipeline_mode=once),
            pl.BlockSpec((1, d), lambda i, c: (0, 0)),
            pl.BlockSpec((1, d), lambda i, c: (0, 0)),
        ],
        out_specs=out_specs,
        out_shape=out_shape,
        scratch_shapes=[pltpu.VMEM((tm, d), F32)],
        compiler_params=_cp("parallel", "arbitrary"),
        name="ffn_dense",
    )(h, w_in, w_in, w_out, g, b)


def _compress_kernel(p_ref, q_ref, pos_ref, w1_ref, w2_ref, o_ref):
    posb = _dot(pos_ref[...], w1_ref[...])[0:1, :]
    hid = p_ref[...] + q_ref[...] + posb
    act = jax.nn.gelu(hid, approximate=True)
    o_ref[...] = _dot(act.astype(BF16), w2_ref[...])


def _compress(pp, qq, pos8, w1, w2, tm=1024):
    m, hid = pp.shape
    dh = w2.shape[1]
    return pl.pallas_call(
        _compress_kernel,
        grid=(m // tm,),
        in_specs=[
            pl.BlockSpec((tm, hid), lambda i: (i, 0)),
            pl.BlockSpec((tm, hid), lambda i: (i, 0)),
            pl.BlockSpec(pos8.shape, lambda i: (0, 0)),
            pl.BlockSpec(w1.shape, lambda i: (0, 0)),
            pl.BlockSpec(w2.shape, lambda i: (0, 0)),
        ],
        out_specs=pl.BlockSpec((tm, dh), lambda i: (i, 0)),
        out_shape=jax.ShapeDtypeStruct((m, dh), F32),
        compiler_params=_cp("parallel"),
        name="compress",
    )(pp, qq, pos8, w1, w2)


def _cmp_select_kernel(q_ref, kc_ref, vc_ref, mapt_ref, oc_ref, sb_ref, imp_ref, *, ncp, nslc):
    qi = pl.program_id(2)
    t0 = qi * TQA
    rows = A_GROUP * TQA
    qb = q_ref[...]
    q4 = jnp.concatenate([qb[:, r * LANES:(r + 1) * LANES] for r in range(A_GROUP)], axis=0)

    def branch(nk):
        s = _dot_nt(q4, kc_ref[0:nk, :])
        t_tok = t0 + lax.broadcasted_iota(jnp.int32, (TQA, nk), 0)
        n_idx = lax.broadcasted_iota(jnp.int32, (TQA, nk), 1)
        valid = (n_idx * CMP_STRIDE + CMP_BLOCK - 1 <= t_tok) & (n_idx < ncp - 1)
        s3 = s.reshape(A_GROUP, TQA, nk) + jnp.where(valid, 0.0, NEG)[None]
        m = jnp.max(s3, axis=2, keepdims=True)
        p = jnp.exp2(s3 - m)
        l = jnp.sum(p, axis=2, keepdims=True)
        pn = p * jnp.where(m > 0.5 * NEG, 1.0 / l, 0.0)
        oc = _dot(pn.reshape(rows, nk).astype(BF16), vc_ref[0:nk, :])
        oc_ref[...] = oc[:, :HEAD_DIM].reshape(A_GROUP, TQA, HEAD_DIM)
        psum = pn[0] + pn[1] + pn[2] + pn[3]
        p_hi = psum.astype(BF16)
        p_lo = (psum - p_hi.astype(F32)).astype(BF16)
        mapt = mapt_ref[:, 0:nk]
        imp_ref[...] = _dot_nt(mapt, p_hi) + _dot_nt(mapt, p_lo)

    n_buckets = ncp // CMP_CHUNK
    bucket = ((qi + 1) * (TQA // CMP_STRIDE) - 1) // CMP_CHUNK
    for nb in range(n_buckets):
        pl.when(bucket == nb)(functools.partial(branch, (nb + 1) * CMP_CHUNK))
    imp = imp_ref[...]

    j_idx = lax.broadcasted_iota(jnp.int32, (nslc, TQA), 0)
    cur = (t0 + lax.broadcasted_iota(jnp.int32, (nslc, TQA), 1)) // SLC_BLOCK
    forced = (j_idx == 0) | (j_idx == cur) | (j_idx == cur - 1)
    score = jnp.where(j_idx > cur, -1.0, jnp.where(forced, -jnp.inf, imp))
    for _ in range(N_SEL - N_FORCED):
        mx = jnp.max(score, axis=0, keepdims=True)
        first = jnp.min(jnp.where(score == mx, j_idx, nslc), axis=0, keepdims=True)
        score = jnp.where(j_idx == first, -jnp.inf, score)
    sb_ref[...] = jnp.where((score == -jnp.inf) & (j_idx <= cur), 0.0, NEG).T


def _cmp_select(proj, kc_cmp, vc_cmp, cmp_map_t, bsz, seq):
    ncp = kc_cmp.shape[2]
    nslc = seq // SLC_BLOCK
    nq = seq // TQA
    g_ = A_KV_HEADS
    assert nslc >= N_SEL and ncp % CMP_CHUNK == 0
    return pl.pallas_call(
        functools.partial(_cmp_select_kernel, ncp=ncp, nslc=nslc),
        grid=(bsz, g_, nq),
        in_specs=[
            pl.BlockSpec((TQA, A_GROUP * LANES), lambda b, g, i: (b * nq + i, g)),
            pl.BlockSpec((None, None, ncp, LANES), lambda b, g, i: (b, g, 0, 0)),
            pl.BlockSpec((None, None, ncp, LANES), lambda b, g, i: (b, g, 0, 0)),
            pl.BlockSpec((nslc, ncp), lambda b, g, i: (0, 0)),
        ],
        out_specs=[
            pl.BlockSpec((None, None, A_GROUP, TQA, HEAD_DIM), lambda b, g, i: (b, g, 0, i, 0)),
            pl.BlockSpec((TQA, nslc), lambda b, g, i: (b * nq + i, g)),
        ],
        out_shape=[
            jax.ShapeDtypeStruct((bsz, g_, A_GROUP, seq, HEAD_DIM), F32),
            jax.ShapeDtypeStruct((bsz * seq, g_ * nslc), F32),
        ],
        scratch_shapes=[pltpu.VMEM((nslc, TQA), F32)],
        compiler_params=_cp("parallel", "parallel", "arbitrary"),
        name="cmp_select",
    )(proj, kc_cmp, vc_cmp, cmp_map_t)


def _sel_win_kernel(q_ref, c_ref, sa_ref, sb_ref, bias_ref, ks_ref, vs_ref, kw_ref, vw_ref, oc_ref, gate_ref,
                    o_ref, *, nslc):
    qi = pl.program_id(2)
    t0 = qi * TQA
    rows = A_GROUP * TQA
    qb = q_ref[...]
    q4 = jnp.concatenate([qb[:, r * LANES:(r + 1) * LANES] for r in range(A_GROUP)], axis=0).astype(F32)
    q_rot = _rope(q4, _tile_rows(c_ref[...], A_GROUP), _tile_rows(sa_ref[...], A_GROUP),
                  _tile_rows(sb_ref[...], A_GROUP)) * LOG2E
    q_rot_b = q_rot.astype(BF16)
    bias = bias_ref[...]
    lane = lax.broadcasted_iota(jnp.int32, (TQA, LANES), 1)
    bias_lanes = (lane >= HEAD_DIM) & (lane < HEAD_DIM + BLK_PER_KT)
    t_q = t0 + lax.broadcasted_iota(jnp.int32, (TQA, KT), 0)
    k_off = lax.broadcasted_iota(jnp.int32, (TQA, KT), 1)

    def add_mask(s, mask_bias):
        return (s.reshape(A_GROUP, TQA, -1) + mask_bias[None]).reshape(s.shape)

    def scores(kt):
        shift = (HEAD_DIM + nslc - BLK_PER_KT * kt) % nslc
        rolled = pltpu.roll(bias, shift, 1)[:, :LANES]
        qa = (q_rot + _tile_rows(jnp.where(bias_lanes, rolled, 0.0), A_GROUP)).astype(BF16)
        k0 = pl.multiple_of(kt * KT, KT)
        return _dot_nt(qa, ks_ref[pl.ds(k0, KT), :])

    def causal_bias(kt):
        return jnp.where(kt * KT + k_off <= t_q, 0.0, NEG)

    def group(kt0, n, carry, causal):
        m, acc = carry
        ss = [scores(kt0 + i) for i in range(n)]
        if causal:
            ss = [add_mask(s, causal_bias(kt0 + i)) for i, s in enumerate(ss)]
        m_row = functools.reduce(jnp.maximum, [jnp.max(s, axis=1, keepdims=True) for s in ss])
        m_new = jnp.maximum(m, jnp.broadcast_to(m_row, (rows, LANES)))
        m_keys = _tile_lanes(m_new, KT // LANES)
        pv = None
        for i, s in enumerate(ss):
            k0 = pl.multiple_of((kt0 + i) * KT, KT)
            d = _dot(jnp.exp2(s - m_keys).astype(BF16), vs_ref[pl.ds(k0, KT), :])
            pv = d if pv is None else pv + d
        return m_new, jnp.exp2(m - m_new) * acc + pv

    kt_d = t0 // KT
    n_quads = kt_d // KT_GROUP
    carry = (jnp.full((rows, LANES), NEG, F32), jnp.zeros((rows, LANES), F32))
    carry = lax.fori_loop(0, n_quads, lambda j, c: group(j * KT_GROUP, KT_GROUP, c, False), carry)
    carry = lax.fori_loop(n_quads * KT_GROUP, kt_d, lambda kt, c: group(kt, 1, c, False), carry)
    _, acc_s = group(kt_d, 1, carry, True)
    o_s = _normalize(acc_s)

    wspan = NSA_WIN + TQA
    w0 = pl.multiple_of(jnp.maximum(t0 - NSA_WIN, 0), TQA)
    s_w = _dot_nt(q_rot_b, kw_ref[pl.ds(w0, wspan), :])
    tw = t0 + lax.broadcasted_iota(jnp.int32, (TQA, wspan), 0)
    diff = tw - (w0 + lax.broadcasted_iota(jnp.int32, (TQA, wspan), 1))
    s_w = add_mask(s_w, jnp.where((diff >= 0) & (diff < NSA_WIN), 0.0, NEG))
    p_w = jnp.exp2(s_w - jnp.max(s_w, axis=1, keepdims=True))
    acc_w = _dot(p_w.astype(BF16), vw_ref[pl.ds(w0, wspan), :])
    o_w = _normalize(acc_w)

    gates = gate_ref[...]

    def gcol(c):
        return jnp.concatenate([gates[:, c * A_GROUP + r:c * A_GROUP + r + 1] for r in range(A_GROUP)], axis=0)

    o_c = oc_ref[...].reshape(rows, HEAD_DIM)
    o = gcol(0) * o_c + gcol(1) * o_s + gcol(2) * o_w
    o_ref[...] = _heads_to_lanes(o, A_GROUP).astype(o_ref.dtype)


def _sel_win(proj, tabs, selbias, o_c, gates, bsz, seq):
    nslc = seq // SLC_BLOCK
    nq = seq // TQA
    g_ = A_KV_HEADS
    c, sa, sb = tabs
    tab_spec = pl.BlockSpec((TQA, LANES), lambda b, g, i: (b * nq + i, 0))
    q_slabs = A_HEADS

    def kv_spec(base):
        return pl.BlockSpec((seq, LANES), lambda b, g, i: (b, base + g))

    return pl.pallas_call(
        functools.partial(_sel_win_kernel, nslc=nslc),
        grid=(bsz, g_, nq),
        in_specs=[
            pl.BlockSpec((TQA, A_GROUP * LANES), lambda b, g, i: (b * nq + i, g)),
            tab_spec, tab_spec, tab_spec,
            pl.BlockSpec((TQA, nslc), lambda b, g, i: (b * nq + i, g)),
            kv_spec(q_slabs), kv_spec(q_slabs + g_), kv_spec(q_slabs + 2 * g_), kv_spec(q_slabs + 3 * g_),
            pl.BlockSpec((None, None, A_GROUP, TQA, HEAD_DIM), lambda b, g, i: (b, g, 0, i, 0)),
            pl.BlockSpec((TQA, LANES), lambda b, g, i: (b * nq + i, g)),
        ],
        out_specs=pl.BlockSpec((TQA, A_GROUP * HEAD_DIM), lambda b, g, i: (b * nq + i, g)),
        out_shape=jax.ShapeDtypeStruct((bsz * seq, A_HEADS * HEAD_DIM), BF16),
        compiler_params=_cp("parallel", "parallel", "arbitrary"),
        name="sel_win",
    )(proj, c, sa, sb, selbias, proj, proj, proj, proj, o_c, gates)


def _swa_kernel(q_ref, k_ref, v_ref, sink_ref, o_ref):
    qi = pl.program_id(2)
    t0 = qi * TQB
    rows = B_GROUP * TQB
    span = SWA_WIN + TQB
    qb = q_ref[...]
    q8 = jnp.concatenate([qb[:, r * LANES:(r + 1) * LANES] for r in range(B_GROUP)], axis=0)
    w0 = pl.multiple_of(jnp.maximum(t0 - SWA_WIN, 0), SWA_WIN)
    s = _dot_nt(q8, k_ref[pl.ds(w0, span), :])
    tq = t0 + lax.broadcasted_iota(jnp.int32, (TQB, span), 0)
    diff = tq - (w0 + lax.broadcasted_iota(jnp.int32, (TQB, span), 1))
    band = jnp.where((diff >= 0) & (diff < SWA_WIN), 0.0, NEG)
    s = (s.reshape(B_GROUP, TQB, span) + band[None]).reshape(rows, span)
    sk = sink_ref[...]
    sink = jnp.concatenate([jnp.broadcast_to(sk[r:r + 1, :], (TQB, LANES)) for r in range(B_GROUP)], axis=0)
    m = jnp.maximum(jnp.broadcast_to(jnp.max(s, axis=1, keepdims=True), (rows, LANES)), sink)
    e = jnp.exp(s - _tile_lanes(m, span // LANES))
    acc = _dot(e.astype(BF16), v_ref[pl.ds(w0, span), :])
    o = _normalize(acc, jnp.exp(sink - m))
    o_ref[...] = _heads_to_lanes(o, B_GROUP).astype(o_ref.dtype)


def _swa(qproj, kvproj, sinks, bsz, seq):
    nq = seq // TQB
    g_ = B_KV_HEADS
    return pl.pallas_call(
        _swa_kernel,
        grid=(bsz, g_, nq),
        in_specs=[
            pl.BlockSpec((TQB, B_GROUP * LANES), lambda b, g, i: (b * nq + i, g)),
            pl.BlockSpec((seq, LANES), lambda b, g, i: (b, g)),
            pl.BlockSpec((seq, LANES), lambda b, g, i: (b, g_ + g)),
            pl.BlockSpec((None, B_GROUP, LANES), lambda b, g, i: (g, 0, 0)),
        ],
        out_specs=pl.BlockSpec((TQB, B_GROUP * HEAD_DIM), lambda b, g, i: (b * nq + i, g)),
        out_shape=jax.ShapeDtypeStruct((bsz * seq, B_HEADS * HEAD_DIM), BF16),
        compiler_params=_cp("parallel", "parallel", "arbitrary"),
        name="swa",
    )(qproj, kvproj, kvproj, sinks)


def _router_kernel(x_ref, wh_ref, wl_ref, o_ref):
    x = x_ref[...]
    xh = x.astype(BF16)
    xl = (x - xh.astype(F32)).astype(BF16)
    logits = _dot(xh, wh_ref[...]) + _dot(xh, wl_ref[...]) + _dot(xl, wh_ref[...])
    lane = lax.broadcasted_iota(jnp.int32, logits.shape, 1)
    lg = jnp.where(lane < N_EXPERTS, logits, -jnp.inf)
    m1 = jnp.max(lg, axis=1, keepdims=True)
    i1 = jnp.min(jnp.where(lg == m1, lane, LANES), axis=1, keepdims=True)
    lg2 = jnp.where(lane == i1, -jnp.inf, lg)
    m2 = jnp.max(lg2, axis=1, keepdims=True)
    i2 = jnp.min(jnp.where(lg2 == m2, lane, LANES), axis=1, keepdims=True)
    e2 = jnp.exp(m2 - m1)
    g1 = 1.0 / (1.0 + e2)
    g2 = e2 * g1
    out = jnp.where(lane == 0, i1.astype(F32), jnp.where(lane == 1, i2.astype(F32),
                    jnp.where(lane == 2, g1, jnp.where(lane == 3, g2, 0.0))))
    o_ref[...] = out


def _router(h, w_hi, w_lo, tm=512):
    m, d = h.shape
    return pl.pallas_call(
        _router_kernel,
        grid=(m // tm,),
        in_specs=[pl.BlockSpec((tm, d), lambda i: (i, 0)), pl.BlockSpec((d, LANES), lambda i: (0, 0)),
                  pl.BlockSpec((d, LANES), lambda i: (0, 0))],
        out_specs=pl.BlockSpec((tm, LANES), lambda i: (i, 0)),
        out_shape=jax.ShapeDtypeStruct((m, LANES), F32),
        compiler_params=_cp("parallel"),
        name="router",
    )(h, w_hi, w_lo)


def _moe_ffn_kernel(blk_e_ref, n_act_ref, x_ref, wg_ref, wu_ref, wo_ref, o_ref, acc_ref):
    i = pl.program_id(0)
    c = pl.program_id(1)

    @pl.when(c == 0)
    def _():
        acc_ref[...] = jnp.zeros_like(acc_ref)

    @pl.when(i < n_act_ref[0])
    def _():
        xb = x_ref[...]
        gate = _dot(xb, wg_ref[...])
        up = _dot(xb, wu_ref[...])
        a = gate * _sigmoid(gate) * up
        acc_ref[...] += _dot(a.astype(BF16), wo_ref[...])

    @pl.when(c == pl.num_programs(1) - 1)
    def _():
        o_ref[...] = acc_ref[...]


def _moe_ffn(xs, blk_e, n_act, w_in, w_out, tm=MOE_TM, fc=1792):
    cap, d = xs.shape
    ff = w_out.shape[1]
    nc = ff // fc

    def chunk(i, c, n_act_ref):
        return jnp.where(i < n_act_ref[0], c, nc - 1)

    grid_spec = pltpu.PrefetchScalarGridSpec(
        num_scalar_prefetch=2,
        grid=(cap // tm, nc),
        in_specs=[
            pl.BlockSpec((tm, d), lambda i, c, e, na: (i, 0)),
            pl.BlockSpec((None, d, fc), lambda i, c, e, na: (e[i], 0, chunk(i, c, na))),
            pl.BlockSpec((None, d, fc), lambda i, c, e, na: (e[i], 0, nc + chunk(i, c, na))),
            pl.BlockSpec((None, fc, d), lambda i, c, e, na: (e[i], chunk(i, c, na), 0)),
        ],
        out_specs=pl.BlockSpec((tm, d), lambda i, c, e, na: (i, 0)),
        scratch_shapes=[pltpu.VMEM((tm, d), F32)],
    )
    return pl.pallas_call(
        _moe_ffn_kernel,
        grid_spec=grid_spec,
        out_shape=jax.ShapeDtypeStruct((cap, d), F32),
        compiler_params=_cp("parallel", "arbitrary"),
        name="moe_ffn",
    )(blk_e, n_act, xs, w_in, w_in, w_out)


def _combine_ln_kernel(h_ref, y1_ref, y2_ref, r_ref, g_ref, b_ref, o_ref, ob_ref):
    r = r_ref[...]
    ffn = r[:, 2:3] * y1_ref[...] + r[:, 3:4] * y2_ref[...]
    y = _layer_norm(ALPHA * h_ref[...] + ffn, g_ref[...], b_ref[...])
    o_ref[...] = y
    ob_ref[...] = y.astype(BF16)


def _combine_ln(h, y1, y2, route, g, b, tm=512):
    m, d = h.shape
    row = pl.BlockSpec((tm, d), lambda i: (i, 0))
    vec = pl.BlockSpec((1, d), lambda i: (0, 0))
    out_specs, out_shape = _dual_out(m, d, tm)
    return pl.pallas_call(
        _combine_ln_kernel,
        grid=(m // tm,),
        in_specs=[row, row, row, pl.BlockSpec((tm, LANES), lambda i: (i, 0)), vec, vec],
        out_specs=out_specs,
        out_shape=out_shape,
        compiler_params=_cp("parallel"),
        name="combine_ln",
    )(h, y1, y2, route, g, b)


def _pad_heads(w, heads):
    d = w.shape[0]
    w = w.reshape(d, heads, HEAD_DIM)
    return jnp.pad(w, ((0, 0), (0, 0), (0, LANES - HEAD_DIM))).reshape(d, heads * LANES)


def _rope_tables(positions):
    half = ROT_DIM // 2
    inv = ROPE_THETA ** (-jnp.arange(0, ROT_DIM, 2, dtype=F32) / ROT_DIM)
    ang = positions.astype(F32).reshape(-1, 1) * inv
    cos, sin = jnp.cos(ang), jnp.sin(ang)
    n = ang.shape[0]
    ones = jnp.ones((n, LANES - ROT_DIM), F32)
    zeros = jnp.zeros((n, LANES - half), F32)
    c = jnp.concatenate([cos, cos, ones], axis=1)
    sa = jnp.concatenate([-sin, zeros], axis=1)
    sb = jnp.concatenate([jnp.zeros((n, half), F32), sin, jnp.zeros((n, LANES - ROT_DIM), F32)], axis=1)
    return c, sa, sb


def _cmp_to_slc(ncp, nslc):
    n = np.arange(ncp)[:, None]
    j = np.arange(nslc)[None, :]
    overlap = (np.minimum(n * CMP_STRIDE + CMP_BLOCK, j * SLC_BLOCK + SLC_BLOCK)
               - np.maximum(n * CMP_STRIDE, j * SLC_BLOCK))
    m = np.clip(overlap, 0, None).astype(np.float32) / CMP_BLOCK
    m[ncp - 1, :] = 0.0
    return jnp.asarray(m, BF16)


def _compress_branch(t, pos, w1, w2, bsz, seq):
    g_ = A_KV_HEADS
    nch = seq // CMP_STRIDE
    half = CMP_STRIDE * HEAD_DIM
    a = t.reshape(bsz, nch, CMP_STRIDE, g_, HEAD_DIM).transpose(0, 1, 3, 2, 4).reshape(bsz * nch * g_, half)
    w1b = w1.astype(BF16)
    w1cat = jnp.concatenate([w1b[:half], w1b[half:]], axis=1)
    pq = _mm(a, w1cat, F32, tm=1024).reshape(bsz, nch, g_, 2 * CMP_HID)
    pp = pq[..., :CMP_HID].reshape(-1, CMP_HID)
    qq = jnp.concatenate([pq[:, 1:, :, CMP_HID:], jnp.zeros((bsz, 1, g_, CMP_HID), F32)], axis=1).reshape(-1, CMP_HID)
    pos8 = jnp.zeros((8, CMP_BLOCK * HEAD_DIM), BF16).at[0].set(pos.reshape(-1).astype(BF16))
    out = _compress(pp, qq, pos8, w1b, w2.astype(BF16))
    out = out.reshape(bsz, nch, g_, HEAD_DIM).transpose(0, 2, 1, 3)
    return out


def _nsa_layer(h, hb, tabs, w_in, w_out, cmp_pos, cmp_w1, cmp_w2, ln_g, ln_b, bsz, seq):
    o, _, _ = _nsa_attn(hb, tabs, w_in, cmp_pos, cmp_w1, cmp_w2, bsz, seq)
    return _out_ln(o, w_out.astype(BF16), h, ln_g[None, :], ln_b[None, :])


def _nsa_attn(hb, tabs, w_in, cmp_pos, cmp_w1, cmp_w2, bsz, seq):
    hb = hb.astype(BF16)
    g_ = A_KV_HEADS
    aq = A_HEADS * HEAD_DIM
    akv = g_ * HEAD_DIM
    wq = w_in[:, :aq] * (HEAD_DIM ** -0.5)
    w_kc, w_vc, w_ks, w_vs, w_kw, w_vw = (w_in[:, aq + i * akv: aq + (i + 1) * akv] for i in range(6))
    w_gl = w_in[:, aq + 6 * akv:]
    w_big = jnp.concatenate([_pad_heads(wq, A_HEADS), _pad_heads(w_ks, g_), _pad_heads(w_vs, g_),
                             _pad_heads(w_kw, g_), _pad_heads(w_vw, g_)], axis=1).astype(BF16)
    slab_modes = ["plain"] * A_HEADS + ["rope_onehot"] * g_ + ["ones"] * g_ + ["rope"] * g_ + ["ones"] * g_
    proj = _proj(hb, w_big, tabs, slab_modes)

    w_gl_g = w_gl.reshape(-1, 3, g_, A_GROUP).transpose(0, 2, 1, 3).reshape(-1, g_, 3 * A_GROUP)
    w_gl_g = jnp.pad(w_gl_g, ((0, 0), (0, 0), (0, LANES - 3 * A_GROUP))).reshape(-1, g_ * LANES)
    side = _mm(hb, jnp.concatenate([w_gl_g, w_kc, w_vc], axis=1).astype(BF16), F32, sig_cols=g_ * LANES)
    gates = side[:, :g_ * LANES]
    kc = _compress_branch(side[:, g_ * LANES:g_ * LANES + akv], cmp_pos[0], cmp_w1[0], cmp_w2[0], bsz, seq)
    vc = _compress_branch(side[:, g_ * LANES + akv:], cmp_pos[1], cmp_w1[1], cmp_w2[1], bsz, seq)
    pad = ((0, 0), (0, 0), (0, 0), (0, LANES - HEAD_DIM))
    kc = jnp.pad(kc * LOG2E, pad).astype(BF16)
    vc = jnp.pad(vc, pad).astype(BF16)
    ncp = seq // CMP_STRIDE
    cmp_map_t = _cmp_to_slc(ncp, seq // SLC_BLOCK).T

    o_c, selbias = _cmp_select(proj, kc, vc, cmp_map_t, bsz, seq)
    o = _sel_win(proj, tabs, selbias, o_c, gates, bsz, seq)
    return o, selbias, o_c


def _shared_kv(hb, tabs, w_kv):
    g_ = B_KV_HEADS
    bkv = g_ * HEAD_DIM
    w = jnp.concatenate([_pad_heads(w_kv[:, :bkv], g_), _pad_heads(w_kv[:, bkv:], g_)], axis=1).astype(BF16)
    return _proj(hb.astype(BF16), w, tabs, ["rope"] * g_ + ["ones"] * g_)


def _swa_layer(h, hb, tabs, kvproj, w_q, w_out, sinks, ln_g, ln_b, bsz, seq):
    o = _swa_attn(hb, tabs, kvproj, w_q, sinks, bsz, seq)
    return _out_ln(o, w_out.astype(BF16), h, ln_g[None, :], ln_b[None, :])


def _swa_attn(hb, tabs, kvproj, w_q, sinks, bsz, seq):
    wq = _pad_heads(w_q * (HEAD_DIM ** -0.5), B_HEADS).astype(BF16)
    qproj = _proj(hb.astype(BF16), wq, tabs, ["rope"] * B_HEADS)
    sk = jnp.broadcast_to(sinks.astype(F32).reshape(B_KV_HEADS, B_GROUP, 1), (B_KV_HEADS, B_GROUP, LANES))
    return _swa(qproj, kvproj, sk, bsz, seq)


def _moe_layer(h, hb, w_router, w_in, w_out, ln_g, ln_b):
    y1, y2, route = _moe_experts(h, hb, w_router, w_in, w_out)
    return _combine_ln(h, y1, y2, route, ln_g[None, :], ln_b[None, :])


def _moe_ffn_out(h, w_router, w_in, w_out):
    y1, y2, route = _moe_experts(h, h.astype(BF16), w_router, w_in, w_out)
    return route[:, 2:3] * y1 + route[:, 3:4] * y2


def _moe_experts(h, hb, w_router, w_in, w_out):
    n_tok, d = h.shape
    wr = jnp.pad(w_router, ((0, 0), (0, LANES - N_EXPERTS)))
    wr_hi = wr.astype(BF16)
    wr_lo = (wr - wr_hi.astype(F32)).astype(BF16)
    route = _router(h, wr_hi, wr_lo)
    top_e = route[:, :2].astype(jnp.int32)

    e_flat = top_e.reshape(-1)
    onehot = (e_flat[:, None] == jnp.arange(N_EXPERTS)[None, :]).astype(jnp.int32)
    csum = jnp.cumsum(onehot, axis=0)
    counts = csum[-1]
    rank = jnp.take_along_axis(csum, e_flat[:, None], axis=1)[:, 0] - 1
    padded = (counts + MOE_TM - 1) // MOE_TM * MOE_TM
    pad_end = jnp.cumsum(padded)
    pad_start = pad_end - padded
    dest = pad_start[e_flat] + rank
    cap = n_tok * 2 + N_EXPERTS * MOE_TM
    n_blk = cap // MOE_TM
    tok_flat = jnp.repeat(jnp.arange(n_tok, dtype=jnp.int32), 2)
    buf_tok = jnp.zeros((cap,), jnp.int32).at[dest].set(tok_flat)
    blk_start = jnp.arange(n_blk, dtype=jnp.int32) * MOE_TM
    blk_e = jnp.sum((pad_end[None, :] <= blk_start[:, None]).astype(jnp.int32), axis=1)
    blk_e = jnp.minimum(blk_e, N_EXPERTS - 1).astype(jnp.int32)

    xs = hb[buf_tok]
    n_act = (pad_end[-1:] // MOE_TM).astype(jnp.int32)
    y = _moe_ffn(xs, blk_e, n_act, w_in.astype(BF16), w_out.astype(BF16))
    dest2 = dest.reshape(n_tok, 2)
    y1 = y[dest2[:, 0]]
    y2 = y[dest2[:, 1]]
    return y1, y2, route


def kernel(x, positions, w_in_a, w_out_a, cmp_pos, cmp_w1, cmp_w2, w_kv_shared, w_q_b, w_out_b, sinks_b,
           ln_g, ln_b, dense_w_in, dense_w_out, moe_router, moe_w_in, moe_w_out):
    bsz, seq, d = x.shape
    n_a = DEPTH // 2
    tabs = _rope_tables(positions)
    h = x.reshape(bsz * seq, d)
    hb = h.astype(BF16)
    kvproj = None
    for l in range(DEPTH):
        if l < n_a:
            h, hb = _nsa_layer(h, hb, tabs, w_in_a[l], w_out_a[l], cmp_pos[l], cmp_w1[l], cmp_w2[l],
                               ln_g[l, 0], ln_b[l, 0], bsz, seq)
        else:
            b = l - n_a
            h, hb = _swa_layer(h, hb, tabs, kvproj, w_q_b[b], w_out_b[b], sinks_b[b], ln_g[l, 0], ln_b[l, 0],
                               bsz, seq)
        if l % 2 == 0:
            h, hb = _ffn_dense(h, dense_w_in[l // 2].astype(BF16), dense_w_out[l // 2].astype(BF16),
                               ln_g[l, 1][None, :], ln_b[l, 1][None, :])
        else:
            h, hb = _moe_layer(h, hb, moe_router[l // 2], moe_w_in[l // 2], moe_w_out[l // 2],
                               ln_g[l, 1], ln_b[l, 1])
        if l == n_a - 1:
            kvproj = _shared_kv(hb, tabs, w_kv_shared)
    return h.reshape(bsz, seq, d)
```

```python
import functools

import numpy as np
import jax
import jax.numpy as jnp
from jax import lax
from jax.experimental import pallas as pl
from jax.experimental.pallas import tpu as pltpu

F32 = jnp.float32
BF16 = jnp.bfloat16

D_MODEL = 1024
DEPTH = 4
HEAD_DIM = 64
LANES = 128
ROT_DIM = HEAD_DIM // 4
ROPE_THETA = 500000.0
A_HEADS = 16
A_KV_HEADS = 4
A_GROUP = 4
CMP_BLOCK = 32
CMP_STRIDE = 16
CMP_HID = 256
SLC_BLOCK = 64
N_SEL = 16
N_FORCED = 3
NSA_WIN = 512
B_HEADS = 16
B_KV_HEADS = 2
B_GROUP = 8
SWA_WIN = 128
D_FF = 2816
N_EXPERTS = 8
D_FF_EXPERT = 3584
ALPHA = (2 * DEPTH) ** 0.25
LN_EPS = 1e-5
NEG = -1e30
FORCE = 1e9

TQB = 256
TQA = 256
CMP_CHUNK = 256
LOG2E = 1.4426950408889634
KT = 512
BLK_PER_KT = KT // SLC_BLOCK
KT_GROUP = 4
MOE_TM = 512
VMEM_LIMIT = 56 * 1024 * 1024


def _cp(*sem):
    return pltpu.CompilerParams(dimension_semantics=sem, vmem_limit_bytes=VMEM_LIMIT)


def _dot(a, b):
    return jnp.dot(a, b, preferred_element_type=F32)


def _dot_nt(a, b):
    return lax.dot_general(a, b, (((1,), (1,)), ((), ())), preferred_element_type=F32)


def _sigmoid(x):
    return 1.0 / (1.0 + jnp.exp(-x))


def _layer_norm(z, g, b):
    mu = jnp.mean(z, axis=-1, keepdims=True)
    zc = z - mu
    var = jnp.mean(zc * zc, axis=-1, keepdims=True)
    return zc * lax.rsqrt(var + LN_EPS) * g + b


def _rope(x, c, sa, sb):
    w = x.shape[1]
    return x * c + pltpu.roll(x, w - ROT_DIM // 2, 1) * sa + pltpu.roll(x, ROT_DIM // 2, 1) * sb


def _tile_lanes(t, n):
    return t if n == 1 else jnp.concatenate([t] * n, axis=1)


def _tile_rows(t, n):
    return t if n == 1 else jnp.concatenate([t] * n, axis=0)


def _normalize(acc, extra=0.0):
    den = pltpu.roll(acc, HEAD_DIM, 1) + extra
    return (acc / den)[:, :HEAD_DIM]


def _heads_to_lanes(o, heads):
    t = o.shape[0] // heads
    return jnp.concatenate([o[r * t:(r + 1) * t] for r in range(heads)], axis=1)


def _mm_kernel(x_ref, w_ref, o_ref):
    o_ref[...] = _dot(x_ref[...].astype(BF16), w_ref[...]).astype(o_ref.dtype)


def _mm(x, w, out_dtype, tm=512, tn=None):
    m, k = x.shape
    n = w.shape[1]
    tm = min(tm, m)
    tn = n if tn is None else tn
    return pl.pallas_call(
        _mm_kernel,
        grid=(m // tm, n // tn),
        in_specs=[pl.BlockSpec((tm, k), lambda i, j: (i, 0)), pl.BlockSpec((k, tn), lambda i, j: (0, j))],
        out_specs=pl.BlockSpec((tm, tn), lambda i, j: (i, j)),
        out_shape=jax.ShapeDtypeStruct((m, n), out_dtype),
        compiler_params=_cp("parallel", "arbitrary"),
        name="mm",
    )(x, w)


def _nsa_side_kernel(x_ref, w_ref, gate_ref, kc_ref, vc_ref):
    acc = _dot(x_ref[...], w_ref[...])
    ng = gate_ref.shape[1]
    nk = kc_ref.shape[1]
    gate_ref[...] = _sigmoid(acc[:, :ng])
    kc_ref[...] = acc[:, ng:ng + nk]
    vc_ref[...] = acc[:, ng + nk:]


def _nsa_side(hb, w, n_gate, n_kv, tm=512):
    m, k = hb.shape
    row = lambda n: pl.BlockSpec((tm, n), lambda i: (i, 0))
    return pl.pallas_call(
        _nsa_side_kernel,
        grid=(m // tm,),
        in_specs=[row(k), pl.BlockSpec(w.shape, lambda i: (0, 0))],
        out_specs=[row(n_gate), row(n_kv), row(n_kv)],
        out_shape=[jax.ShapeDtypeStruct((m, n_gate), F32), jax.ShapeDtypeStruct((m, n_kv), F32),
                   jax.ShapeDtypeStruct((m, n_kv), F32)],
        compiler_params=_cp("parallel"),
        name="nsa_side",
    )(hb, w)


def _cast_kernel(x_ref, o_ref):
    o_ref[...] = x_ref[...].astype(o_ref.dtype)


def _cast_layer_bf16(w, layer, tc, tr=512):
    _, e, r, c = w.shape
    return pl.pallas_call(
        _cast_kernel,
        grid=(e, r // tr, c // tc),
        in_specs=[pl.BlockSpec((None, None, tr, tc), lambda ei, i, j: (layer, ei, i, j))],
        out_specs=pl.BlockSpec((None, tr, tc), lambda ei, i, j: (ei, i, j)),
        out_shape=jax.ShapeDtypeStruct((e, r, c), BF16),
        compiler_params=_cp("parallel", "parallel", "arbitrary"),
        name="cast_bf16",
    )(w)


def _proj_kernel(x_ref, w_ref, c_ref, sa_ref, sb_ref, o_ref, *, block_modes, tm, tn):
    i = pl.program_id(0)
    j = pl.program_id(1)
    acc = _dot(x_ref[...], w_ref[...])

    def piece(mode, s0, ns):
        a = acc[:, s0 * LANES:(s0 + ns) * LANES]
        lane = lax.broadcasted_iota(jnp.int32, a.shape, 1) % LANES
        if mode in ("rope", "rope_onehot"):
            a = _rope(a, _tile_lanes(c_ref[...], ns), _tile_lanes(sa_ref[...], ns), _tile_lanes(sb_ref[...], ns))
        if mode == "rope_onehot":
            row = i * tm + lax.broadcasted_iota(jnp.int32, a.shape, 0)
            a = a + jnp.where(lane == HEAD_DIM + (row // SLC_BLOCK) % BLK_PER_KT, 1.0, 0.0)
        if mode == "ones":
            a = a + jnp.where(lane >= HEAD_DIM, 1.0, 0.0)
        return a.astype(o_ref.dtype)

    for modes in sorted(set(block_modes)):
        pred = functools.reduce(jnp.logical_or, [j == jj for jj, mm in enumerate(block_modes) if mm == modes])

        @pl.when(pred)
        def _(modes=modes):
            runs = []
            for s, mode in enumerate(modes):
                if runs and runs[-1][0] == mode:
                    runs[-1][2] += 1
                else:
                    runs.append([mode, s, 1])
            pieces = [piece(*r) for r in runs]
            o_ref[...] = pieces[0] if len(pieces) == 1 else jnp.concatenate(pieces, axis=1)


def _proj(x, w, tabs, slab_modes, tn=1024, tm=1024):
    m, k = x.shape
    n = w.shape[1]
    tn = min(tn, n)
    spb = tn // LANES
    block_modes = tuple(tuple(slab_modes[jj * spb:(jj + 1) * spb]) for jj in range(n // tn))
    c, sa, sb = tabs
    return pl.pallas_call(
        functools.partial(_proj_kernel, block_modes=block_modes, tm=tm, tn=tn),
        grid=(m // tm, n // tn),
        in_specs=[
            pl.BlockSpec((tm, k), lambda i, j: (i, 0)),
            pl.BlockSpec((k, tn), lambda i, j: (0, j)),
            pl.BlockSpec((tm, LANES), lambda i, j: (i, 0)),
            pl.BlockSpec((tm, LANES), lambda i, j: (i, 0)),
            pl.BlockSpec((tm, LANES), lambda i, j: (i, 0)),
        ],
        out_specs=pl.BlockSpec((tm, tn), lambda i, j: (i, j)),
        out_shape=jax.ShapeDtypeStruct((m, n), BF16),
        compiler_params=_cp("parallel", "arbitrary"),
        name="proj",
    )(x, w, c, sa, sb)


def _out_ln_kernel(x_ref, w_ref, h_ref, g_ref, b_ref, o_ref, ob_ref):
    mix = _dot(x_ref[...], w_ref[...])
    y = _layer_norm(ALPHA * h_ref[...] + mix, g_ref[...], b_ref[...])
    o_ref[...] = y
    ob_ref[...] = y.astype(BF16)


def _dual_out(m, d, tm):
    spec = pl.BlockSpec((tm, d), lambda i, *_: (i, 0))
    return [spec, spec], [jax.ShapeDtypeStruct((m, d), F32), jax.ShapeDtypeStruct((m, d), BF16)]


def _out_ln(x, w, h, g, b, tm=512):
    m, k = x.shape
    d = w.shape[1]
    out_specs, out_shape = _dual_out(m, d, tm)
    return pl.pallas_call(
        _out_ln_kernel,
        grid=(m // tm,),
        in_specs=[
            pl.BlockSpec((tm, k), lambda i: (i, 0)),
            pl.BlockSpec((k, d), lambda i: (0, 0)),
            pl.BlockSpec((tm, d), lambda i: (i, 0)),
            pl.BlockSpec((1, d), lambda i: (0, 0)),
            pl.BlockSpec((1, d), lambda i: (0, 0)),
        ],
        out_specs=out_specs,
        out_shape=out_shape,
        compiler_params=_cp("parallel"),
        name="out_ln",
    )(x, w, h, g, b)


def _ffn_kernel(x_ref, wg_ref, wu_ref, wo_ref, g_ref, b_ref, o_ref, ob_ref, acc_ref):
    c = pl.program_id(1)

    @pl.when(c == 0)
    def _():
        acc_ref[...] = jnp.zeros_like(acc_ref)

    xb = x_ref[...].astype(BF16)
    gate = _dot(xb, wg_ref[...])
    up = _dot(xb, wu_ref[...])
    a = gate * _sigmoid(gate) * up
    acc_ref[...] += _dot(a.astype(BF16), wo_ref[...])

    @pl.when(c == pl.num_programs(1) - 1)
    def _():
        y = _layer_norm(ALPHA * x_ref[...] + acc_ref[...], g_ref[...], b_ref[...])
        o_ref[...] = y
        ob_ref[...] = y.astype(BF16)


def _ffn_dense(h, w_in, w_out, g, b, tm=512):
    m, d = h.shape
    ff = w_out.shape[0]
    fc, nc = ff, 1
    once = pl.Buffered(1)
    out_specs, out_shape = _dual_out(m, d, tm)
    return pl.pallas_call(
        _ffn_kernel,
        grid=(m // tm, nc),
        in_specs=[
            pl.BlockSpec((tm, d), lambda i, c: (i, 0)),
            pl.BlockSpec((d, fc), lambda i, c: (0, c), pipeline_mode=once),
            pl.BlockSpec((d, fc), lambda i, c: (0, nc + c), pipeline_mode=once),
            pl.BlockSpec((fc, d), lambda i, c: (c, 0), pipeline_mode=once),
            pl.BlockSpec((1, d), lambda i, c: (0, 0)),
            pl.BlockSpec((1, d), lambda i, c: (0, 0)),
        ],
        out_specs=out_specs,
        out_shape=out_shape,
        scratch_shapes=[pltpu.VMEM((tm, d), F32)],
        compiler_params=_cp("parallel", "arbitrary"),
        name="ffn_dense",
    )(h, w_in, w_in, w_out, g, b)


def _compress_kernel(p_ref, q_ref, pos_ref, w1_ref, w2_ref, o_ref):
    posb = _dot(pos_ref[...], w1_ref[...])[0:1, :]
    hid = p_ref[...] + q_ref[...] + posb
    act = jax.nn.gelu(hid, approximate=True)
    o_ref[...] = _dot(act.astype(BF16), w2_ref[...])


def _compress(pp, qq, pos8, w1, w2, tm=1024):
    m, hid = pp.shape
    dh = w2.shape[1]
    return pl.pallas_call(
        _compress_kernel,
        grid=(m // tm,),
        in_specs=[
            pl.BlockSpec((tm, hid), lambda i: (i, 0)),
            pl.BlockSpec((tm, hid), lambda i: (i, 0)),
            pl.BlockSpec(pos8.shape, lambda i: (0, 0)),
            pl.BlockSpec(w1.shape, lambda i: (0, 0)),
            pl.BlockSpec(w2.shape, lambda i: (0, 0)),
        ],
        out_specs=pl.BlockSpec((tm, dh), lambda i: (i, 0)),
        out_shape=jax.ShapeDtypeStruct((m, dh), F32),
        compiler_params=_cp("parallel"),
        name="compress",
    )(pp, qq, pos8, w1, w2)


def _cmp_select_kernel(q_ref, kc_ref, vc_ref, mapt_ref, oc_ref, sb_ref, *, ncp, nslc):
    qi = pl.program_id(2)
    t0 = qi * TQA
    rows = A_GROUP * TQA
    qb = q_ref[...]
    q4 = jnp.concatenate([qb[:, r * LANES:(r + 1) * LANES] for r in range(A_GROUP)], axis=0)

    def branch(nb):
        nk = (nb + 1) * CMP_CHUNK
        nr = nk * CMP_STRIDE // SLC_BLOCK
        s = _dot_nt(q4, kc_ref[0:nk, :])
        t_tok = t0 + lax.broadcasted_iota(jnp.int32, (TQA, nk), 0)
        n_idx = lax.broadcasted_iota(jnp.int32, (TQA, nk), 1)
        valid = (n_idx * CMP_STRIDE + CMP_BLOCK - 1 <= t_tok) & (n_idx < ncp - 1)
        s3 = s.reshape(A_GROUP, TQA, nk) + jnp.where(valid, 0.0, NEG)[None]
        m = jnp.max(s3, axis=2, keepdims=True)
        p = jnp.exp2(s3 - m)
        l = jnp.sum(p, axis=2, keepdims=True)
        pn = p * jnp.where(m > 0.5 * NEG, 1.0 / l, 0.0)
        oc = _dot(pn.reshape(rows, nk).astype(BF16), vc_ref[0:nk, :])
        oc_ref[...] = oc[:, :HEAD_DIM].reshape(A_GROUP, TQA, HEAD_DIM)
        psum = pn[0] + pn[1] + pn[2] + pn[3]
        p_hi = psum.astype(BF16)
        p_lo = (psum - p_hi.astype(F32)).astype(BF16)
        mapt = mapt_ref[0:nr, 0:nk]
        imp = _dot_nt(mapt, p_hi) + _dot_nt(mapt, p_lo)

        j_idx = lax.broadcasted_iota(jnp.int32, (nr, TQA), 0)
        cur = (t0 + lax.broadcasted_iota(jnp.int32, (nr, TQA), 1)) // SLC_BLOCK
        forced = (j_idx == 0) | (j_idx == cur) | (j_idx == cur - 1)
        score = jnp.where(j_idx > cur, -1.0, jnp.where(forced, -jnp.inf, imp))
        for _ in range(N_SEL - N_FORCED):
            mx = jnp.max(score, axis=0, keepdims=True)
            first = jnp.min(jnp.where(score == mx, j_idx, nr), axis=0, keepdims=True)
            score = jnp.where(j_idx == first, -jnp.inf, score)
        sb_t = jnp.where((score == -jnp.inf) & (j_idx <= cur), 0.0, NEG)
        if nr < nslc:
            sb_t = jnp.concatenate([sb_t, jnp.full((nslc - nr, TQA), NEG, F32)], axis=0)
        sb_ref[...] = sb_t.T

    bucket = ((qi + 1) * (TQA // CMP_STRIDE) - 1) // CMP_CHUNK
    for nb in range(ncp // CMP_CHUNK):
        pl.when(bucket == nb)(functools.partial(branch, nb))


def _cmp_select(proj, kc_cmp, vc_cmp, cmp_map_t, bsz, seq):
    ncp = kc_cmp.shape[2]
    nslc = seq // SLC_BLOCK
    nq = seq // TQA
    g_ = A_KV_HEADS
    assert nslc >= N_SEL and ncp % CMP_CHUNK == 0
    return pl.pallas_call(
        functools.partial(_cmp_select_kernel, ncp=ncp, nslc=nslc),
        grid=(bsz, g_, nq),
        in_specs=[
            pl.BlockSpec((TQA, A_GROUP * LANES), lambda b, g, i: (b * nq + i, g)),
            pl.BlockSpec((None, None, ncp, LANES), lambda b, g, i: (b, g, 0, 0)),
            pl.BlockSpec((None, None, ncp, LANES), lambda b, g, i: (b, g, 0, 0)),
            pl.BlockSpec((nslc, ncp), lambda b, g, i: (0, 0)),
        ],
        out_specs=[
            pl.BlockSpec((None, None, A_GROUP, TQA, HEAD_DIM), lambda b, g, i: (b, g, 0, i, 0)),
            pl.BlockSpec((TQA, nslc), lambda b, g, i: (b * nq + i, g)),
        ],
        out_shape=[
            jax.ShapeDtypeStruct((bsz, g_, A_GROUP, seq, HEAD_DIM), F32),
            jax.ShapeDtypeStruct((bsz * seq, g_ * nslc), F32),
        ],
        compiler_params=_cp("parallel", "parallel", "arbitrary"),
        name="cmp_select",
    )(proj, kc_cmp, vc_cmp, cmp_map_t)


def _sel_win_kernel(q_ref, c_ref, sa_ref, sb_ref, bias_ref, ks_ref, vs_ref, kw_ref, vw_ref, oc_ref, gate_ref,
                    o_ref, *, nslc):
    qi = pl.program_id(2)
    t0 = qi * TQA
    rows = A_GROUP * TQA
    qb = q_ref[...]
    q4 = jnp.concatenate([qb[:, r * LANES:(r + 1) * LANES] for r in range(A_GROUP)], axis=0).astype(F32)
    q_rot = _rope(q4, _tile_rows(c_ref[...], A_GROUP), _tile_rows(sa_ref[...], A_GROUP),
                  _tile_rows(sb_ref[...], A_GROUP)) * LOG2E
    q_rot_b = q_rot.astype(BF16)
    bias = bias_ref[...]
    lane = lax.broadcasted_iota(jnp.int32, (TQA, LANES), 1)
    bias_lanes = (lane >= HEAD_DIM) & (lane < HEAD_DIM + BLK_PER_KT)
    t_q = t0 + lax.broadcasted_iota(jnp.int32, (TQA, KT), 0)
    k_off = lax.broadcasted_iota(jnp.int32, (TQA, KT), 1)

    def add_mask(s, mask_bias):
        return (s.reshape(A_GROUP, TQA, -1) + mask_bias[None]).reshape(s.shape)

    def scores(kt):
        shift = (HEAD_DIM + nslc - BLK_PER_KT * kt) % nslc
        rolled = pltpu.roll(bias, shift, 1)[:, :LANES]
        qa = (q_rot + _tile_rows(jnp.where(bias_lanes, rolled, 0.0), A_GROUP)).astype(BF16)
        k0 = pl.multiple_of(kt * KT, KT)
        return _dot_nt(qa, ks_ref[pl.ds(k0, KT), :])

    def causal_bias(kt):
        return jnp.where(kt * KT + k_off <= t_q, 0.0, NEG)

    def group(kt0, n, carry, causal):
        m, acc = carry
        ss = [scores(kt0 + i) for i in range(n)]
        if causal:
            ss = [add_mask(s, causal_bias(kt0 + i)) for i, s in enumerate(ss)]
        m_row = functools.reduce(jnp.maximum, [jnp.max(s, axis=1, keepdims=True) for s in ss])
        m_new = jnp.maximum(m, jnp.broadcast_to(m_row, (rows, LANES)))
        m_keys = _tile_lanes(m_new, KT // LANES)
        pv = None
        for i, s in enumerate(ss):
            k0 = pl.multiple_of((kt0 + i) * KT, KT)
            d = _dot(jnp.exp2(s - m_keys).astype(BF16), vs_ref[pl.ds(k0, KT), :])
            pv = d if pv is None else pv + d
        return m_new, jnp.exp2(m - m_new) * acc + pv

    kt_d = t0 // KT
    n_quads = kt_d // KT_GROUP
    carry = (jnp.full((rows, LANES), NEG, F32), jnp.zeros((rows, LANES), F32))
    carry = lax.fori_loop(0, n_quads, lambda j, c: group(j * KT_GROUP, KT_GROUP, c, False), carry)
    carry = lax.fori_loop(n_quads * KT_GROUP, kt_d, lambda kt, c: group(kt, 1, c, False), carry)
    _, acc_s = group(kt_d, 1, carry, True)
    o_s = _normalize(acc_s)

    wspan = NSA_WIN + TQA
    w0 = pl.multiple_of(jnp.maximum(t0 - NSA_WIN, 0), TQA)
    s_w = _dot_nt(q_rot_b, kw_ref[pl.ds(w0, wspan), :])
    tw = t0 + lax.broadcasted_iota(jnp.int32, (TQA, wspan), 0)
    diff = tw - (w0 + lax.broadcasted_iota(jnp.int32, (TQA, wspan), 1))
    s_w = add_mask(s_w, jnp.where((diff >= 0) & (diff < NSA_WIN), 0.0, NEG))
    p_w = jnp.exp2(s_w - jnp.max(s_w, axis=1, keepdims=True))
    acc_w = _dot(p_w.astype(BF16), vw_ref[pl.ds(w0, wspan), :])
    o_w = _normalize(acc_w)

    gates = gate_ref[...]

    def gcol(c):
        return jnp.concatenate([gates[:, c * A_GROUP + r:c * A_GROUP + r + 1] for r in range(A_GROUP)], axis=0)

    o_c = oc_ref[...].reshape(rows, HEAD_DIM)
    o = gcol(0) * o_c + gcol(1) * o_s + gcol(2) * o_w
    o_ref[...] = _heads_to_lanes(o, A_GROUP).astype(o_ref.dtype)


def _sel_win(proj, tabs, selbias, o_c, gates, bsz, seq):
    nslc = seq // SLC_BLOCK
    nq = seq // TQA
    g_ = A_KV_HEADS
    c, sa, sb = tabs
    tab_spec = pl.BlockSpec((TQA, LANES), lambda b, g, i: (b * nq + i, 0))
    q_slabs = A_HEADS

    def kv_spec(base):
        return pl.BlockSpec((seq, LANES), lambda b, g, i: (b, base + g))

    return pl.pallas_call(
        functools.partial(_sel_win_kernel, nslc=nslc),
        grid=(bsz, g_, nq),
        in_specs=[
            pl.BlockSpec((TQA, A_GROUP * LANES), lambda b, g, i: (b * nq + i, g)),
            tab_spec, tab_spec, tab_spec,
            pl.BlockSpec((TQA, nslc), lambda b, g, i: (b * nq + i, g)),
            kv_spec(q_slabs), kv_spec(q_slabs + g_), kv_spec(q_slabs + 2 * g_), kv_spec(q_slabs + 3 * g_),
            pl.BlockSpec((None, None, A_GROUP, TQA, HEAD_DIM), lambda b, g, i: (b, g, 0, i, 0)),
            pl.BlockSpec((TQA, LANES), lambda b, g, i: (b * nq + i, g)),
        ],
        out_specs=pl.BlockSpec((TQA, A_GROUP * HEAD_DIM), lambda b, g, i: (b * nq + i, g)),
        out_shape=jax.ShapeDtypeStruct((bsz * seq, A_HEADS * HEAD_DIM), BF16),
        compiler_params=_cp("parallel", "parallel", "arbitrary"),
        name="sel_win",
    )(proj, c, sa, sb, selbias, proj, proj, proj, proj, o_c, gates)


def _swa_kernel(q_ref, k_ref, v_ref, sink_ref, o_ref):
    qi = pl.program_id(2)
    t0 = qi * TQB
    rows = B_GROUP * TQB
    span = SWA_WIN + TQB
    qb = q_ref[...]
    q8 = jnp.concatenate([qb[:, r * LANES:(r + 1) * LANES] for r in range(B_GROUP)], axis=0)
    w0 = pl.multiple_of(jnp.maximum(t0 - SWA_WIN, 0), SWA_WIN)
    s = _dot_nt(q8, k_ref[pl.ds(w0, span), :])
    tq = t0 + lax.broadcasted_iota(jnp.int32, (TQB, span), 0)
    diff = tq - (w0 + lax.broadcasted_iota(jnp.int32, (TQB, span), 1))
    band = jnp.where((diff >= 0) & (diff < SWA_WIN), 0.0, NEG)
    s = (s.reshape(B_GROUP, TQB, span) + band[None]).reshape(rows, span)
    sk = sink_ref[...]
    sink = jnp.concatenate([jnp.broadcast_to(sk[r:r + 1, :], (TQB, LANES)) for r in range(B_GROUP)], axis=0)
    m = jnp.maximum(jnp.broadcast_to(jnp.max(s, axis=1, keepdims=True), (rows, LANES)), sink)
    e = jnp.exp(s - _tile_lanes(m, span // LANES))
    acc = _dot(e.astype(BF16), v_ref[pl.ds(w0, span), :])
    o = _normalize(acc, jnp.exp(sink - m))
    o_ref[...] = _heads_to_lanes(o, B_GROUP).astype(o_ref.dtype)


def _swa(qproj, kvproj, sinks, bsz, seq):
    nq = seq // TQB
    g_ = B_KV_HEADS
    return pl.pallas_call(
        _swa_kernel,
        grid=(bsz, g_, nq),
        in_specs=[
            pl.BlockSpec((TQB, B_GROUP * LANES), lambda b, g, i: (b * nq + i, g)),
            pl.BlockSpec((seq, LANES), lambda b, g, i: (b, g)),
            pl.BlockSpec((seq, LANES), lambda b, g, i: (b, g_ + g)),
            pl.BlockSpec((None, B_GROUP, LANES), lambda b, g, i: (g, 0, 0)),
        ],
        out_specs=pl.BlockSpec((TQB, B_GROUP * HEAD_DIM), lambda b, g, i: (b * nq + i, g)),
        out_shape=jax.ShapeDtypeStruct((bsz * seq, B_HEADS * HEAD_DIM), BF16),
        compiler_params=_cp("parallel", "parallel", "arbitrary"),
        name="swa",
    )(qproj, kvproj, kvproj, sinks)


def _router_kernel(x_ref, wh_ref, wl_ref, o_ref):
    x = x_ref[...]
    xh = x.astype(BF16)
    xl = (x - xh.astype(F32)).astype(BF16)
    logits = _dot(xh, wh_ref[...]) + _dot(xh, wl_ref[...]) + _dot(xl, wh_ref[...])
    lane = lax.broadcasted_iota(jnp.int32, logits.shape, 1)
    lg = jnp.where(lane < N_EXPERTS, logits, -jnp.inf)
    m1 = jnp.max(lg, axis=1, keepdims=True)
    i1 = jnp.min(jnp.where(lg == m1, lane, LANES), axis=1, keepdims=True)
    lg2 = jnp.where(lane == i1, -jnp.inf, lg)
    m2 = jnp.max(lg2, axis=1, keepdims=True)
    i2 = jnp.min(jnp.where(lg2 == m2, lane, LANES), axis=1, keepdims=True)
    e2 = jnp.exp(m2 - m1)
    g1 = 1.0 / (1.0 + e2)
    g2 = e2 * g1
    out = jnp.where(lane == 0, i1.astype(F32), jnp.where(lane == 1, i2.astype(F32),
                    jnp.where(lane == 2, g1, jnp.where(lane == 3, g2, 0.0))))
    o_ref[...] = out


def _router(h, w_hi, w_lo, tm=512):
    m, d = h.shape
    return pl.pallas_call(
        _router_kernel,
        grid=(m // tm,),
        in_specs=[pl.BlockSpec((tm, d), lambda i: (i, 0)), pl.BlockSpec((d, LANES), lambda i: (0, 0)),
                  pl.BlockSpec((d, LANES), lambda i: (0, 0))],
        out_specs=pl.BlockSpec((tm, LANES), lambda i: (i, 0)),
        out_shape=jax.ShapeDtypeStruct((m, LANES), F32),
        compiler_params=_cp("parallel"),
        name="router",
    )(h, w_hi, w_lo)


def _moe_ffn_kernel(blk_e_ref, n_act_ref, x_ref, wg_ref, wu_ref, wo_ref, o_ref, acc_ref):
    i = pl.program_id(0)
    c = pl.program_id(1)

    @pl.when(c == 0)
    def _():
        acc_ref[...] = jnp.zeros_like(acc_ref)

    @pl.when(i < n_act_ref[0])
    def _():
        xb = x_ref[...]
        gate = _dot(xb, wg_ref[...])
        up = _dot(xb, wu_ref[...])
        a = gate * _sigmoid(gate) * up
        acc_ref[...] += _dot(a.astype(BF16), wo_ref[...])

    @pl.when(c == pl.num_programs(1) - 1)
    def _():
        o_ref[...] = acc_ref[...]


def _moe_ffn(xs, blk_e, n_act, w_in, w_out, tm=MOE_TM, fc=1792):
    cap, d = xs.shape
    ff = w_out.shape[1]
    nc = ff // fc

    def chunk(i, c, n_act_ref):
        return jnp.where(i < n_act_ref[0], c, nc - 1)

    grid_spec = pltpu.PrefetchScalarGridSpec(
        num_scalar_prefetch=2,
        grid=(cap // tm, nc),
        in_specs=[
            pl.BlockSpec((tm, d), lambda i, c, e, na: (i, 0)),
            pl.BlockSpec((None, d, fc), lambda i, c, e, na: (e[i], 0, chunk(i, c, na))),
            pl.BlockSpec((None, d, fc), lambda i, c, e, na: (e[i], 0, nc + chunk(i, c, na))),
            pl.BlockSpec((None, fc, d), lambda i, c, e, na: (e[i], chunk(i, c, na), 0)),
        ],
        out_specs=pl.BlockSpec((tm, d), lambda i, c, e, na: (i, 0)),
        scratch_shapes=[pltpu.VMEM((tm, d), F32)],
    )
    return pl.pallas_call(
        _moe_ffn_kernel,
        grid_spec=grid_spec,
        out_shape=jax.ShapeDtypeStruct((cap, d), F32),
        compiler_params=_cp("parallel", "arbitrary"),
        name="moe_ffn",
    )(blk_e, n_act, xs, w_in, w_in, w_out)


def _combine_ln_kernel(h_ref, y1_ref, y2_ref, r_ref, g_ref, b_ref, o_ref, ob_ref):
    r = r_ref[...]
    ffn = r[:, 2:3] * y1_ref[...] + r[:, 3:4] * y2_ref[...]
    y = _layer_norm(ALPHA * h_ref[...] + ffn, g_ref[...], b_ref[...])
    o_ref[...] = y
    ob_ref[...] = y.astype(BF16)


def _combine_ln(h, y1, y2, route, g, b, tm=512):
    m, d = h.shape
    row = pl.BlockSpec((tm, d), lambda i: (i, 0))
    vec = pl.BlockSpec((1, d), lambda i: (0, 0))
    out_specs, out_shape = _dual_out(m, d, tm)
    return pl.pallas_call(
        _combine_ln_kernel,
        grid=(m // tm,),
        in_specs=[row, row, row, pl.BlockSpec((tm, LANES), lambda i: (i, 0)), vec, vec],
        out_specs=out_specs,
        out_shape=out_shape,
        compiler_params=_cp("parallel"),
        name="combine_ln",
    )(h, y1, y2, route, g, b)


def _pad_heads(w, heads):
    d = w.shape[0]
    w = w.reshape(d, heads, HEAD_DIM)
    return jnp.pad(w, ((0, 0), (0, 0), (0, LANES - HEAD_DIM))).reshape(d, heads * LANES)


def _rope_tables(positions):
    half = ROT_DIM // 2
    inv = ROPE_THETA ** (-jnp.arange(0, ROT_DIM, 2, dtype=F32) / ROT_DIM)
    ang = positions.astype(F32).reshape(-1, 1) * inv
    cos, sin = jnp.cos(ang), jnp.sin(ang)
    n = ang.shape[0]
    ones = jnp.ones((n, LANES - ROT_DIM), F32)
    zeros = jnp.zeros((n, LANES - half), F32)
    c = jnp.concatenate([cos, cos, ones], axis=1)
    sa = jnp.concatenate([-sin, zeros], axis=1)
    sb = jnp.concatenate([jnp.zeros((n, half), F32), sin, jnp.zeros((n, LANES - ROT_DIM), F32)], axis=1)
    return c, sa, sb


def _cmp_to_slc(ncp, nslc):
    n = np.arange(ncp)[:, None]
    j = np.arange(nslc)[None, :]
    overlap = (np.minimum(n * CMP_STRIDE + CMP_BLOCK, j * SLC_BLOCK + SLC_BLOCK)
               - np.maximum(n * CMP_STRIDE, j * SLC_BLOCK))
    m = np.clip(overlap, 0, None).astype(np.float32) / CMP_BLOCK
    m[ncp - 1, :] = 0.0
    return jnp.asarray(m, BF16)


def _compress_branch(t, pos, w1, w2, bsz, seq):
    g_ = A_KV_HEADS
    nch = seq // CMP_STRIDE
    half = CMP_STRIDE * HEAD_DIM
    a = t.reshape(bsz * nch, CMP_STRIDE * g_ * HEAD_DIM)
    w1b = w1.astype(BF16)
    w1cat = jnp.concatenate([w1b[:half], w1b[half:]], axis=1)
    eye = jnp.eye(g_, dtype=BF16)
    w1blk = (w1cat.reshape(CMP_STRIDE, 1, HEAD_DIM, 1, 2 * CMP_HID) * eye[None, :, None, :, None])
    w1blk = w1blk.reshape(CMP_STRIDE * g_ * HEAD_DIM, g_ * 2 * CMP_HID)
    pq = _mm(a, w1blk, F32, tm=256, tn=2 * CMP_HID).reshape(bsz, nch, g_, 2 * CMP_HID)
    pp = pq[..., :CMP_HID].reshape(-1, CMP_HID)
    qq = jnp.concatenate([pq[:, 1:, :, CMP_HID:], jnp.zeros((bsz, 1, g_, CMP_HID), F32)], axis=1).reshape(-1, CMP_HID)
    pos8 = jnp.zeros((8, CMP_BLOCK * HEAD_DIM), BF16).at[0].set(pos.reshape(-1).astype(BF16))
    out = _compress(pp, qq, pos8, w1b, w2.astype(BF16))
    out = out.reshape(bsz, nch, g_, HEAD_DIM).transpose(0, 2, 1, 3)
    return out


def _nsa_layer(h, hb, tabs, w_in, w_out, cmp_pos, cmp_w1, cmp_w2, ln_g, ln_b, bsz, seq):
    o, _, _ = _nsa_attn(hb, tabs, w_in, cmp_pos, cmp_w1, cmp_w2, bsz, seq)
    return _out_ln(o, w_out.astype(BF16), h, ln_g[None, :], ln_b[None, :])


def _nsa_attn(hb, tabs, w_in, cmp_pos, cmp_w1, cmp_w2, bsz, seq):
    hb = hb.astype(BF16)
    g_ = A_KV_HEADS
    aq = A_HEADS * HEAD_DIM
    akv = g_ * HEAD_DIM
    wq = w_in[:, :aq] * (HEAD_DIM ** -0.5)
    w_kc, w_vc, w_ks, w_vs, w_kw, w_vw = (w_in[:, aq + i * akv: aq + (i + 1) * akv] for i in range(6))
    w_gl = w_in[:, aq + 6 * akv:]
    w_big = jnp.concatenate([_pad_heads(wq, A_HEADS), _pad_heads(w_ks, g_), _pad_heads(w_vs, g_),
                             _pad_heads(w_kw, g_), _pad_heads(w_vw, g_)], axis=1).astype(BF16)
    slab_modes = ["plain"] * A_HEADS + ["rope_onehot"] * g_ + ["ones"] * g_ + ["rope"] * g_ + ["ones"] * g_
    proj = _proj(hb, w_big, tabs, slab_modes)

    w_gl_g = w_gl.reshape(-1, 3, g_, A_GROUP).transpose(0, 2, 1, 3).reshape(-1, g_, 3 * A_GROUP)
    w_gl_g = jnp.pad(w_gl_g, ((0, 0), (0, 0), (0, LANES - 3 * A_GROUP))).reshape(-1, g_ * LANES)
    w_side = jnp.concatenate([w_gl_g, w_kc, w_vc], axis=1).astype(BF16)
    gates, kc_in, vc_in = _nsa_side(hb, w_side, g_ * LANES, akv)
    kc = _compress_branch(kc_in, cmp_pos[0], cmp_w1[0], cmp_w2[0], bsz, seq)
    vc = _compress_branch(vc_in, cmp_pos[1], cmp_w1[1], cmp_w2[1], bsz, seq)
    pad = ((0, 0), (0, 0), (0, 0), (0, LANES - HEAD_DIM))
    kc = jnp.pad(kc * LOG2E, pad).astype(BF16)
    vc = jnp.pad(vc, pad).astype(BF16)
    ncp = seq // CMP_STRIDE
    cmp_map_t = _cmp_to_slc(ncp, seq // SLC_BLOCK).T

    o_c, selbias = _cmp_select(proj, kc, vc, cmp_map_t, bsz, seq)
    o = _sel_win(proj, tabs, selbias, o_c, gates, bsz, seq)
    return o, selbias, o_c


def _shared_kv(hb, tabs, w_kv):
    g_ = B_KV_HEADS
    bkv = g_ * HEAD_DIM
    w = jnp.concatenate([_pad_heads(w_kv[:, :bkv], g_), _pad_heads(w_kv[:, bkv:], g_)], axis=1).astype(BF16)
    return _proj(hb.astype(BF16), w, tabs, ["rope"] * g_ + ["ones"] * g_)


def _swa_layer(h, hb, tabs, kvproj, w_q, w_out, sinks, ln_g, ln_b, bsz, seq):
    o = _swa_attn(hb, tabs, kvproj, w_q, sinks, bsz, seq)
    return _out_ln(o, w_out.astype(BF16), h, ln_g[None, :], ln_b[None, :])


def _swa_attn(hb, tabs, kvproj, w_q, sinks, bsz, seq):
    wq = _pad_heads(w_q * (HEAD_DIM ** -0.5), B_HEADS).astype(BF16)
    qproj = _proj(hb.astype(BF16), wq, tabs, ["rope"] * B_HEADS)
    sk = jnp.broadcast_to(sinks.astype(F32).reshape(B_KV_HEADS, B_GROUP, 1), (B_KV_HEADS, B_GROUP, LANES))
    return _swa(qproj, kvproj, sk, bsz, seq)


def _moe_layer(h, hb, w_router, w_in_all, w_out_all, layer, ln_g, ln_b):
    y1, y2, route = _moe_experts(h, hb, w_router, w_in_all, w_out_all, layer)
    return _combine_ln(h, y1, y2, route, ln_g[None, :], ln_b[None, :])


def _moe_ffn_out(h, w_router, w_in_all, w_out_all, layer):
    y1, y2, route = _moe_experts(h, h.astype(BF16), w_router, w_in_all, w_out_all, layer)
    return route[:, 2:3] * y1 + route[:, 3:4] * y2


def _moe_experts(h, hb, w_router, w_in_all, w_out_all, layer):
    n_tok, d = h.shape
    wr = jnp.pad(w_router, ((0, 0), (0, LANES - N_EXPERTS)))
    wr_hi = wr.astype(BF16)
    wr_lo = (wr - wr_hi.astype(F32)).astype(BF16)
    route = _router(h, wr_hi, wr_lo)
    top_e = route[:, :2].astype(jnp.int32)

    e_flat = top_e.reshape(-1)
    onehot = (e_flat[:, None] == jnp.arange(N_EXPERTS)[None, :]).astype(jnp.int32)
    csum = jnp.cumsum(onehot, axis=0)
    counts = csum[-1]
    rank = jnp.take_along_axis(csum, e_flat[:, None], axis=1)[:, 0] - 1
    padded = (counts + MOE_TM - 1) // MOE_TM * MOE_TM
    pad_end = jnp.cumsum(padded)
    pad_start = pad_end - padded
    dest = pad_start[e_flat] + rank
    cap = n_tok * 2 + N_EXPERTS * MOE_TM
    n_blk = cap // MOE_TM
    tok_flat = jnp.repeat(jnp.arange(n_tok, dtype=jnp.int32), 2)
    buf_tok = jnp.zeros((cap,), jnp.int32).at[dest].set(tok_flat)
    blk_start = jnp.arange(n_blk, dtype=jnp.int32) * MOE_TM
    blk_e = jnp.sum((pad_end[None, :] <= blk_start[:, None]).astype(jnp.int32), axis=1)
    blk_e = jnp.minimum(blk_e, N_EXPERTS - 1).astype(jnp.int32)

    xs = hb[buf_tok]
    n_act = (pad_end[-1:] // MOE_TM).astype(jnp.int32)
    w_in = _cast_layer_bf16(w_in_all, layer, tc=D_FF_EXPERT // 2)
    w_out = _cast_layer_bf16(w_out_all, layer, tc=d)
    y = _moe_ffn(xs, blk_e, n_act, w_in, w_out)
    dest2 = dest.reshape(n_tok, 2)
    y1 = y[dest2[:, 0]]
    y2 = y[dest2[:, 1]]
    return y1, y2, route


def kernel(x, positions, w_in_a, w_out_a, cmp_pos, cmp_w1, cmp_w2, w_kv_shared, w_q_b, w_out_b, sinks_b,
           ln_g, ln_b, dense_w_in, dense_w_out, moe_router, moe_w_in, moe_w_out):
    bsz, seq, d = x.shape
    n_a = DEPTH // 2
    tabs = _rope_tables(positions)
    h = x.reshape(bsz * seq, d)
    hb = h.astype(BF16)
    kvproj = None
    for l in range(DEPTH):
        if l < n_a:
            h, hb = _nsa_layer(h, hb, tabs, w_in_a[l], w_out_a[l], cmp_pos[l], cmp_w1[l], cmp_w2[l],
                               ln_g[l, 0], ln_b[l, 0], bsz, seq)
        else:
            b = l - n_a
            h, hb = _swa_layer(h, hb, tabs, kvproj, w_q_b[b], w_out_b[b], sinks_b[b], ln_g[l, 0], ln_b[l, 0],
                               bsz, seq)
        if l % 2 == 0:
            h, hb = _ffn_dense(h, dense_w_in[l // 2].astype(BF16), dense_w_out[l // 2].astype(BF16),
                               ln_g[l, 1][None, :], ln_b[l, 1][None, :])
        else:
            h, hb = _moe_layer(h, hb, moe_router[l // 2], moe_w_in, moe_w_out, l // 2,
                               ln_g[l, 1], ln_b[l, 1])
        if l == n_a - 1:
            kvproj = _shared_kv(hb, tabs, w_kv_shared)
    return h.reshape(bsz, seq, d)
```

```python
import functools

import numpy as np
import jax
import jax.numpy as jnp
from jax import lax
from jax.experimental import pallas as pl
from jax.experimental.pallas import tpu as pltpu

F32 = jnp.float32
BF16 = jnp.bfloat16

D_MODEL = 1024
DEPTH = 4
HEAD_DIM = 64
LANES = 128
ROT_DIM = HEAD_DIM // 4
ROPE_THETA = 500000.0
A_HEADS = 16
A_KV_HEADS = 4
A_GROUP = 4
CMP_BLOCK = 32
CMP_STRIDE = 16
CMP_HID = 256
SLC_BLOCK = 64
N_SEL = 16
N_FORCED = 3
NSA_WIN = 512
B_HEADS = 16
B_KV_HEADS = 2
B_GROUP = 8
SWA_WIN = 128
D_FF = 2816
N_EXPERTS = 8
D_FF_EXPERT = 3584
ALPHA = (2 * DEPTH) ** 0.25
LN_EPS = 1e-5
NEG = -1e30
FORCE = 1e9

TQB = 256
TQA = 256
CMP_CHUNK = 256
LOG2E = 1.4426950408889634
KT = 512
BLK_PER_KT = KT // SLC_BLOCK
KT_GROUP = 4
MOE_TM = 512
VMEM_LIMIT = 56 * 1024 * 1024


def _cp(*sem):
    return pltpu.CompilerParams(dimension_semantics=sem, vmem_limit_bytes=VMEM_LIMIT)


def _dot(a, b):
    return jnp.dot(a, b, preferred_element_type=F32)


def _dot_nt(a, b):
    return lax.dot_general(a, b, (((1,), (1,)), ((), ())), preferred_element_type=F32)


def _sigmoid(x):
    return 1.0 / (1.0 + jnp.exp(-x))


def _layer_norm(z, g, b):
    mu = jnp.mean(z, axis=-1, keepdims=True)
    zc = z - mu
    var = jnp.mean(zc * zc, axis=-1, keepdims=True)
    return zc * lax.rsqrt(var + LN_EPS) * g + b


def _rope(x, c, s):
    half = ROT_DIM // 2
    ns = x.shape[1] // LANES
    rolled = [pltpu.roll(x[:, i * LANES:(i + 1) * LANES], half, 1) for i in range(ns)]
    z = rolled[0] if ns == 1 else jnp.concatenate(rolled, axis=1)
    return x * c + z * s


def _tile_lanes(t, n):
    return t if n == 1 else jnp.concatenate([t] * n, axis=1)


def _tile_rows(t, n):
    return t if n == 1 else jnp.concatenate([t] * n, axis=0)


def _normalize(acc, extra=0.0):
    den = pltpu.roll(acc, HEAD_DIM, 1) + extra
    return (acc / den)[:, :HEAD_DIM]


def _heads_to_lanes(o, heads):
    t = o.shape[0] // heads
    return jnp.concatenate([o[r * t:(r + 1) * t] for r in range(heads)], axis=1)


def _mm_kernel(x_ref, w_ref, o_ref):
    o_ref[...] = _dot(x_ref[...].astype(BF16), w_ref[...]).astype(o_ref.dtype)


def _mm(x, w, out_dtype, tm=512, tn=None):
    m, k = x.shape
    n = w.shape[1]
    tm = min(tm, m)
    tn = n if tn is None else tn
    return pl.pallas_call(
        _mm_kernel,
        grid=(m // tm, n // tn),
        in_specs=[pl.BlockSpec((tm, k), lambda i, j: (i, 0)), pl.BlockSpec((k, tn), lambda i, j: (0, j))],
        out_specs=pl.BlockSpec((tm, tn), lambda i, j: (i, j)),
        out_shape=jax.ShapeDtypeStruct((m, n), out_dtype),
        compiler_params=_cp("parallel", "arbitrary"),
        name="mm",
    )(x, w)


def _nsa_side_kernel(x_ref, w_ref, gate_ref, kc_ref, vc_ref):
    acc = _dot(x_ref[...], w_ref[...])
    ng = gate_ref.shape[1]
    nk = kc_ref.shape[1]
    gate_ref[...] = _sigmoid(acc[:, :ng])
    kc_ref[...] = acc[:, ng:ng + nk]
    vc_ref[...] = acc[:, ng + nk:]


def _nsa_side(hb, w, n_gate, n_kv, tm=512):
    m, k = hb.shape
    row = lambda n: pl.BlockSpec((tm, n), lambda i: (i, 0))
    return pl.pallas_call(
        _nsa_side_kernel,
        grid=(m // tm,),
        in_specs=[row(k), pl.BlockSpec(w.shape, lambda i: (0, 0))],
        out_specs=[row(n_gate), row(n_kv), row(n_kv)],
        out_shape=[jax.ShapeDtypeStruct((m, n_gate), F32), jax.ShapeDtypeStruct((m, n_kv), F32),
                   jax.ShapeDtypeStruct((m, n_kv), F32)],
        compiler_params=_cp("parallel"),
        name="nsa_side",
    )(hb, w)


def _cast_kernel(x_ref, o_ref):
    o_ref[...] = x_ref[...].astype(o_ref.dtype)


def _cast_layer_bf16(w, layer, tr):
    _, e, r, c = w.shape
    return pl.pallas_call(
        _cast_kernel,
        grid=(e, r // tr),
        in_specs=[pl.BlockSpec((None, None, tr, c), lambda ei, i: (layer, ei, i, 0))],
        out_specs=pl.BlockSpec((None, tr, c), lambda ei, i: (ei, i, 0)),
        out_shape=jax.ShapeDtypeStruct((e, r, c), BF16),
        compiler_params=_cp("parallel", "arbitrary"),
        name="cast_bf16",
    )(w)


def _proj_kernel(x_ref, w_ref, c_ref, s_ref, o_ref, *, block_modes, tm, tn):
    i = pl.program_id(0)
    j = pl.program_id(1)
    acc = _dot(x_ref[...], w_ref[...])

    def piece(mode, s0, ns):
        a = acc[:, s0 * LANES:(s0 + ns) * LANES]
        lane = lax.broadcasted_iota(jnp.int32, a.shape, 1) % LANES
        if mode in ("rope", "rope_onehot"):
            a = _rope(a, _tile_lanes(c_ref[...], ns), _tile_lanes(s_ref[...], ns))
        if mode == "rope_onehot":
            row = i * tm + lax.broadcasted_iota(jnp.int32, a.shape, 0)
            a = a + jnp.where(lane == HEAD_DIM + (row // SLC_BLOCK) % BLK_PER_KT, 1.0, 0.0)
        if mode == "ones":
            a = a + jnp.where(lane >= HEAD_DIM, 1.0, 0.0)
        return a.astype(o_ref.dtype)

    for modes in sorted(set(block_modes)):
        pred = functools.reduce(jnp.logical_or, [j == jj for jj, mm in enumerate(block_modes) if mm == modes])

        @pl.when(pred)
        def _(modes=modes):
            runs = []
            for s, mode in enumerate(modes):
                if runs and runs[-1][0] == mode:
                    runs[-1][2] += 1
                else:
                    runs.append([mode, s, 1])
            pieces = [piece(*r) for r in runs]
            o_ref[...] = pieces[0] if len(pieces) == 1 else jnp.concatenate(pieces, axis=1)


def _proj(x, w, tabs, slab_modes, tn=1024, tm=1024):
    m, k = x.shape
    n = w.shape[1]
    tn = min(tn, n)
    spb = tn // LANES
    block_modes = tuple(tuple(slab_modes[jj * spb:(jj + 1) * spb]) for jj in range(n // tn))
    c, s = tabs
    return pl.pallas_call(
        functools.partial(_proj_kernel, block_modes=block_modes, tm=tm, tn=tn),
        grid=(m // tm, n // tn),
        in_specs=[
            pl.BlockSpec((tm, k), lambda i, j: (i, 0)),
            pl.BlockSpec((k, tn), lambda i, j: (0, j)),
            pl.BlockSpec((tm, LANES), lambda i, j: (i, 0)),
            pl.BlockSpec((tm, LANES), lambda i, j: (i, 0)),
        ],
        out_specs=pl.BlockSpec((tm, tn), lambda i, j: (i, j)),
        out_shape=jax.ShapeDtypeStruct((m, n), BF16),
        compiler_params=_cp("parallel", "arbitrary"),
        name="proj",
    )(x, w, c, s)


def _out_ln_kernel(x_ref, w_ref, h_ref, g_ref, b_ref, o_ref, ob_ref):
    mix = _dot(x_ref[...], w_ref[...])
    y = _layer_norm(ALPHA * h_ref[...] + mix, g_ref[...], b_ref[...])
    o_ref[...] = y
    ob_ref[...] = y.astype(BF16)


def _dual_out(m, d, tm):
    spec = pl.BlockSpec((tm, d), lambda i, *_: (i, 0))
    return [spec, spec], [jax.ShapeDtypeStruct((m, d), F32), jax.ShapeDtypeStruct((m, d), BF16)]


def _out_ln(x, w, h, g, b, tm=512):
    m, k = x.shape
    d = w.shape[1]
    out_specs, out_shape = _dual_out(m, d, tm)
    return pl.pallas_call(
        _out_ln_kernel,
        grid=(m // tm,),
        in_specs=[
            pl.BlockSpec((tm, k), lambda i: (i, 0)),
            pl.BlockSpec((k, d), lambda i: (0, 0)),
            pl.BlockSpec((tm, d), lambda i: (i, 0)),
            pl.BlockSpec((1, d), lambda i: (0, 0)),
            pl.BlockSpec((1, d), lambda i: (0, 0)),
        ],
        out_specs=out_specs,
        out_shape=out_shape,
        compiler_params=_cp("parallel"),
        name="out_ln",
    )(x, w, h, g, b)


def _ffn_kernel(x_ref, wg_ref, wu_ref, wo_ref, g_ref, b_ref, o_ref, ob_ref, acc_ref):
    c = pl.program_id(1)

    @pl.when(c == 0)
    def _():
        acc_ref[...] = jnp.zeros_like(acc_ref)

    xb = x_ref[...].astype(BF16)
    gate = _dot(xb, wg_ref[...])
    up = _dot(xb, wu_ref[...])
    a = gate * _sigmoid(gate) * up
    acc_ref[...] += _dot(a.astype(BF16), wo_ref[...])

    @pl.when(c == pl.num_programs(1) - 1)
    def _():
        y = _layer_norm(ALPHA * x_ref[...] + acc_ref[...], g_ref[...], b_ref[...])
        o_ref[...] = y
        ob_ref[...] = y.astype(BF16)


def _ffn_dense(h, w_in, w_out, g, b, tm=512):
    m, d = h.shape
    ff = w_out.shape[0]
    fc, nc = ff, 1
    once = pl.Buffered(1)
    out_specs, out_shape = _dual_out(m, d, tm)
    return pl.pallas_call(
        _ffn_kernel,
        grid=(m // tm, nc),
        in_specs=[
            pl.BlockSpec((tm, d), lambda i, c: (i, 0)),
            pl.BlockSpec((d, fc), lambda i, c: (0, c), pipeline_mode=once),
            pl.BlockSpec((d, fc), lambda i, c: (0, nc + c), pipeline_mode=once),
            pl.BlockSpec((fc, d), lambda i, c: (c, 0), pipeline_mode=once),
            pl.BlockSpec((1, d), lambda i, c: (0, 0)),
            pl.BlockSpec((1, d), lambda i, c: (0, 0)),
        ],
        out_specs=out_specs,
        out_shape=out_shape,
        scratch_shapes=[pltpu.VMEM((tm, d), F32)],
        compiler_params=_cp("parallel", "arbitrary"),
        name="ffn_dense",
    )(h, w_in, w_in, w_out, g, b)


def _compress_kernel(p_ref, q_ref, pos_ref, w1_ref, w2_ref, o_ref):
    posb = _dot(pos_ref[...], w1_ref[...])[0:1, :]
    hid = p_ref[...] + q_ref[...] + posb
    act = jax.nn.gelu(hid, approximate=True)
    o_ref[...] = _dot(act.astype(BF16), w2_ref[...])


def _compress(pp, qq, pos8, w1, w2, tm=1024):
    m, hid = pp.shape
    dh = w2.shape[1]
    return pl.pallas_call(
        _compress_kernel,
        grid=(m // tm,),
        in_specs=[
            pl.BlockSpec((tm, hid), lambda i: (i, 0)),
            pl.BlockSpec((tm, hid), lambda i: (i, 0)),
            pl.BlockSpec(pos8.shape, lambda i: (0, 0)),
            pl.BlockSpec(w1.shape, lambda i: (0, 0)),
            pl.BlockSpec(w2.shape, lambda i: (0, 0)),
        ],
        out_specs=pl.BlockSpec((tm, dh), lambda i: (i, 0)),
        out_shape=jax.ShapeDtypeStruct((m, dh), F32),
        compiler_params=_cp("parallel"),
        name="compress",
    )(pp, qq, pos8, w1, w2)


def _cmp_select_kernel(q_ref, kc_ref, vc_ref, mapt_ref, oc_ref, sb_ref, *, ncp, nslc):
    qi = pl.program_id(2)
    t0 = qi * TQA
    rows = A_GROUP * TQA
    qb = q_ref[...]
    q4 = jnp.concatenate([qb[:, r * LANES:(r + 1) * LANES] for r in range(A_GROUP)], axis=0)

    def branch(nb):
        nk = (nb + 1) * CMP_CHUNK
        nr = nk * CMP_STRIDE // SLC_BLOCK
        s = _dot_nt(q4, kc_ref[0:nk, :])
        t_tok = t0 + lax.broadcasted_iota(jnp.int32, (TQA, nk), 0)
        n_idx = lax.broadcasted_iota(jnp.int32, (TQA, nk), 1)
        valid = (n_idx * CMP_STRIDE + CMP_BLOCK - 1 <= t_tok) & (n_idx < ncp - 1)
        s3 = s.reshape(A_GROUP, TQA, nk) + jnp.where(valid, 0.0, NEG)[None]
        m = jnp.max(s3, axis=2, keepdims=True)
        p = jnp.exp2(s3 - m)
        l = jnp.sum(p, axis=2, keepdims=True)
        pn = p * jnp.where(m > 0.5 * NEG, 1.0 / l, 0.0)
        oc = _dot(pn.reshape(rows, nk).astype(BF16), vc_ref[0:nk, :])
        oc_ref[...] = oc[:, :HEAD_DIM].reshape(A_GROUP, TQA, HEAD_DIM)
        psum = pn[0] + pn[1] + pn[2] + pn[3]
        p_hi = psum.astype(BF16)
        p_lo = (psum - p_hi.astype(F32)).astype(BF16)
        mapt = mapt_ref[0:nr, 0:nk]
        imp = _dot_nt(mapt, p_hi) + _dot_nt(mapt, p_lo)

        j_idx = lax.broadcasted_iota(jnp.int32, (nr, TQA), 0)
        cur = (t0 + lax.broadcasted_iota(jnp.int32, (nr, TQA), 1)) // SLC_BLOCK
        forced = (j_idx == 0) | (j_idx == cur) | (j_idx == cur - 1)
        score = jnp.where(j_idx > cur, -1.0, jnp.where(forced, -jnp.inf, imp))
        for _ in range(N_SEL - N_FORCED):
            mx = jnp.max(score, axis=0, keepdims=True)
            first = jnp.min(jnp.where(score == mx, j_idx, nr), axis=0, keepdims=True)
            score = jnp.where(j_idx == first, -jnp.inf, score)
        sb_t = jnp.where((score == -jnp.inf) & (j_idx <= cur), 0.0, NEG)
        if nr < nslc:
            sb_t = jnp.concatenate([sb_t, jnp.full((nslc - nr, TQA), NEG, F32)], axis=0)
        sb_ref[...] = sb_t.T

    bucket = ((qi + 1) * (TQA // CMP_STRIDE) - 1) // CMP_CHUNK
    for nb in range(ncp // CMP_CHUNK):
        pl.when(bucket == nb)(functools.partial(branch, nb))


def _cmp_select(proj, kc_cmp, vc_cmp, cmp_map_t, bsz, seq):
    ncp = kc_cmp.shape[2]
    nslc = seq // SLC_BLOCK
    nq = seq // TQA
    g_ = A_KV_HEADS
    assert nslc >= N_SEL and ncp % CMP_CHUNK == 0
    return pl.pallas_call(
        functools.partial(_cmp_select_kernel, ncp=ncp, nslc=nslc),
        grid=(bsz, g_, nq),
        in_specs=[
            pl.BlockSpec((TQA, A_GROUP * LANES), lambda b, g, i: (b * nq + i, g)),
            pl.BlockSpec((None, None, ncp, LANES), lambda b, g, i: (b, g, 0, 0)),
            pl.BlockSpec((None, None, ncp, LANES), lambda b, g, i: (b, g, 0, 0)),
            pl.BlockSpec((nslc, ncp), lambda b, g, i: (0, 0)),
        ],
        out_specs=[
            pl.BlockSpec((None, None, A_GROUP, TQA, HEAD_DIM), lambda b, g, i: (b, g, 0, i, 0)),
            pl.BlockSpec((TQA, nslc), lambda b, g, i: (b * nq + i, g)),
        ],
        out_shape=[
            jax.ShapeDtypeStruct((bsz, g_, A_GROUP, seq, HEAD_DIM), F32),
            jax.ShapeDtypeStruct((bsz * seq, g_ * nslc), F32),
        ],
        compiler_params=_cp("parallel", "parallel", "arbitrary"),
        name="cmp_select",
    )(proj, kc_cmp, vc_cmp, cmp_map_t)


def _sel_win_kernel(q_ref, c_ref, s_ref, bias_ref, ks_ref, vs_ref, kw_ref, vw_ref, oc_ref, gate_ref,
                    o_ref, acc_ref, *, nslc):
    qi = pl.program_id(2)
    t0 = qi * TQA
    rows = A_GROUP * TQA
    qb = q_ref[...]
    q4 = jnp.concatenate([qb[:, r * LANES:(r + 1) * LANES] for r in range(A_GROUP)], axis=0).astype(F32)
    q_rot = _rope(q4, _tile_rows(c_ref[...], A_GROUP), _tile_rows(s_ref[...], A_GROUP)) * LOG2E
    q_rot_b = q_rot.astype(BF16)
    bias = bias_ref[...]
    lane = lax.broadcasted_iota(jnp.int32, (TQA, LANES), 1)
    bias_lanes = (lane >= HEAD_DIM) & (lane < HEAD_DIM + BLK_PER_KT)
    t_q = t0 + lax.broadcasted_iota(jnp.int32, (TQA, KT), 0)
    k_off = lax.broadcasted_iota(jnp.int32, (TQA, KT), 1)

    def add_mask(s, mask_bias):
        return (s.reshape(A_GROUP, TQA, -1) + mask_bias[None]).reshape(s.shape)

    def scores(kt):
        shift = (HEAD_DIM + nslc - BLK_PER_KT * kt) % nslc
        rolled = pltpu.roll(bias, shift, 1)[:, :LANES]
        qa = (q_rot + _tile_rows(jnp.where(bias_lanes, rolled, 0.0), A_GROUP)).astype(BF16)
        k0 = pl.multiple_of(kt * KT, KT)
        return _dot_nt(qa, ks_ref[pl.ds(k0, KT), :])

    def causal_bias(kt):
        return jnp.where(kt * KT + k_off <= t_q, 0.0, NEG)

    def group(kt0, n, carry, diagonal_last):
        m, acc = carry
        ss = [scores(kt0 + i) for i in range(n)]
        if diagonal_last:
            ss[-1] = add_mask(ss[-1], causal_bias(kt0 + n - 1))
        m_row = functools.reduce(jnp.maximum, [jnp.max(s, axis=1, keepdims=True) for s in ss])
        m_new = jnp.maximum(m, jnp.broadcast_to(m_row, (rows, LANES)))
        m_keys = _tile_lanes(m_new, KT // LANES)
        pv = None
        for i, s in enumerate(ss):
            k0 = pl.multiple_of((kt0 + i) * KT, KT)
            d = _dot(jnp.exp2(s - m_keys).astype(BF16), vs_ref[pl.ds(k0, KT), :])
            pv = d if pv is None else pv + d
        return m_new, jnp.exp2(m - m_new) * acc + pv

    kt_d = t0 // KT
    n_quads = kt_d // KT_GROUP
    carry = (jnp.full((rows, LANES), NEG, F32), jnp.zeros((rows, LANES), F32))
    carry = lax.fori_loop(0, n_quads, lambda j, c: group(j * KT_GROUP, KT_GROUP, c, False), carry)
    for rem in range(KT_GROUP):
        @pl.when(kt_d - n_quads * KT_GROUP == rem)
        def _(rem=rem):
            acc_ref[...] = group(kt_d - rem, rem + 1, carry, True)[1]
    o_s = _normalize(acc_ref[...])

    wspan = NSA_WIN + TQA
    w0 = pl.multiple_of(jnp.maximum(t0 - NSA_WIN, 0), TQA)
    s_w = _dot_nt(q_rot_b, kw_ref[pl.ds(w0, wspan), :])
    tw = t0 + lax.broadcasted_iota(jnp.int32, (TQA, wspan), 0)
    diff = tw - (w0 + lax.broadcasted_iota(jnp.int32, (TQA, wspan), 1))
    s_w = add_mask(s_w, jnp.where((diff >= 0) & (diff < NSA_WIN), 0.0, NEG))
    p_w = jnp.exp2(s_w - jnp.max(s_w, axis=1, keepdims=True))
    acc_w = _dot(p_w.astype(BF16), vw_ref[pl.ds(w0, wspan), :])
    o_w = _normalize(acc_w)

    gates = gate_ref[...]

    def gcol(c):
        return jnp.concatenate([gates[:, c * A_GROUP + r:c * A_GROUP + r + 1] for r in range(A_GROUP)], axis=0)

    o_c = oc_ref[...].reshape(rows, HEAD_DIM)
    o = gcol(0) * o_c + gcol(1) * o_s + gcol(2) * o_w
    o_ref[...] = _heads_to_lanes(o, A_GROUP).astype(o_ref.dtype)


def _sel_win(proj, tabs, selbias, o_c, gates, bsz, seq):
    nslc = seq // SLC_BLOCK
    nq = seq // TQA
    g_ = A_KV_HEADS
    c, s = tabs
    tab_spec = pl.BlockSpec((TQA, LANES), lambda b, g, i: (b * nq + i, 0))
    q_slabs = A_HEADS

    def kv_spec(base):
        return pl.BlockSpec((seq, LANES), lambda b, g, i: (b, base + g))

    return pl.pallas_call(
        functools.partial(_sel_win_kernel, nslc=nslc),
        grid=(bsz, g_, nq),
        in_specs=[
            pl.BlockSpec((TQA, A_GROUP * LANES), lambda b, g, i: (b * nq + i, g)),
            tab_spec, tab_spec,
            pl.BlockSpec((TQA, nslc), lambda b, g, i: (b * nq + i, g)),
            kv_spec(q_slabs), kv_spec(q_slabs + g_), kv_spec(q_slabs + 2 * g_), kv_spec(q_slabs + 3 * g_),
            pl.BlockSpec((None, None, A_GROUP, TQA, HEAD_DIM), lambda b, g, i: (b, g, 0, i, 0)),
            pl.BlockSpec((TQA, LANES), lambda b, g, i: (b * nq + i, g)),
        ],
        out_specs=pl.BlockSpec((TQA, A_GROUP * HEAD_DIM), lambda b, g, i: (b * nq + i, g)),
        out_shape=jax.ShapeDtypeStruct((bsz * seq, A_HEADS * HEAD_DIM), BF16),
        scratch_shapes=[pltpu.VMEM((A_GROUP * TQA, LANES), F32)],
        compiler_params=_cp("parallel", "parallel", "arbitrary"),
        name="sel_win",
    )(proj, c, s, selbias, proj, proj, proj, proj, o_c, gates)


def _swa_kernel(q_ref, k_ref, v_ref, sink_ref, o_ref):
    qi = pl.program_id(2)
    t0 = qi * TQB
    rows = B_GROUP * TQB
    span = SWA_WIN + TQB
    qb = q_ref[...]
    q8 = jnp.concatenate([qb[:, r * LANES:(r + 1) * LANES] for r in range(B_GROUP)], axis=0)
    w0 = pl.multiple_of(jnp.maximum(t0 - SWA_WIN, 0), SWA_WIN)
    s = _dot_nt(q8, k_ref[pl.ds(w0, span), :])
    tq = t0 + lax.broadcasted_iota(jnp.int32, (TQB, span), 0)
    diff = tq - (w0 + lax.broadcasted_iota(jnp.int32, (TQB, span), 1))
    band = jnp.where((diff >= 0) & (diff < SWA_WIN), 0.0, NEG)
    s = (s.reshape(B_GROUP, TQB, span) + band[None]).reshape(rows, span)
    sk = sink_ref[...]
    sink = jnp.concatenate([jnp.broadcast_to(sk[r:r + 1, :], (TQB, LANES)) for r in range(B_GROUP)], axis=0)
    m = jnp.maximum(jnp.broadcast_to(jnp.max(s, axis=1, keepdims=True), (rows, LANES)), sink)
    e = jnp.exp(s - _tile_lanes(m, span // LANES))
    acc = _dot(e.astype(BF16), v_ref[pl.ds(w0, span), :])
    o = _normalize(acc, jnp.exp(sink - m))
    o_ref[...] = _heads_to_lanes(o, B_GROUP).astype(o_ref.dtype)


def _swa(qproj, kvproj, sinks, bsz, seq):
    nq = seq // TQB
    g_ = B_KV_HEADS
    return pl.pallas_call(
        _swa_kernel,
        grid=(bsz, g_, nq),
        in_specs=[
            pl.BlockSpec((TQB, B_GROUP * LANES), lambda b, g, i: (b * nq + i, g)),
            pl.BlockSpec((seq, LANES), lambda b, g, i: (b, g)),
            pl.BlockSpec((seq, LANES), lambda b, g, i: (b, g_ + g)),
            pl.BlockSpec((None, B_GROUP, LANES), lambda b, g, i: (g, 0, 0)),
        ],
        out_specs=pl.BlockSpec((TQB, B_GROUP * HEAD_DIM), lambda b, g, i: (b * nq + i, g)),
        out_shape=jax.ShapeDtypeStruct((bsz * seq, B_HEADS * HEAD_DIM), BF16),
        compiler_params=_cp("parallel", "parallel", "arbitrary"),
        name="swa",
    )(qproj, kvproj, kvproj, sinks)


def _router_kernel(x_ref, wh_ref, wl_ref, o_ref):
    x = x_ref[...]
    xh = x.astype(BF16)
    xl = (x - xh.astype(F32)).astype(BF16)
    logits = _dot(xh, wh_ref[...]) + _dot(xh, wl_ref[...]) + _dot(xl, wh_ref[...])
    lane = lax.broadcasted_iota(jnp.int32, logits.shape, 1)
    lg = jnp.where(lane < N_EXPERTS, logits, -jnp.inf)
    m1 = jnp.max(lg, axis=1, keepdims=True)
    i1 = jnp.min(jnp.where(lg == m1, lane, LANES), axis=1, keepdims=True)
    lg2 = jnp.where(lane == i1, -jnp.inf, lg)
    m2 = jnp.max(lg2, axis=1, keepdims=True)
    i2 = jnp.min(jnp.where(lg2 == m2, lane, LANES), axis=1, keepdims=True)
    e2 = jnp.exp(m2 - m1)
    g1 = 1.0 / (1.0 + e2)
    g2 = e2 * g1
    out = jnp.where(lane == 0, i1.astype(F32), jnp.where(lane == 1, i2.astype(F32),
                    jnp.where(lane == 2, g1, jnp.where(lane == 3, g2, 0.0))))
    o_ref[...] = out


def _router(h, w_hi, w_lo, tm=512):
    m, d = h.shape
    return pl.pallas_call(
        _router_kernel,
        grid=(m // tm,),
        in_specs=[pl.BlockSpec((tm, d), lambda i: (i, 0)), pl.BlockSpec((d, LANES), lambda i: (0, 0)),
                  pl.BlockSpec((d, LANES), lambda i: (0, 0))],
        out_specs=pl.BlockSpec((tm, LANES), lambda i: (i, 0)),
        out_shape=jax.ShapeDtypeStruct((m, LANES), F32),
        compiler_params=_cp("parallel"),
        name="router",
    )(h, w_hi, w_lo)


def _moe_ffn_kernel(blk_e_ref, n_act_ref, x_ref, wg_ref, wu_ref, wo_ref, o_ref, acc_ref):
    i = pl.program_id(0)
    c = pl.program_id(1)

    @pl.when(c == 0)
    def _():
        acc_ref[...] = jnp.zeros_like(acc_ref)

    @pl.when(i < n_act_ref[0])
    def _():
        xb = x_ref[...]
        gate = _dot(xb, wg_ref[...])
        up = _dot(xb, wu_ref[...])
        a = gate * _sigmoid(gate) * up
        acc_ref[...] += _dot(a.astype(BF16), wo_ref[...])

    @pl.when(c == pl.num_programs(1) - 1)
    def _():
        o_ref[...] = acc_ref[...]


def _moe_ffn(xs, blk_e, n_act, w_in, w_out, tm=MOE_TM, fc=1792):
    cap, d = xs.shape
    ff = w_out.shape[1]
    nc = ff // fc

    def chunk(i, c, n_act_ref):
        return jnp.where(i < n_act_ref[0], c, nc - 1)

    grid_spec = pltpu.PrefetchScalarGridSpec(
        num_scalar_prefetch=2,
        grid=(cap // tm, nc),
        in_specs=[
            pl.BlockSpec((tm, d), lambda i, c, e, na: (i, 0)),
            pl.BlockSpec((None, d, fc), lambda i, c, e, na: (e[i], 0, chunk(i, c, na))),
            pl.BlockSpec((None, d, fc), lambda i, c, e, na: (e[i], 0, nc + chunk(i, c, na))),
            pl.BlockSpec((None, fc, d), lambda i, c, e, na: (e[i], chunk(i, c, na), 0)),
        ],
        out_specs=pl.BlockSpec((tm, d), lambda i, c, e, na: (i, 0)),
        scratch_shapes=[pltpu.VMEM((tm, d), F32)],
    )
    return pl.pallas_call(
        _moe_ffn_kernel,
        grid_spec=grid_spec,
        out_shape=jax.ShapeDtypeStruct((cap, d), F32),
        compiler_params=_cp("parallel", "arbitrary"),
        name="moe_ffn",
    )(blk_e, n_act, xs, w_in, w_in, w_out)


def _combine_ln_kernel(h_ref, y1_ref, y2_ref, r_ref, g_ref, b_ref, o_ref, ob_ref):
    r = r_ref[...]
    ffn = r[:, 2:3] * y1_ref[...] + r[:, 3:4] * y2_ref[...]
    y = _layer_norm(ALPHA * h_ref[...] + ffn, g_ref[...], b_ref[...])
    o_ref[...] = y
    ob_ref[...] = y.astype(BF16)


def _combine_ln(h, y1, y2, route, g, b, tm=512):
    m, d = h.shape
    row = pl.BlockSpec((tm, d), lambda i: (i, 0))
    vec = pl.BlockSpec((1, d), lambda i: (0, 0))
    out_specs, out_shape = _dual_out(m, d, tm)
    return pl.pallas_call(
        _combine_ln_kernel,
        grid=(m // tm,),
        in_specs=[row, row, row, pl.BlockSpec((tm, LANES), lambda i: (i, 0)), vec, vec],
        out_specs=out_specs,
        out_shape=out_shape,
        compiler_params=_cp("parallel"),
        name="combine_ln",
    )(h, y1, y2, route, g, b)


def _pad_heads(w, heads, rope_copy=False):
    d = w.shape[0]
    half = ROT_DIM // 2
    w = w.reshape(d, heads, HEAD_DIM)
    if rope_copy:
        fill = jnp.zeros((d, heads, LANES - HEAD_DIM - half), w.dtype)
        w = jnp.concatenate([w, fill, w[:, :, half:ROT_DIM]], axis=2)
    else:
        w = jnp.pad(w, ((0, 0), (0, 0), (0, LANES - HEAD_DIM)))
    return w.reshape(d, heads * LANES)


def _rope_tables(positions):
    half = ROT_DIM // 2
    inv = ROPE_THETA ** (-jnp.arange(0, ROT_DIM, 2, dtype=F32) / ROT_DIM)
    ang = positions.astype(F32).reshape(-1, 1) * inv
    cos, sin = jnp.cos(ang), jnp.sin(ang)
    n = ang.shape[0]
    c = jnp.concatenate([cos, cos, jnp.ones((n, LANES - ROT_DIM - half), F32), jnp.zeros((n, half), F32)], axis=1)
    s = jnp.concatenate([-sin, sin, jnp.zeros((n, LANES - ROT_DIM), F32)], axis=1)
    return c, s


def _cmp_to_slc(ncp, nslc):
    n = np.arange(ncp)[:, None]
    j = np.arange(nslc)[None, :]
    overlap = (np.minimum(n * CMP_STRIDE + CMP_BLOCK, j * SLC_BLOCK + SLC_BLOCK)
               - np.maximum(n * CMP_STRIDE, j * SLC_BLOCK))
    m = np.clip(overlap, 0, None).astype(np.float32) / CMP_BLOCK
    m[ncp - 1, :] = 0.0
    return jnp.asarray(m, BF16)


def _compress_branch(t, pos, w1, w2, bsz, seq):
    g_ = A_KV_HEADS
    nch = seq // CMP_STRIDE
    half = CMP_STRIDE * HEAD_DIM
    a = t.reshape(bsz * nch, CMP_STRIDE * g_ * HEAD_DIM)
    w1b = w1.astype(BF16)
    w1cat = jnp.concatenate([w1b[:half], w1b[half:]], axis=1)
    eye = jnp.eye(g_, dtype=BF16)
    w1blk = (w1cat.reshape(CMP_STRIDE, 1, HEAD_DIM, 1, 2 * CMP_HID) * eye[None, :, None, :, None])
    w1blk = w1blk.reshape(CMP_STRIDE * g_ * HEAD_DIM, g_ * 2 * CMP_HID)
    pq = _mm(a, w1blk, F32, tm=256, tn=2 * CMP_HID).reshape(bsz, nch, g_, 2 * CMP_HID)
    pp = pq[..., :CMP_HID].reshape(-1, CMP_HID)
    qq = jnp.concatenate([pq[:, 1:, :, CMP_HID:], jnp.zeros((bsz, 1, g_, CMP_HID), F32)], axis=1).reshape(-1, CMP_HID)
    pos8 = jnp.zeros((8, CMP_BLOCK * HEAD_DIM), BF16).at[0].set(pos.reshape(-1).astype(BF16))
    out = _compress(pp, qq, pos8, w1b, w2.astype(BF16))
    out = out.reshape(bsz, nch, g_, HEAD_DIM).transpose(0, 2, 1, 3)
    return out


def _nsa_layer(h, hb, tabs, w_in, w_out, cmp_pos, cmp_w1, cmp_w2, ln_g, ln_b, bsz, seq):
    o, _, _ = _nsa_attn(hb, tabs, w_in, cmp_pos, cmp_w1, cmp_w2, bsz, seq)
    return _out_ln(o, w_out.astype(BF16), h, ln_g[None, :], ln_b[None, :])


def _nsa_attn(hb, tabs, w_in, cmp_pos, cmp_w1, cmp_w2, bsz, seq):
    hb = hb.astype(BF16)
    g_ = A_KV_HEADS
    aq = A_HEADS * HEAD_DIM
    akv = g_ * HEAD_DIM
    wq = w_in[:, :aq] * (HEAD_DIM ** -0.5)
    w_kc, w_vc, w_ks, w_vs, w_kw, w_vw = (w_in[:, aq + i * akv: aq + (i + 1) * akv] for i in range(6))
    w_gl = w_in[:, aq + 6 * akv:]
    w_big = jnp.concatenate([_pad_heads(wq, A_HEADS, True), _pad_heads(w_ks, g_, True), _pad_heads(w_vs, g_),
                             _pad_heads(w_kw, g_, True), _pad_heads(w_vw, g_)], axis=1).astype(BF16)
    slab_modes = ["plain"] * A_HEADS + ["rope_onehot"] * g_ + ["ones"] * g_ + ["rope"] * g_ + ["ones"] * g_
    proj = _proj(hb, w_big, tabs, slab_modes)

    w_gl_g = w_gl.reshape(-1, 3, g_, A_GROUP).transpose(0, 2, 1, 3).reshape(-1, g_, 3 * A_GROUP)
    w_gl_g = jnp.pad(w_gl_g, ((0, 0), (0, 0), (0, LANES - 3 * A_GROUP))).reshape(-1, g_ * LANES)
    w_side = jnp.concatenate([w_gl_g, w_kc, w_vc], axis=1).astype(BF16)
    gates, kc_in, vc_in = _nsa_side(hb, w_side, g_ * LANES, akv)
    kc = _compress_branch(kc_in, cmp_pos[0], cmp_w1[0], cmp_w2[0], bsz, seq)
    vc = _compress_branch(vc_in, cmp_pos[1], cmp_w1[1], cmp_w2[1], bsz, seq)
    pad = ((0, 0), (0, 0), (0, 0), (0, LANES - HEAD_DIM))
    kc = jnp.pad(kc * LOG2E, pad).astype(BF16)
    vc = jnp.pad(vc, pad).astype(BF16)
    ncp = seq // CMP_STRIDE
    cmp_map_t = _cmp_to_slc(ncp, seq // SLC_BLOCK).T

    o_c, selbias = _cmp_select(proj, kc, vc, cmp_map_t, bsz, seq)
    o = _sel_win(proj, tabs, selbias, o_c, gates, bsz, seq)
    return o, selbias, o_c


def _shared_kv(hb, tabs, w_kv):
    g_ = B_KV_HEADS
    bkv = g_ * HEAD_DIM
    w = jnp.concatenate([_pad_heads(w_kv[:, :bkv], g_, True), _pad_heads(w_kv[:, bkv:], g_)], axis=1).astype(BF16)
    return _proj(hb.astype(BF16), w, tabs, ["rope"] * g_ + ["ones"] * g_)


def _swa_layer(h, hb, tabs, kvproj, w_q, w_out, sinks, ln_g, ln_b, bsz, seq):
    o = _swa_attn(hb, tabs, kvproj, w_q, sinks, bsz, seq)
    return _out_ln(o, w_out.astype(BF16), h, ln_g[None, :], ln_b[None, :])


def _swa_attn(hb, tabs, kvproj, w_q, sinks, bsz, seq):
    wq = _pad_heads(w_q * (HEAD_DIM ** -0.5), B_HEADS, True).astype(BF16)
    qproj = _proj(hb.astype(BF16), wq, tabs, ["rope"] * B_HEADS)
    sk = jnp.broadcast_to(sinks.astype(F32).reshape(B_KV_HEADS, B_GROUP, 1), (B_KV_HEADS, B_GROUP, LANES))
    return _swa(qproj, kvproj, sk, bsz, seq)


def _moe_layer(h, hb, w_router, w_in_all, w_out_all, layer, ln_g, ln_b):
    y1, y2, route = _moe_experts(h, hb, w_router, w_in_all, w_out_all, layer)
    return _combine_ln(h, y1, y2, route, ln_g[None, :], ln_b[None, :])


def _moe_ffn_out(h, w_router, w_in_all, w_out_all, layer):
    y1, y2, route = _moe_experts(h, h.astype(BF16), w_router, w_in_all, w_out_all, layer)
    return route[:, 2:3] * y1 + route[:, 3:4] * y2


def _moe_experts(h, hb, w_router, w_in_all, w_out_all, layer):
    n_tok, d = h.shape
    wr = jnp.pad(w_router, ((0, 0), (0, LANES - N_EXPERTS)))
    wr_hi = wr.astype(BF16)
    wr_lo = (wr - wr_hi.astype(F32)).astype(BF16)
    route = _router(h, wr_hi, wr_lo)
    top_e = route[:, :2].astype(jnp.int32)

    e_flat = top_e.reshape(-1)
    onehot = (e_flat[:, None] == jnp.arange(N_EXPERTS)[None, :]).astype(jnp.int32)
    csum = jnp.cumsum(onehot, axis=0)
    counts = csum[-1]
    rank = jnp.take_along_axis(csum, e_flat[:, None], axis=1)[:, 0] - 1
    padded = (counts + MOE_TM - 1) // MOE_TM * MOE_TM
    pad_end = jnp.cumsum(padded)
    pad_start = pad_end - padded
    dest = pad_start[e_flat] + rank
    cap = n_tok * 2 + N_EXPERTS * MOE_TM
    n_blk = cap // MOE_TM
    tok_flat = jnp.repeat(jnp.arange(n_tok, dtype=jnp.int32), 2)
    buf_tok = jnp.zeros((cap,), jnp.int32).at[dest].set(tok_flat)
    blk_start = jnp.arange(n_blk, dtype=jnp.int32) * MOE_TM
    blk_e = jnp.sum((pad_end[None, :] <= blk_start[:, None]).astype(jnp.int32), axis=1)
    blk_e = jnp.minimum(blk_e, N_EXPERTS - 1).astype(jnp.int32)

    xs = hb[buf_tok]
    n_act = (pad_end[-1:] // MOE_TM).astype(jnp.int32)
    w_in = _cast_layer_bf16(w_in_all, layer, tr=128)
    w_out = _cast_layer_bf16(w_out_all, layer, tr=D_FF_EXPERT // 4)
    y = _moe_ffn(xs, blk_e, n_act, w_in, w_out)
    dest2 = dest.reshape(n_tok, 2)
    y1 = y[dest2[:, 0]]
    y2 = y[dest2[:, 1]]
    return y1, y2, route


def kernel(x, positions, w_in_a, w_out_a, cmp_pos, cmp_w1, cmp_w2, w_kv_shared, w_q_b, w_out_b, sinks_b,
           ln_g, ln_b, dense_w_in, dense_w_out, moe_router, moe_w_in, moe_w_out):
    bsz, seq, d = x.shape
    n_a = DEPTH // 2
    tabs = _rope_tables(positions)
    h = x.reshape(bsz * seq, d)
    hb = h.astype(BF16)
    kvproj = None
    for l in range(DEPTH):
        if l < n_a:
            h, hb = _nsa_layer(h, hb, tabs, w_in_a[l], w_out_a[l], cmp_pos[l], cmp_w1[l], cmp_w2[l],
                               ln_g[l, 0], ln_b[l, 0], bsz, seq)
        else:
            b = l - n_a
            h, hb = _swa_layer(h, hb, tabs, kvproj, w_q_b[b], w_out_b[b], sinks_b[b], ln_g[l, 0], ln_b[l, 0],
                               bsz, seq)
        if l % 2 == 0:
            h, hb = _ffn_dense(h, dense_w_in[l // 2].astype(BF16), dense_w_out[l // 2].astype(BF16),
                               ln_g[l, 1][None, :], ln_b[l, 1][None, :])
        else:
            h, hb = _moe_layer(h, hb, moe_router[l // 2], moe_w_in, moe_w_out, l // 2,
                               ln_g[l, 1], ln_b[l, 1])
        if l == n_a - 1:
            kvproj = _shared_kv(hb, tabs, w_kv_shared)
    return h.reshape(bsz, seq, d)
```

```python
import functools

import numpy as np
import jax
import jax.numpy as jnp
from jax import lax
from jax.experimental import pallas as pl
from jax.experimental.pallas import tpu as pltpu

F32 = jnp.float32
BF16 = jnp.bfloat16

D_MODEL = 1024
DEPTH = 4
HEAD_DIM = 64
LANES = 128
ROT_DIM = HEAD_DIM // 4
ROPE_THETA = 500000.0
A_HEADS = 16
A_KV_HEADS = 4
A_GROUP = 4
CMP_BLOCK = 32
CMP_STRIDE = 16
CMP_HID = 256
SLC_BLOCK = 64
N_SEL = 16
N_FORCED = 3
NSA_WIN = 512
B_HEADS = 16
B_KV_HEADS = 2
B_GROUP = 8
SWA_WIN = 128
D_FF = 2816
N_EXPERTS = 8
D_FF_EXPERT = 3584
ALPHA = (2 * DEPTH) ** 0.25
LN_EPS = 1e-5
NEG = -1e30
FORCE = 1e9

TQB = 256
TQA = 256
CMP_CHUNK = 256
LOG2E = 1.4426950408889634
KT = 512
BLK_PER_KT = KT // SLC_BLOCK
KT_GROUP = 4
MOE_TM = 512
CAST_STREAMS = 4
VMEM_LIMIT = 56 * 1024 * 1024


def _cp(*sem):
    return pltpu.CompilerParams(dimension_semantics=sem, vmem_limit_bytes=VMEM_LIMIT)


def _dot(a, b):
    return jnp.dot(a, b, preferred_element_type=F32)


def _dot_nt(a, b):
    return lax.dot_general(a, b, (((1,), (1,)), ((), ())), preferred_element_type=F32)


def _sigmoid(x):
    return 1.0 / (1.0 + jnp.exp(-x))


def _layer_norm(z, g, b):
    mu = jnp.mean(z, axis=-1, keepdims=True)
    zc = z - mu
    var = jnp.mean(zc * zc, axis=-1, keepdims=True)
    return zc * lax.rsqrt(var + LN_EPS) * g + b


def _rope(x, c, s):
    half = ROT_DIM // 2
    ns = x.shape[1] // LANES
    rolled = [pltpu.roll(x[:, i * LANES:(i + 1) * LANES], half, 1) for i in range(ns)]
    z = rolled[0] if ns == 1 else jnp.concatenate(rolled, axis=1)
    return x * c + z * s


def _tile_lanes(t, n):
    return t if n == 1 else jnp.concatenate([t] * n, axis=1)


def _tile_rows(t, n):
    return t if n == 1 else jnp.concatenate([t] * n, axis=0)


def _normalize(acc, extra=0.0):
    den = pltpu.roll(acc, HEAD_DIM, 1) + extra
    return (acc / den)[:, :HEAD_DIM]


def _heads_to_lanes(o, heads):
    t = o.shape[0] // heads
    return jnp.concatenate([o[r * t:(r + 1) * t] for r in range(heads)], axis=1)


def _mm_kernel(x_ref, w_ref, o_ref):
    o_ref[...] = _dot(x_ref[...].astype(BF16), w_ref[...]).astype(o_ref.dtype)


def _mm(x, w, out_dtype, tm=512, tn=None):
    m, k = x.shape
    n = w.shape[1]
    tm = min(tm, m)
    tn = n if tn is None else tn
    return pl.pallas_call(
        _mm_kernel,
        grid=(m // tm, n // tn),
        in_specs=[pl.BlockSpec((tm, k), lambda i, j: (i, 0)), pl.BlockSpec((k, tn), lambda i, j: (0, j))],
        out_specs=pl.BlockSpec((tm, tn), lambda i, j: (i, j)),
        out_shape=jax.ShapeDtypeStruct((m, n), out_dtype),
        compiler_params=_cp("parallel", "arbitrary"),
        name="mm",
    )(x, w)


def _nsa_side_kernel(x_ref, w_ref, gate_ref, kc_ref, vc_ref):
    acc = _dot(x_ref[...], w_ref[...])
    ng = gate_ref.shape[1]
    nk = kc_ref.shape[1]
    gate_ref[...] = _sigmoid(acc[:, :ng])
    kc_ref[...] = acc[:, ng:ng + nk]
    vc_ref[...] = acc[:, ng + nk:]


def _nsa_side(hb, w, n_gate, n_kv, tm=512):
    m, k = hb.shape
    row = lambda n: pl.BlockSpec((tm, n), lambda i: (i, 0))
    return pl.pallas_call(
        _nsa_side_kernel,
        grid=(m // tm,),
        in_specs=[row(k), pl.BlockSpec(w.shape, lambda i: (0, 0))],
        out_specs=[row(n_gate), row(n_kv), row(n_kv)],
        out_shape=[jax.ShapeDtypeStruct((m, n_gate), F32), jax.ShapeDtypeStruct((m, n_kv), F32),
                   jax.ShapeDtypeStruct((m, n_kv), F32)],
        compiler_params=_cp("parallel"),
        name="nsa_side",
    )(hb, w)


def _cast_kernel(*refs):
    o_ref = refs[-1]
    tr = refs[0].shape[0]
    for k, x_ref in enumerate(refs[:-1]):
        o_ref[k * tr:(k + 1) * tr, :] = x_ref[...].astype(o_ref.dtype)


def _cast_layer_bf16(w, layer, tr):
    _, e, r, c = w.shape
    ns = CAST_STREAMS

    def in_spec(k):
        return pl.BlockSpec((None, None, tr, c), lambda ei, i: (layer, ei, ns * i + k, 0))

    return pl.pallas_call(
        _cast_kernel,
        grid=(e, r // (ns * tr)),
        in_specs=[in_spec(k) for k in range(ns)],
        out_specs=pl.BlockSpec((None, ns * tr, c), lambda ei, i: (ei, i, 0)),
        out_shape=jax.ShapeDtypeStruct((e, r, c), BF16),
        compiler_params=_cp("parallel", "arbitrary"),
        name="cast_bf16",
    )(*([w] * ns))


def _proj_kernel(x_ref, w_ref, c_ref, s_ref, o_ref, *, block_modes, tm, tn):
    i = pl.program_id(0)
    j = pl.program_id(1)
    acc = _dot(x_ref[...], w_ref[...])

    def piece(mode, s0, ns):
        a = acc[:, s0 * LANES:(s0 + ns) * LANES]
        lane = lax.broadcasted_iota(jnp.int32, a.shape, 1) % LANES
        if mode in ("rope", "rope_onehot"):
            a = _rope(a, _tile_lanes(c_ref[...], ns), _tile_lanes(s_ref[...], ns))
        if mode == "rope_onehot":
            row = i * tm + lax.broadcasted_iota(jnp.int32, a.shape, 0)
            a = a + jnp.where(lane == HEAD_DIM + (row // SLC_BLOCK) % BLK_PER_KT, 1.0, 0.0)
        if mode == "ones":
            a = a + jnp.where(lane >= HEAD_DIM, 1.0, 0.0)
        return a.astype(o_ref.dtype)

    for modes in sorted(set(block_modes)):
        pred = functools.reduce(jnp.logical_or, [j == jj for jj, mm in enumerate(block_modes) if mm == modes])

        @pl.when(pred)
        def _(modes=modes):
            runs = []
            for s, mode in enumerate(modes):
                if runs and runs[-1][0] == mode:
                    runs[-1][2] += 1
                else:
                    runs.append([mode, s, 1])
            pieces = [piece(*r) for r in runs]
            o_ref[...] = pieces[0] if len(pieces) == 1 else jnp.concatenate(pieces, axis=1)


def _proj(x, w, tabs, slab_modes, tn=1024, tm=1024):
    m, k = x.shape
    n = w.shape[1]
    tn = min(tn, n)
    spb = tn // LANES
    block_modes = tuple(tuple(slab_modes[jj * spb:(jj + 1) * spb]) for jj in range(n // tn))
    c, s = tabs
    return pl.pallas_call(
        functools.partial(_proj_kernel, block_modes=block_modes, tm=tm, tn=tn),
        grid=(m // tm, n // tn),
        in_specs=[
            pl.BlockSpec((tm, k), lambda i, j: (i, 0)),
            pl.BlockSpec((k, tn), lambda i, j: (0, j)),
            pl.BlockSpec((tm, LANES), lambda i, j: (i, 0)),
            pl.BlockSpec((tm, LANES), lambda i, j: (i, 0)),
        ],
        out_specs=pl.BlockSpec((tm, tn), lambda i, j: (i, j)),
        out_shape=jax.ShapeDtypeStruct((m, n), BF16),
        compiler_params=_cp("parallel", "arbitrary"),
        name="proj",
    )(x, w, c, s)


def _out_ln_kernel(x_ref, w_ref, h_ref, g_ref, b_ref, o_ref, ob_ref):
    mix = _dot(x_ref[...], w_ref[...])
    y = _layer_norm(ALPHA * h_ref[...] + mix, g_ref[...], b_ref[...])
    o_ref[...] = y
    ob_ref[...] = y.astype(BF16)


def _dual_out(m, d, tm):
    spec = pl.BlockSpec((tm, d), lambda i, *_: (i, 0))
    return [spec, spec], [jax.ShapeDtypeStruct((m, d), F32), jax.ShapeDtypeStruct((m, d), BF16)]


def _out_ln(x, w, h, g, b, tm=512):
    m, k = x.shape
    d = w.shape[1]
    out_specs, out_shape = _dual_out(m, d, tm)
    return pl.pallas_call(
        _out_ln_kernel,
        grid=(m // tm,),
        in_specs=[
            pl.BlockSpec((tm, k), lambda i: (i, 0)),
            pl.BlockSpec((k, d), lambda i: (0, 0)),
            pl.BlockSpec((tm, d), lambda i: (i, 0)),
            pl.BlockSpec((1, d), lambda i: (0, 0)),
            pl.BlockSpec((1, d), lambda i: (0, 0)),
        ],
        out_specs=out_specs,
        out_shape=out_shape,
        compiler_params=_cp("parallel"),
        name="out_ln",
    )(x, w, h, g, b)


def _mix_ffn_kernel(o_ref, wa_ref, h_ref, g1_ref, b1_ref, wg_ref, wu_ref, wo_ref, g2_ref, b2_ref, y_ref, yb_ref):
    h1 = _layer_norm(ALPHA * h_ref[...] + _dot(o_ref[...], wa_ref[...]), g1_ref[...], b1_ref[...])
    xb = h1.astype(BF16)
    gate = _dot(xb, wg_ref[...])
    up = _dot(xb, wu_ref[...])
    a = gate * _sigmoid(gate) * up
    y = _layer_norm(ALPHA * h1 + _dot(a.astype(BF16), wo_ref[...]), g2_ref[...], b2_ref[...])
    y_ref[...] = y
    yb_ref[...] = y.astype(BF16)


def _mix_ffn(o, w_attn, h, g1, b1, w_in, w_out, g2, b2, tm=512):
    m, d = h.shape
    ff = w_out.shape[0]
    once = pl.Buffered(1)
    row = pl.BlockSpec((tm, d), lambda i: (i, 0))
    vec = pl.BlockSpec((1, d), lambda i: (0, 0))
    out_specs, out_shape = _dual_out(m, d, tm)
    return pl.pallas_call(
        _mix_ffn_kernel,
        grid=(m // tm,),
        in_specs=[
            pl.BlockSpec((tm, o.shape[1]), lambda i: (i, 0)),
            pl.BlockSpec(w_attn.shape, lambda i: (0, 0), pipeline_mode=once),
            row, vec, vec,
            pl.BlockSpec((d, ff), lambda i: (0, 0), pipeline_mode=once),
            pl.BlockSpec((d, ff), lambda i: (0, 1), pipeline_mode=once),
            pl.BlockSpec((ff, d), lambda i: (0, 0), pipeline_mode=once),
            vec, vec,
        ],
        out_specs=out_specs,
        out_shape=out_shape,
        compiler_params=_cp("parallel"),
        name="mix_ffn",
    )(o, w_attn, h, g1, b1, w_in, w_in, w_out, g2, b2)


def _compress_kernel(p_ref, q_ref, pos_ref, w1_ref, w2_ref, o_ref):
    posb = _dot(pos_ref[...], w1_ref[...])[0:1, :]
    hid = p_ref[...] + q_ref[...] + posb
    act = jax.nn.gelu(hid, approximate=True)
    o_ref[...] = _dot(act.astype(BF16), w2_ref[...])


def _compress(pp, qq, pos8, w1, w2, tm=1024):
    m, hid = pp.shape
    dh = w2.shape[1]
    return pl.pallas_call(
        _compress_kernel,
        grid=(m // tm,),
        in_specs=[
            pl.BlockSpec((tm, hid), lambda i: (i, 0)),
            pl.BlockSpec((tm, hid), lambda i: (i, 0)),
            pl.BlockSpec(pos8.shape, lambda i: (0, 0)),
            pl.BlockSpec(w1.shape, lambda i: (0, 0)),
            pl.BlockSpec(w2.shape, lambda i: (0, 0)),
        ],
        out_specs=pl.BlockSpec((tm, dh), lambda i: (i, 0)),
        out_shape=jax.ShapeDtypeStruct((m, dh), F32),
        compiler_params=_cp("parallel"),
        name="compress",
    )(pp, qq, pos8, w1, w2)


def _cmp_select_kernel(q_ref, kc_ref, vc_ref, mapt_ref, oc_ref, sb_ref, *, ncp, nslc):
    qi = pl.program_id(2)
    t0 = qi * TQA
    rows = A_GROUP * TQA
    qb = q_ref[...]
    q4 = jnp.concatenate([qb[:, r * LANES:(r + 1) * LANES] for r in range(A_GROUP)], axis=0)

    def branch(nb):
        nk = (nb + 1) * CMP_CHUNK
        nr = nk * CMP_STRIDE // SLC_BLOCK
        s = _dot_nt(q4, kc_ref[0:nk, :])
        t_tok = t0 + lax.broadcasted_iota(jnp.int32, (TQA, nk), 0)
        n_idx = lax.broadcasted_iota(jnp.int32, (TQA, nk), 1)
        valid = (n_idx * CMP_STRIDE + CMP_BLOCK - 1 <= t_tok) & (n_idx < ncp - 1)
        s3 = s.reshape(A_GROUP, TQA, nk) + jnp.where(valid, 0.0, NEG)[None]
        m = jnp.max(s3, axis=2, keepdims=True)
        p = jnp.exp2(s3 - m)
        l = jnp.sum(p, axis=2, keepdims=True)
        pn = p * jnp.where(m > 0.5 * NEG, 1.0 / l, 0.0)
        oc = _dot(pn.reshape(rows, nk).astype(BF16), vc_ref[0:nk, :])
        oc_ref[...] = oc[:, :HEAD_DIM].reshape(A_GROUP, TQA, HEAD_DIM)
        psum = pn[0] + pn[1] + pn[2] + pn[3]
        p_hi = psum.astype(BF16)
        p_lo = (psum - p_hi.astype(F32)).astype(BF16)
        mapt = mapt_ref[0:nr, 0:nk]
        imp = _dot_nt(mapt, p_hi) + _dot_nt(mapt, p_lo)

        j_idx = lax.broadcasted_iota(jnp.int32, (nr, TQA), 0)
        cur = (t0 + lax.broadcasted_iota(jnp.int32, (nr, TQA), 1)) // SLC_BLOCK
        forced = (j_idx == 0) | (j_idx == cur) | (j_idx == cur - 1)
        score = jnp.where(j_idx > cur, -1.0, jnp.where(forced, -jnp.inf, imp))
        for _ in range(N_SEL - N_FORCED):
            mx = jnp.max(score, axis=0, keepdims=True)
            first = jnp.min(jnp.where(score == mx, j_idx, nr), axis=0, keepdims=True)
            score = jnp.where(j_idx == first, -jnp.inf, score)
        sb_t = jnp.where((score == -jnp.inf) & (j_idx <= cur), 0.0, NEG)
        if nr < nslc:
            sb_t = jnp.concatenate([sb_t, jnp.full((nslc - nr, TQA), NEG, F32)], axis=0)
        sb_ref[...] = sb_t.T

    bucket = ((qi + 1) * (TQA // CMP_STRIDE) - 1) // CMP_CHUNK
    for nb in range(ncp // CMP_CHUNK):
        pl.when(bucket == nb)(functools.partial(branch, nb))


def _cmp_select(proj, kc_cmp, vc_cmp, cmp_map_t, bsz, seq):
    ncp = kc_cmp.shape[2]
    nslc = seq // SLC_BLOCK
    nq = seq // TQA
    g_ = A_KV_HEADS
    assert nslc >= N_SEL and ncp % CMP_CHUNK == 0
    return pl.pallas_call(
        functools.partial(_cmp_select_kernel, ncp=ncp, nslc=nslc),
        grid=(bsz, g_, nq),
        in_specs=[
            pl.BlockSpec((TQA, A_GROUP * LANES), lambda b, g, i: (b * nq + i, g)),
            pl.BlockSpec((None, None, ncp, LANES), lambda b, g, i: (b, g, 0, 0)),
            pl.BlockSpec((None, None, ncp, LANES), lambda b, g, i: (b, g, 0, 0)),
            pl.BlockSpec((nslc, ncp), lambda b, g, i: (0, 0)),
        ],
        out_specs=[
            pl.BlockSpec((None, None, A_GROUP, TQA, HEAD_DIM), lambda b, g, i: (b, g, 0, i, 0)),
            pl.BlockSpec((TQA, nslc), lambda b, g, i: (b * nq + i, g)),
        ],
        out_shape=[
            jax.ShapeDtypeStruct((bsz, g_, A_GROUP, seq, HEAD_DIM), F32),
            jax.ShapeDtypeStruct((bsz * seq, g_ * nslc), F32),
        ],
        compiler_params=_cp("parallel", "parallel", "arbitrary"),
        name="cmp_select",
    )(proj, kc_cmp, vc_cmp, cmp_map_t)


def _sel_win_kernel(q_ref, c_ref, s_ref, bias_ref, ks_ref, vs_ref, kw_ref, vw_ref, oc_ref, gate_ref,
                    o_ref, acc_ref, *, nslc):
    qi = pl.program_id(2)
    t0 = qi * TQA
    rows = A_GROUP * TQA
    qb = q_ref[...]
    q4 = jnp.concatenate([qb[:, r * LANES:(r + 1) * LANES] for r in range(A_GROUP)], axis=0).astype(F32)
    q_rot = _rope(q4, _tile_rows(c_ref[...], A_GROUP), _tile_rows(s_ref[...], A_GROUP)) * LOG2E
    q_rot_b = q_rot.astype(BF16)
    bias = bias_ref[...]
    lane = lax.broadcasted_iota(jnp.int32, (TQA, LANES), 1)
    bias_lanes = (lane >= HEAD_DIM) & (lane < HEAD_DIM + BLK_PER_KT)
    t_q = t0 + lax.broadcasted_iota(jnp.int32, (TQA, KT), 0)
    k_off = lax.broadcasted_iota(jnp.int32, (TQA, KT), 1)

    def add_mask(s, mask_bias):
        return (s.reshape(A_GROUP, TQA, -1) + mask_bias[None]).reshape(s.shape)

    def scores(kt):
        shift = (HEAD_DIM + nslc - BLK_PER_KT * kt) % nslc
        rolled = pltpu.roll(bias, shift, 1)[:, :LANES]
        qa = (q_rot + _tile_rows(jnp.where(bias_lanes, rolled, 0.0), A_GROUP)).astype(BF16)
        k0 = pl.multiple_of(kt * KT, KT)
        return _dot_nt(qa, ks_ref[pl.ds(k0, KT), :])

    def causal_bias(kt):
        return jnp.where(kt * KT + k_off <= t_q, 0.0, NEG)

    def group(kt0, n, carry, diagonal_last):
        m, acc = carry
        ss = [scores(kt0 + i) for i in range(n)]
        if diagonal_last:
            ss[-1] = add_mask(ss[-1], causal_bias(kt0 + n - 1))
        m_row = functools.reduce(jnp.maximum, [jnp.max(s, axis=1, keepdims=True) for s in ss])
        m_new = jnp.maximum(m, jnp.broadcast_to(m_row, (rows, LANES)))
        m_keys = _tile_lanes(m_new, KT // LANES)
        pv = None
        for i, s in enumerate(ss):
            k0 = pl.multiple_of((kt0 + i) * KT, KT)
            d = _dot(jnp.exp2(s - m_keys).astype(BF16), vs_ref[pl.ds(k0, KT), :])
            pv = d if pv is None else pv + d
        return m_new, jnp.exp2(m - m_new) * acc + pv

    kt_d = t0 // KT
    n_quads = kt_d // KT_GROUP
    carry = (jnp.full((rows, LANES), NEG, F32), jnp.zeros((rows, LANES), F32))
    carry = lax.fori_loop(0, n_quads, lambda j, c: group(j * KT_GROUP, KT_GROUP, c, False), carry)
    for rem in range(KT_GROUP):
        @pl.when(kt_d - n_quads * KT_GROUP == rem)
        def _(rem=rem):
            acc_ref[...] = group(kt_d - rem, rem + 1, carry, True)[1]
    o_s = _normalize(acc_ref[...])

    wspan = NSA_WIN + TQA
    w0 = pl.multiple_of(jnp.maximum(t0 - NSA_WIN, 0), TQA)
    s_w = _dot_nt(q_rot_b, kw_ref[pl.ds(w0, wspan), :])
    tw = t0 + lax.broadcasted_iota(jnp.int32, (TQA, wspan), 0)
    diff = tw - (w0 + lax.broadcasted_iota(jnp.int32, (TQA, wspan), 1))
    s_w = add_mask(s_w, jnp.where((diff >= 0) & (diff < NSA_WIN), 0.0, NEG))
    p_w = jnp.exp2(s_w - jnp.max(s_w, axis=1, keepdims=True))
    acc_w = _dot(p_w.astype(BF16), vw_ref[pl.ds(w0, wspan), :])
    o_w = _normalize(acc_w)

    gates = gate_ref[...]

    def gcol(c):
        return jnp.concatenate([gates[:, c * A_GROUP + r:c * A_GROUP + r + 1] for r in range(A_GROUP)], axis=0)

    o_c = oc_ref[...].reshape(rows, HEAD_DIM)
    o = gcol(0) * o_c + gcol(1) * o_s + gcol(2) * o_w
    o_ref[...] = _heads_to_lanes(o, A_GROUP).astype(o_ref.dtype)


def _sel_win(proj, tabs, selbias, o_c, gates, bsz, seq):
    nslc = seq // SLC_BLOCK
    nq = seq // TQA
    g_ = A_KV_HEADS
    c, s = tabs
    tab_spec = pl.BlockSpec((TQA, LANES), lambda b, g, i: (b * nq + i, 0))
    q_slabs = A_HEADS

    def kv_spec(base):
        return pl.BlockSpec((seq, LANES), lambda b, g, i: (b, base + g))

    return pl.pallas_call(
        functools.partial(_sel_win_kernel, nslc=nslc),
        grid=(bsz, g_, nq),
        in_specs=[
            pl.BlockSpec((TQA, A_GROUP * LANES), lambda b, g, i: (b * nq + i, g)),
            tab_spec, tab_spec,
            pl.BlockSpec((TQA, nslc), lambda b, g, i: (b * nq + i, g)),
            kv_spec(q_slabs), kv_spec(q_slabs + g_), kv_spec(q_slabs + 2 * g_), kv_spec(q_slabs + 3 * g_),
            pl.BlockSpec((None, None, A_GROUP, TQA, HEAD_DIM), lambda b, g, i: (b, g, 0, i, 0)),
            pl.BlockSpec((TQA, LANES), lambda b, g, i: (b * nq + i, g)),
        ],
        out_specs=pl.BlockSpec((TQA, A_GROUP * HEAD_DIM), lambda b, g, i: (b * nq + i, g)),
        out_shape=jax.ShapeDtypeStruct((bsz * seq, A_HEADS * HEAD_DIM), BF16),
        scratch_shapes=[pltpu.VMEM((A_GROUP * TQA, LANES), F32)],
        compiler_params=_cp("parallel", "parallel", "arbitrary"),
        name="sel_win",
    )(proj, c, s, selbias, proj, proj, proj, proj, o_c, gates)


def _swa_kernel(q_ref, k_ref, v_ref, sink_ref, o_ref):
    qi = pl.program_id(2)
    t0 = qi * TQB
    rows = B_GROUP * TQB
    span = SWA_WIN + TQB
    qb = q_ref[...]
    q8 = jnp.concatenate([qb[:, r * LANES:(r + 1) * LANES] for r in range(B_GROUP)], axis=0)
    w0 = pl.multiple_of(jnp.maximum(t0 - SWA_WIN, 0), SWA_WIN)
    s = _dot_nt(q8, k_ref[pl.ds(w0, span), :])
    tq = t0 + lax.broadcasted_iota(jnp.int32, (TQB, span), 0)
    diff = tq - (w0 + lax.broadcasted_iota(jnp.int32, (TQB, span), 1))
    band = jnp.where((diff >= 0) & (diff < SWA_WIN), 0.0, NEG)
    s = (s.reshape(B_GROUP, TQB, span) + band[None]).reshape(rows, span)
    sk = sink_ref[...]
    sink = jnp.concatenate([jnp.broadcast_to(sk[r:r + 1, :], (TQB, LANES)) for r in range(B_GROUP)], axis=0)
    m = jnp.maximum(jnp.broadcast_to(jnp.max(s, axis=1, keepdims=True), (rows, LANES)), sink)
    e = jnp.exp(s - _tile_lanes(m, span // LANES))
    acc = _dot(e.astype(BF16), v_ref[pl.ds(w0, span), :])
    o = _normalize(acc, jnp.exp(sink - m))
    o_ref[...] = _heads_to_lanes(o, B_GROUP).astype(o_ref.dtype)


def _swa(qproj, kvproj, sinks, bsz, seq):
    nq = seq // TQB
    g_ = B_KV_HEADS
    return pl.pallas_call(
        _swa_kernel,
        grid=(bsz, g_, nq),
        in_specs=[
            pl.BlockSpec((TQB, B_GROUP * LANES), lambda b, g, i: (b * nq + i, g)),
            pl.BlockSpec((seq, LANES), lambda b, g, i: (b, g)),
            pl.BlockSpec((seq, LANES), lambda b, g, i: (b, g_ + g)),
            pl.BlockSpec((None, B_GROUP, LANES), lambda b, g, i: (g, 0, 0)),
        ],
        out_specs=pl.BlockSpec((TQB, B_GROUP * HEAD_DIM), lambda b, g, i: (b * nq + i, g)),
        out_shape=jax.ShapeDtypeStruct((bsz * seq, B_HEADS * HEAD_DIM), BF16),
        compiler_params=_cp("parallel", "parallel", "arbitrary"),
        name="swa",
    )(qproj, kvproj, kvproj, sinks)


def _router_kernel(x_ref, wh_ref, wl_ref, o_ref):
    x = x_ref[...]
    xh = x.astype(BF16)
    xl = (x - xh.astype(F32)).astype(BF16)
    logits = _dot(xh, wh_ref[...]) + _dot(xh, wl_ref[...]) + _dot(xl, wh_ref[...])
    lane = lax.broadcasted_iota(jnp.int32, logits.shape, 1)
    lg = jnp.where(lane < N_EXPERTS, logits, -jnp.inf)
    m1 = jnp.max(lg, axis=1, keepdims=True)
    i1 = jnp.min(jnp.where(lg == m1, lane, LANES), axis=1, keepdims=True)
    lg2 = jnp.where(lane == i1, -jnp.inf, lg)
    m2 = jnp.max(lg2, axis=1, keepdims=True)
    i2 = jnp.min(jnp.where(lg2 == m2, lane, LANES), axis=1, keepdims=True)
    e2 = jnp.exp(m2 - m1)
    g1 = 1.0 / (1.0 + e2)
    g2 = e2 * g1
    out = jnp.where(lane == 0, i1.astype(F32), jnp.where(lane == 1, i2.astype(F32),
                    jnp.where(lane == 2, g1, jnp.where(lane == 3, g2, 0.0))))
    o_ref[...] = out


def _router(h, w_hi, w_lo, tm=512):
    m, d = h.shape
    return pl.pallas_call(
        _router_kernel,
        grid=(m // tm,),
        in_specs=[pl.BlockSpec((tm, d), lambda i: (i, 0)), pl.BlockSpec((d, LANES), lambda i: (0, 0)),
                  pl.BlockSpec((d, LANES), lambda i: (0, 0))],
        out_specs=pl.BlockSpec((tm, LANES), lambda i: (i, 0)),
        out_shape=jax.ShapeDtypeStruct((m, LANES), F32),
        compiler_params=_cp("parallel"),
        name="router",
    )(h, w_hi, w_lo)


def _moe_ffn_kernel(blk_e_ref, n_act_ref, x_ref, wg_ref, wu_ref, wo_ref, o_ref, acc_ref):
    i = pl.program_id(0)
    c = pl.program_id(1)

    @pl.when(c == 0)
    def _():
        acc_ref[...] = jnp.zeros_like(acc_ref)

    @pl.when(i < n_act_ref[0])
    def _():
        xb = x_ref[...]
        gate = _dot(xb, wg_ref[...])
        up = _dot(xb, wu_ref[...])
        a = gate * _sigmoid(gate) * up
        acc_ref[...] += _dot(a.astype(BF16), wo_ref[...])

    @pl.when(c == pl.num_programs(1) - 1)
    def _():
        o_ref[...] = acc_ref[...]


def _moe_ffn(xs, blk_e, n_act, w_in, w_out, tm=MOE_TM, fc=1792):
    cap, d = xs.shape
    ff = w_out.shape[1]
    nc = ff // fc

    def chunk(i, c, n_act_ref):
        return jnp.where(i < n_act_ref[0], c, nc - 1)

    grid_spec = pltpu.PrefetchScalarGridSpec(
        num_scalar_prefetch=2,
        grid=(cap // tm, nc),
        in_specs=[
            pl.BlockSpec((tm, d), lambda i, c, e, na: (i, 0)),
            pl.BlockSpec((None, d, fc), lambda i, c, e, na: (e[i], 0, chunk(i, c, na))),
            pl.BlockSpec((None, d, fc), lambda i, c, e, na: (e[i], 0, nc + chunk(i, c, na))),
            pl.BlockSpec((None, fc, d), lambda i, c, e, na: (e[i], chunk(i, c, na), 0)),
        ],
        out_specs=pl.BlockSpec((tm, d), lambda i, c, e, na: (i, 0)),
        scratch_shapes=[pltpu.VMEM((tm, d), F32)],
    )
    return pl.pallas_call(
        _moe_ffn_kernel,
        grid_spec=grid_spec,
        out_shape=jax.ShapeDtypeStruct((cap, d), F32),
        compiler_params=_cp("parallel", "arbitrary"),
        name="moe_ffn",
    )(blk_e, n_act, xs, w_in, w_in, w_out)


def _combine_ln_kernel(h_ref, y1_ref, y2_ref, r_ref, g_ref, b_ref, o_ref, ob_ref):
    r = r_ref[...]
    ffn = r[:, 2:3] * y1_ref[...] + r[:, 3:4] * y2_ref[...]
    y = _layer_norm(ALPHA * h_ref[...] + ffn, g_ref[...], b_ref[...])
    o_ref[...] = y
    ob_ref[...] = y.astype(BF16)


def _combine_ln(h, y1, y2, route, g, b, tm=512):
    m, d = h.shape
    row = pl.BlockSpec((tm, d), lambda i: (i, 0))
    vec = pl.BlockSpec((1, d), lambda i: (0, 0))
    out_specs, out_shape = _dual_out(m, d, tm)
    return pl.pallas_call(
        _combine_ln_kernel,
        grid=(m // tm,),
        in_specs=[row, row, row, pl.BlockSpec((tm, LANES), lambda i: (i, 0)), vec, vec],
        out_specs=out_specs,
        out_shape=out_shape,
        compiler_params=_cp("parallel"),
        name="combine_ln",
    )(h, y1, y2, route, g, b)


def _pad_heads(w, heads, rope_copy=False):
    d = w.shape[0]
    half = ROT_DIM // 2
    w = w.reshape(d, heads, HEAD_DIM)
    if rope_copy:
        fill = jnp.zeros((d, heads, LANES - HEAD_DIM - half), w.dtype)
        w = jnp.concatenate([w, fill, w[:, :, half:ROT_DIM]], axis=2)
    else:
        w = jnp.pad(w, ((0, 0), (0, 0), (0, LANES - HEAD_DIM)))
    return w.reshape(d, heads * LANES)


def _rope_tables(positions):
    half = ROT_DIM // 2
    inv = ROPE_THETA ** (-jnp.arange(0, ROT_DIM, 2, dtype=F32) / ROT_DIM)
    ang = positions.astype(F32).reshape(-1, 1) * inv
    cos, sin = jnp.cos(ang), jnp.sin(ang)
    n = ang.shape[0]
    c = jnp.concatenate([cos, cos, jnp.ones((n, LANES - ROT_DIM - half), F32), jnp.zeros((n, half), F32)], axis=1)
    s = jnp.concatenate([-sin, sin, jnp.zeros((n, LANES - ROT_DIM), F32)], axis=1)
    return c, s


def _cmp_to_slc(ncp, nslc):
    n = np.arange(ncp)[:, None]
    j = np.arange(nslc)[None, :]
    overlap = (np.minimum(n * CMP_STRIDE + CMP_BLOCK, j * SLC_BLOCK + SLC_BLOCK)
               - np.maximum(n * CMP_STRIDE, j * SLC_BLOCK))
    m = np.clip(overlap, 0, None).astype(np.float32) / CMP_BLOCK
    m[ncp - 1, :] = 0.0
    return jnp.asarray(m, BF16)


def _compress_branch(t, pos, w1, w2, bsz, seq):
    g_ = A_KV_HEADS
    nch = seq // CMP_STRIDE
    half = CMP_STRIDE * HEAD_DIM
    a = t.reshape(bsz * nch, CMP_STRIDE * g_ * HEAD_DIM)
    w1b = w1.astype(BF16)
    w1cat = jnp.concatenate([w1b[:half], w1b[half:]], axis=1)
    eye = jnp.eye(g_, dtype=BF16)
    w1blk = (w1cat.reshape(CMP_STRIDE, 1, HEAD_DIM, 1, 2 * CMP_HID) * eye[None, :, None, :, None])
    w1blk = w1blk.reshape(CMP_STRIDE * g_ * HEAD_DIM, g_ * 2 * CMP_HID)
    pq = _mm(a, w1blk, F32, tm=256, tn=2 * CMP_HID).reshape(bsz, nch, g_, 2 * CMP_HID)
    pp = pq[..., :CMP_HID].reshape(-1, CMP_HID)
    qq = jnp.concatenate([pq[:, 1:, :, CMP_HID:], jnp.zeros((bsz, 1, g_, CMP_HID), F32)], axis=1).reshape(-1, CMP_HID)
    pos8 = jnp.zeros((8, CMP_BLOCK * HEAD_DIM), BF16).at[0].set(pos.reshape(-1).astype(BF16))
    out = _compress(pp, qq, pos8, w1b, w2.astype(BF16))
    out = out.reshape(bsz, nch, g_, HEAD_DIM).transpose(0, 2, 1, 3)
    return out


def _nsa_attn(hb, tabs, w_in, cmp_pos, cmp_w1, cmp_w2, bsz, seq):
    hb = hb.astype(BF16)
    g_ = A_KV_HEADS
    aq = A_HEADS * HEAD_DIM
    akv = g_ * HEAD_DIM
    wq = w_in[:, :aq] * (HEAD_DIM ** -0.5)
    w_kc, w_vc, w_ks, w_vs, w_kw, w_vw = (w_in[:, aq + i * akv: aq + (i + 1) * akv] for i in range(6))
    w_gl = w_in[:, aq + 6 * akv:]
    w_big = jnp.concatenate([_pad_heads(wq, A_HEADS, True), _pad_heads(w_ks, g_, True), _pad_heads(w_vs, g_),
                             _pad_heads(w_kw, g_, True), _pad_heads(w_vw, g_)], axis=1).astype(BF16)
    slab_modes = ["plain"] * A_HEADS + ["rope_onehot"] * g_ + ["ones"] * g_ + ["rope"] * g_ + ["ones"] * g_
    proj = _proj(hb, w_big, tabs, slab_modes)

    w_gl_g = w_gl.reshape(-1, 3, g_, A_GROUP).transpose(0, 2, 1, 3).reshape(-1, g_, 3 * A_GROUP)
    w_gl_g = jnp.pad(w_gl_g, ((0, 0), (0, 0), (0, LANES - 3 * A_GROUP))).reshape(-1, g_ * LANES)
    w_side = jnp.concatenate([w_gl_g, w_kc, w_vc], axis=1).astype(BF16)
    gates, kc_in, vc_in = _nsa_side(hb, w_side, g_ * LANES, akv)
    kc = _compress_branch(kc_in, cmp_pos[0], cmp_w1[0], cmp_w2[0], bsz, seq)
    vc = _compress_branch(vc_in, cmp_pos[1], cmp_w1[1], cmp_w2[1], bsz, seq)
    pad = ((0, 0), (0, 0), (0, 0), (0, LANES - HEAD_DIM))
    kc = jnp.pad(kc * LOG2E, pad).astype(BF16)
    vc = jnp.pad(vc, pad).astype(BF16)
    ncp = seq // CMP_STRIDE
    cmp_map_t = _cmp_to_slc(ncp, seq // SLC_BLOCK).T

    o_c, selbias = _cmp_select(proj, kc, vc, cmp_map_t, bsz, seq)
    o = _sel_win(proj, tabs, selbias, o_c, gates, bsz, seq)
    return o, selbias, o_c


def _shared_kv(hb, tabs, w_kv):
    g_ = B_KV_HEADS
    bkv = g_ * HEAD_DIM
    w = jnp.concatenate([_pad_heads(w_kv[:, :bkv], g_, True), _pad_heads(w_kv[:, bkv:], g_)], axis=1).astype(BF16)
    return _proj(hb.astype(BF16), w, tabs, ["rope"] * g_ + ["ones"] * g_)


def _swa_attn(hb, tabs, kvproj, w_q, sinks, bsz, seq):
    wq = _pad_heads(w_q * (HEAD_DIM ** -0.5), B_HEADS, True).astype(BF16)
    qproj = _proj(hb.astype(BF16), wq, tabs, ["rope"] * B_HEADS)
    sk = jnp.broadcast_to(sinks.astype(F32).reshape(B_KV_HEADS, B_GROUP, 1), (B_KV_HEADS, B_GROUP, LANES))
    return _swa(qproj, kvproj, sk, bsz, seq)


def _moe_layer(h, hb, w_router, w_in_all, w_out_all, layer, ln_g, ln_b):
    y1, y2, route = _moe_experts(h, hb, w_router, w_in_all, w_out_all, layer)
    return _combine_ln(h, y1, y2, route, ln_g[None, :], ln_b[None, :])


def _moe_ffn_out(h, w_router, w_in_all, w_out_all, layer):
    y1, y2, route = _moe_experts(h, h.astype(BF16), w_router, w_in_all, w_out_all, layer)
    return route[:, 2:3] * y1 + route[:, 3:4] * y2


def _moe_experts(h, hb, w_router, w_in_all, w_out_all, layer):
    n_tok, d = h.shape
    wr = jnp.pad(w_router, ((0, 0), (0, LANES - N_EXPERTS)))
    wr_hi = wr.astype(BF16)
    wr_lo = (wr - wr_hi.astype(F32)).astype(BF16)
    route = _router(h, wr_hi, wr_lo)
    top_e = route[:, :2].astype(jnp.int32)

    e_flat = top_e.reshape(-1)
    onehot = (e_flat[:, None] == jnp.arange(N_EXPERTS)[None, :]).astype(jnp.int32)
    csum = jnp.cumsum(onehot, axis=0)
    counts = csum[-1]
    rank = jnp.take_along_axis(csum, e_flat[:, None], axis=1)[:, 0] - 1
    padded = (counts + MOE_TM - 1) // MOE_TM * MOE_TM
    pad_end = jnp.cumsum(padded)
    pad_start = pad_end - padded
    dest = pad_start[e_flat] + rank
    cap = n_tok * 2 + N_EXPERTS * MOE_TM
    n_blk = cap // MOE_TM
    tok_flat = jnp.repeat(jnp.arange(n_tok, dtype=jnp.int32), 2)
    buf_tok = jnp.zeros((cap,), jnp.int32).at[dest].set(tok_flat)
    blk_start = jnp.arange(n_blk, dtype=jnp.int32) * MOE_TM
    blk_e = jnp.sum((pad_end[None, :] <= blk_start[:, None]).astype(jnp.int32), axis=1)
    blk_e = jnp.minimum(blk_e, N_EXPERTS - 1).astype(jnp.int32)

    xs = hb[buf_tok]
    n_act = (pad_end[-1:] // MOE_TM).astype(jnp.int32)
    w_in = _cast_layer_bf16(w_in_all, layer, tr=64)
    w_out = _cast_layer_bf16(w_out_all, layer, tr=D_FF_EXPERT // 8)
    y = _moe_ffn(xs, blk_e, n_act, w_in, w_out)
    dest2 = dest.reshape(n_tok, 2)
    y1 = y[dest2[:, 0]]
    y2 = y[dest2[:, 1]]
    return y1, y2, route


def kernel(x, positions, w_in_a, w_out_a, cmp_pos, cmp_w1, cmp_w2, w_kv_shared, w_q_b, w_out_b, sinks_b,
           ln_g, ln_b, dense_w_in, dense_w_out, moe_router, moe_w_in, moe_w_out):
    bsz, seq, d = x.shape
    n_a = DEPTH // 2
    tabs = _rope_tables(positions)
    h = x.reshape(bsz * seq, d)
    hb = h.astype(BF16)
    kvproj = None
    for l in range(DEPTH):
        if l < n_a:
            o = _nsa_attn(hb, tabs, w_in_a[l], cmp_pos[l], cmp_w1[l], cmp_w2[l], bsz, seq)[0]
            w_attn = w_out_a[l].astype(BF16)
        else:
            b = l - n_a
            o = _swa_attn(hb, tabs, kvproj, w_q_b[b], sinks_b[b], bsz, seq)
            w_attn = w_out_b[b].astype(BF16)
        if l % 2 == 0:
            h, hb = _mix_ffn(o, w_attn, h, ln_g[l, 0][None, :], ln_b[l, 0][None, :],
                             dense_w_in[l // 2].astype(BF16), dense_w_out[l // 2].astype(BF16),
                             ln_g[l, 1][None, :], ln_b[l, 1][None, :])
        else:
            h, hb = _out_ln(o, w_attn, h, ln_g[l, 0][None, :], ln_b[l, 0][None, :])
            h, hb = _moe_layer(h, hb, moe_router[l // 2], moe_w_in, moe_w_out, l // 2,
                               ln_g[l, 1], ln_b[l, 1])
        if l == n_a - 1:
            kvproj = _shared_kv(hb, tabs, w_kv_shared)
    return h.reshape(bsz, seq, d)
```

```python
import functools

import numpy as np
import jax
import jax.numpy as jnp
from jax import lax
from jax.experimental import pallas as pl
from jax.experimental.pallas import tpu as pltpu

F32 = jnp.float32
BF16 = jnp.bfloat16

D_MODEL = 1024
DEPTH = 4
HEAD_DIM = 64
LANES = 128
ROT_DIM = HEAD_DIM // 4
ROPE_THETA = 500000.0
A_HEADS = 16
A_KV_HEADS = 4
A_GROUP = 4
CMP_BLOCK = 32
CMP_STRIDE = 16
CMP_HID = 256
SLC_BLOCK = 64
N_SEL = 16
N_FORCED = 3
NSA_WIN = 512
B_HEADS = 16
B_KV_HEADS = 2
B_GROUP = 8
SWA_WIN = 128
D_FF = 2816
N_EXPERTS = 8
D_FF_EXPERT = 3584
ALPHA = (2 * DEPTH) ** 0.25
LN_EPS = 1e-5
NEG = -1e30
FORCE = 1e9

TQB = 256
TQA = 256
CMP_CHUNK = 256
LOG2E = 1.4426950408889634
KT = 512
BLK_PER_KT = KT // SLC_BLOCK
KT_GROUP = 4
MOE_TM = 512
VMEM_LIMIT = 56 * 1024 * 1024


def _cp(*sem):
    return pltpu.CompilerParams(dimension_semantics=sem, vmem_limit_bytes=VMEM_LIMIT)


def _dot(a, b):
    return jnp.dot(a, b, preferred_element_type=F32)


def _dot_nt(a, b):
    return lax.dot_general(a, b, (((1,), (1,)), ((), ())), preferred_element_type=F32)


def _sigmoid(x):
    return 1.0 / (1.0 + jnp.exp(-x))


def _layer_norm(z, g, b):
    mu = jnp.mean(z, axis=-1, keepdims=True)
    zc = z - mu
    var = jnp.mean(zc * zc, axis=-1, keepdims=True)
    return zc * lax.rsqrt(var + LN_EPS) * g + b


def _rope(x, c, s):
    half = ROT_DIM // 2
    ns = x.shape[1] // LANES
    rolled = [pltpu.roll(x[:, i * LANES:(i + 1) * LANES], half, 1) for i in range(ns)]
    z = rolled[0] if ns == 1 else jnp.concatenate(rolled, axis=1)
    return x * c + z * s


def _tile_lanes(t, n):
    return t if n == 1 else jnp.concatenate([t] * n, axis=1)


def _tile_rows(t, n):
    return t if n == 1 else jnp.concatenate([t] * n, axis=0)


def _normalize(acc, extra=0.0):
    den = pltpu.roll(acc, HEAD_DIM, 1) + extra
    return (acc / den)[:, :HEAD_DIM]


def _heads_to_lanes(o, heads):
    t = o.shape[0] // heads
    return jnp.concatenate([o[r * t:(r + 1) * t] for r in range(heads)], axis=1)


def _mm_kernel(x_ref, w_ref, o_ref):
    o_ref[...] = _dot(x_ref[...].astype(BF16), w_ref[...]).astype(o_ref.dtype)


def _mm(x, w, out_dtype, tm=512, tn=None):
    m, k = x.shape
    n = w.shape[1]
    tm = min(tm, m)
    tn = n if tn is None else tn
    return pl.pallas_call(
        _mm_kernel,
        grid=(m // tm, n // tn),
        in_specs=[pl.BlockSpec((tm, k), lambda i, j: (i, 0)), pl.BlockSpec((k, tn), lambda i, j: (0, j))],
        out_specs=pl.BlockSpec((tm, tn), lambda i, j: (i, j)),
        out_shape=jax.ShapeDtypeStruct((m, n), out_dtype),
        compiler_params=_cp("parallel", "arbitrary"),
        name="mm",
    )(x, w)


def _nsa_side_kernel(x_ref, w_ref, gate_ref, kc_ref, vc_ref):
    acc = _dot(x_ref[...], w_ref[...])
    ng = gate_ref.shape[1]
    nk = kc_ref.shape[1]
    gate_ref[...] = _sigmoid(acc[:, :ng])
    kc_ref[...] = acc[:, ng:ng + nk]
    vc_ref[...] = acc[:, ng + nk:]


def _nsa_side(hb, w, n_gate, n_kv, tm=512):
    m, k = hb.shape
    row = lambda n: pl.BlockSpec((tm, n), lambda i: (i, 0))
    return pl.pallas_call(
        _nsa_side_kernel,
        grid=(m // tm,),
        in_specs=[row(k), pl.BlockSpec(w.shape, lambda i: (0, 0))],
        out_specs=[row(n_gate), row(n_kv), row(n_kv)],
        out_shape=[jax.ShapeDtypeStruct((m, n_gate), F32), jax.ShapeDtypeStruct((m, n_kv), F32),
                   jax.ShapeDtypeStruct((m, n_kv), F32)],
        compiler_params=_cp("parallel"),
        name="nsa_side",
    )(hb, w)


def _proj_kernel(x_ref, w_ref, c_ref, s_ref, o_ref, *, block_modes, tm, tn):
    i = pl.program_id(0)
    j = pl.program_id(1)
    acc = _dot(x_ref[...], w_ref[...])

    def piece(mode, s0, ns):
        a = acc[:, s0 * LANES:(s0 + ns) * LANES]
        lane = lax.broadcasted_iota(jnp.int32, a.shape, 1) % LANES
        if mode in ("rope", "rope_onehot"):
            a = _rope(a, _tile_lanes(c_ref[...], ns), _tile_lanes(s_ref[...], ns))
        if mode == "rope_onehot":
            row = i * tm + lax.broadcasted_iota(jnp.int32, a.shape, 0)
            a = a + jnp.where(lane == HEAD_DIM + (row // SLC_BLOCK) % BLK_PER_KT, 1.0, 0.0)
        if mode == "ones":
            a = a + jnp.where(lane >= HEAD_DIM, 1.0, 0.0)
        return a.astype(o_ref.dtype)

    for modes in sorted(set(block_modes)):
        pred = functools.reduce(jnp.logical_or, [j == jj for jj, mm in enumerate(block_modes) if mm == modes])

        @pl.when(pred)
        def _(modes=modes):
            runs = []
            for s, mode in enumerate(modes):
                if runs and runs[-1][0] == mode:
                    runs[-1][2] += 1
                else:
                    runs.append([mode, s, 1])
            pieces = [piece(*r) for r in runs]
            o_ref[...] = pieces[0] if len(pieces) == 1 else jnp.concatenate(pieces, axis=1)


def _proj(x, w, tabs, slab_modes, tn=1024, tm=1024):
    m, k = x.shape
    n = w.shape[1]
    tn = min(tn, n)
    spb = tn // LANES
    block_modes = tuple(tuple(slab_modes[jj * spb:(jj + 1) * spb]) for jj in range(n // tn))
    c, s = tabs
    return pl.pallas_call(
        functools.partial(_proj_kernel, block_modes=block_modes, tm=tm, tn=tn),
        grid=(m // tm, n // tn),
        in_specs=[
            pl.BlockSpec((tm, k), lambda i, j: (i, 0)),
            pl.BlockSpec((k, tn), lambda i, j: (0, j)),
            pl.BlockSpec((tm, LANES), lambda i, j: (i, 0)),
            pl.BlockSpec((tm, LANES), lambda i, j: (i, 0)),
        ],
        out_specs=pl.BlockSpec((tm, tn), lambda i, j: (i, j)),
        out_shape=jax.ShapeDtypeStruct((m, n), BF16),
        compiler_params=_cp("parallel", "arbitrary"),
        name="proj",
    )(x, w, c, s)


def _out_ln_kernel(x_ref, w_ref, h_ref, g_ref, b_ref, o_ref, ob_ref):
    mix = _dot(x_ref[...], w_ref[...])
    y = _layer_norm(ALPHA * h_ref[...] + mix, g_ref[...], b_ref[...])
    o_ref[...] = y
    ob_ref[...] = y.astype(BF16)


def _dual_out(m, d, tm):
    spec = pl.BlockSpec((tm, d), lambda i, *_: (i, 0))
    return [spec, spec], [jax.ShapeDtypeStruct((m, d), F32), jax.ShapeDtypeStruct((m, d), BF16)]


def _out_ln(x, w, h, g, b, tm=512):
    m, k = x.shape
    d = w.shape[1]
    out_specs, out_shape = _dual_out(m, d, tm)
    return pl.pallas_call(
        _out_ln_kernel,
        grid=(m // tm,),
        in_specs=[
            pl.BlockSpec((tm, k), lambda i: (i, 0)),
            pl.BlockSpec((k, d), lambda i: (0, 0)),
            pl.BlockSpec((tm, d), lambda i: (i, 0)),
            pl.BlockSpec((1, d), lambda i: (0, 0)),
            pl.BlockSpec((1, d), lambda i: (0, 0)),
        ],
        out_specs=out_specs,
        out_shape=out_shape,
        compiler_params=_cp("parallel"),
        name="out_ln",
    )(x, w, h, g, b)


def _mix_ffn_kernel(o_ref, wa_ref, h_ref, g1_ref, b1_ref, wg_ref, wu_ref, wo_ref, g2_ref, b2_ref, y_ref, yb_ref):
    h1 = _layer_norm(ALPHA * h_ref[...] + _dot(o_ref[...], wa_ref[...]), g1_ref[...], b1_ref[...])
    xb = h1.astype(BF16)
    gate = _dot(xb, wg_ref[...])
    up = _dot(xb, wu_ref[...])
    a = gate * _sigmoid(gate) * up
    y = _layer_norm(ALPHA * h1 + _dot(a.astype(BF16), wo_ref[...]), g2_ref[...], b2_ref[...])
    y_ref[...] = y
    yb_ref[...] = y.astype(BF16)


def _mix_ffn(o, w_attn, h, g1, b1, w_in, w_out, g2, b2, tm=512):
    m, d = h.shape
    ff = w_out.shape[0]
    once = pl.Buffered(1)
    row = pl.BlockSpec((tm, d), lambda i: (i, 0))
    vec = pl.BlockSpec((1, d), lambda i: (0, 0))
    out_specs, out_shape = _dual_out(m, d, tm)
    return pl.pallas_call(
        _mix_ffn_kernel,
        grid=(m // tm,),
        in_specs=[
            pl.BlockSpec((tm, o.shape[1]), lambda i: (i, 0)),
            pl.BlockSpec(w_attn.shape, lambda i: (0, 0), pipeline_mode=once),
            row, vec, vec,
            pl.BlockSpec((d, ff), lambda i: (0, 0), pipeline_mode=once),
            pl.BlockSpec((d, ff), lambda i: (0, 1), pipeline_mode=once),
            pl.BlockSpec((ff, d), lambda i: (0, 0), pipeline_mode=once),
            vec, vec,
        ],
        out_specs=out_specs,
        out_shape=out_shape,
        compiler_params=_cp("parallel"),
        name="mix_ffn",
    )(o, w_attn, h, g1, b1, w_in, w_in, w_out, g2, b2)


def _compress_kernel(pq_ref, pos_ref, w1_ref, w2_ref, o_ref):
    nch = pq_ref.shape[0]
    hid = w2_ref.shape[0]
    posb = _dot(pos_ref[...], w1_ref[...])[0:1, :]
    outs = []
    for g in range(A_KV_HEADS):
        p = pq_ref[:, 2 * g * hid:(2 * g + 1) * hid]
        q_next = pltpu.roll(pq_ref[:, (2 * g + 1) * hid:(2 * g + 2) * hid], nch - 1, 0)
        act = jax.nn.gelu(p + q_next + posb, approximate=True)
        outs.append(_dot(act.astype(BF16), w2_ref[...]))
    o_ref[...] = jnp.concatenate(outs, axis=1)


def _compress(pq, pos8, w1, w2, bsz):
    m, n = pq.shape
    nch = m // bsz
    dh = w2.shape[1]
    return pl.pallas_call(
        _compress_kernel,
        grid=(bsz,),
        in_specs=[
            pl.BlockSpec((nch, n), lambda b: (b, 0)),
            pl.BlockSpec(pos8.shape, lambda b: (0, 0)),
            pl.BlockSpec(w1.shape, lambda b: (0, 0)),
            pl.BlockSpec(w2.shape, lambda b: (0, 0)),
        ],
        out_specs=pl.BlockSpec((nch, A_KV_HEADS * dh), lambda b: (b, 0)),
        out_shape=jax.ShapeDtypeStruct((m, A_KV_HEADS * dh), F32),
        compiler_params=_cp("parallel"),
        name="compress",
    )(pq, pos8, w1, w2)


def _cmp_select_kernel(q_ref, kc_ref, vc_ref, mapt_ref, oc_ref, sb_ref, *, ncp, nslc):
    qi = pl.program_id(2)
    t0 = qi * TQA
    rows = A_GROUP * TQA
    qb = q_ref[...]
    q4 = jnp.concatenate([qb[:, r * LANES:(r + 1) * LANES] for r in range(A_GROUP)], axis=0)

    def branch(nb):
        nk = (nb + 1) * CMP_CHUNK
        nr = nk * CMP_STRIDE // SLC_BLOCK
        s = _dot_nt(q4, kc_ref[0:nk, :])
        t_tok = t0 + lax.broadcasted_iota(jnp.int32, (TQA, nk), 0)
        n_idx = lax.broadcasted_iota(jnp.int32, (TQA, nk), 1)
        valid = (n_idx * CMP_STRIDE + CMP_BLOCK - 1 <= t_tok) & (n_idx < ncp - 1)
        s3 = s.reshape(A_GROUP, TQA, nk) + jnp.where(valid, 0.0, NEG)[None]
        m = jnp.max(s3, axis=2, keepdims=True)
        p = jnp.exp2(s3 - m)
        l = jnp.sum(p, axis=2, keepdims=True)
        pn = p * jnp.where(m > 0.5 * NEG, 1.0 / l, 0.0)
        oc = _dot(pn.reshape(rows, nk).astype(BF16), vc_ref[0:nk, :])
        oc_ref[...] = oc[:, :HEAD_DIM].reshape(A_GROUP, TQA, HEAD_DIM)
        psum = pn[0] + pn[1] + pn[2] + pn[3]
        p_hi = psum.astype(BF16)
        p_lo = (psum - p_hi.astype(F32)).astype(BF16)
        mapt = mapt_ref[0:nr, 0:nk]
        imp = _dot_nt(mapt, p_hi) + _dot_nt(mapt, p_lo)

        j_idx = lax.broadcasted_iota(jnp.int32, (nr, TQA), 0)
        cur = (t0 + lax.broadcasted_iota(jnp.int32, (nr, TQA), 1)) // SLC_BLOCK
        forced = (j_idx == 0) | (j_idx == cur) | (j_idx == cur - 1)
        score = jnp.where(j_idx > cur, -1.0, jnp.where(forced, -jnp.inf, imp))
        for _ in range(N_SEL - N_FORCED):
            mx = jnp.max(score, axis=0, keepdims=True)
            first = jnp.min(jnp.where(score == mx, j_idx, nr), axis=0, keepdims=True)
            score = jnp.where(j_idx == first, -jnp.inf, score)
        sb_t = jnp.where((score == -jnp.inf) & (j_idx <= cur), 0.0, NEG)
        if nr < nslc:
            sb_t = jnp.concatenate([sb_t, jnp.full((nslc - nr, TQA), NEG, F32)], axis=0)
        sb_ref[...] = sb_t.T

    bucket = ((qi + 1) * (TQA // CMP_STRIDE) - 1) // CMP_CHUNK
    for nb in range(ncp // CMP_CHUNK):
        pl.when(bucket == nb)(functools.partial(branch, nb))


def _cmp_select(proj, kc_cmp, vc_cmp, cmp_map_t, bsz, seq):
    ncp = kc_cmp.shape[2]
    nslc = seq // SLC_BLOCK
    nq = seq // TQA
    g_ = A_KV_HEADS
    assert nslc >= N_SEL and ncp % CMP_CHUNK == 0
    return pl.pallas_call(
        functools.partial(_cmp_select_kernel, ncp=ncp, nslc=nslc),
        grid=(bsz, g_, nq),
        in_specs=[
            pl.BlockSpec((TQA, A_GROUP * LANES), lambda b, g, i: (b * nq + i, g)),
            pl.BlockSpec((None, None, ncp, LANES), lambda b, g, i: (b, g, 0, 0)),
            pl.BlockSpec((None, None, ncp, LANES), lambda b, g, i: (b, g, 0, 0)),
            pl.BlockSpec((nslc, ncp), lambda b, g, i: (0, 0)),
        ],
        out_specs=[
            pl.BlockSpec((None, None, A_GROUP, TQA, HEAD_DIM), lambda b, g, i: (b, g, 0, i, 0)),
            pl.BlockSpec((TQA, nslc), lambda b, g, i: (b * nq + i, g)),
        ],
        out_shape=[
            jax.ShapeDtypeStruct((bsz, g_, A_GROUP, seq, HEAD_DIM), F32),
            jax.ShapeDtypeStruct((bsz * seq, g_ * nslc), F32),
        ],
        compiler_params=_cp("parallel", "parallel", "arbitrary"),
        name="cmp_select",
    )(proj, kc_cmp, vc_cmp, cmp_map_t)


def _sel_win_kernel(q_ref, c_ref, s_ref, bias_ref, ks_ref, vs_ref, kw_ref, vw_ref, oc_ref, gate_ref,
                    o_ref, acc_ref, *, nslc):
    qi = pl.program_id(2)
    t0 = qi * TQA
    rows = A_GROUP * TQA
    qb = q_ref[...]
    q4 = jnp.concatenate([qb[:, r * LANES:(r + 1) * LANES] for r in range(A_GROUP)], axis=0).astype(F32)
    q_rot = _rope(q4, _tile_rows(c_ref[...], A_GROUP), _tile_rows(s_ref[...], A_GROUP)) * LOG2E
    q_rot_b = q_rot.astype(BF16)
    bias = bias_ref[...]
    lane = lax.broadcasted_iota(jnp.int32, (TQA, LANES), 1)
    bias_lanes = (lane >= HEAD_DIM) & (lane < HEAD_DIM + BLK_PER_KT)
    t_q = t0 + lax.broadcasted_iota(jnp.int32, (TQA, KT), 0)
    k_off = lax.broadcasted_iota(jnp.int32, (TQA, KT), 1)

    def add_mask(s, mask_bias):
        return (s.reshape(A_GROUP, TQA, -1) + mask_bias[None]).reshape(s.shape)

    def scores(kt):
        shift = (HEAD_DIM + nslc - BLK_PER_KT * kt) % nslc
        rolled = pltpu.roll(bias, shift, 1)[:, :LANES]
        qa = (q_rot + _tile_rows(jnp.where(bias_lanes, rolled, 0.0), A_GROUP)).astype(BF16)
        k0 = pl.multiple_of(kt * KT, KT)
        return _dot_nt(qa, ks_ref[pl.ds(k0, KT), :])

    def causal_bias(kt):
        return jnp.where(kt * KT + k_off <= t_q, 0.0, NEG)

    def group(kt0, n, carry, diagonal_last):
        m, acc = carry
        ss = [scores(kt0 + i) for i in range(n)]
        if diagonal_last:
            ss[-1] = add_mask(ss[-1], causal_bias(kt0 + n - 1))
        m_row = functools.reduce(jnp.maximum, [jnp.max(s, axis=1, keepdims=True) for s in ss])
        m_new = jnp.maximum(m, jnp.broadcast_to(m_row, (rows, LANES)))
        m_keys = _tile_lanes(m_new, KT // LANES)
        pv = None
        for i, s in enumerate(ss):
            k0 = pl.multiple_of((kt0 + i) * KT, KT)
            d = _dot(jnp.exp2(s - m_keys).astype(BF16), vs_ref[pl.ds(k0, KT), :])
            pv = d if pv is None else pv + d
        return m_new, jnp.exp2(m - m_new) * acc + pv

    kt_d = t0 // KT
    n_quads = kt_d // KT_GROUP
    carry = (jnp.full((rows, LANES), NEG, F32), jnp.zeros((rows, LANES), F32))
    carry = lax.fori_loop(0, n_quads, lambda j, c: group(j * KT_GROUP, KT_GROUP, c, False), carry)
    for rem in range(KT_GROUP):
        @pl.when(kt_d - n_quads * KT_GROUP == rem)
        def _(rem=rem):
            acc_ref[...] = group(kt_d - rem, rem + 1, carry, True)[1]
    o_s = _normalize(acc_ref[...])

    wspan = NSA_WIN + TQA
    w0 = pl.multiple_of(jnp.maximum(t0 - NSA_WIN, 0), TQA)
    s_w = _dot_nt(q_rot_b, kw_ref[pl.ds(w0, wspan), :])
    tw = t0 + lax.broadcasted_iota(jnp.int32, (TQA, wspan), 0)
    diff = tw - (w0 + lax.broadcasted_iota(jnp.int32, (TQA, wspan), 1))
    s_w = add_mask(s_w, jnp.where((diff >= 0) & (diff < NSA_WIN), 0.0, NEG))
    p_w = jnp.exp2(s_w - jnp.max(s_w, axis=1, keepdims=True))
    acc_w = _dot(p_w.astype(BF16), vw_ref[pl.ds(w0, wspan), :])
    o_w = _normalize(acc_w)

    gates = gate_ref[...]

    def gcol(c):
        return jnp.concatenate([gates[:, c * A_GROUP + r:c * A_GROUP + r + 1] for r in range(A_GROUP)], axis=0)

    o_c = oc_ref[...].reshape(rows, HEAD_DIM)
    o = gcol(0) * o_c + gcol(1) * o_s + gcol(2) * o_w
    o_ref[...] = _heads_to_lanes(o, A_GROUP).astype(o_ref.dtype)


def _sel_win(proj, tabs, selbias, o_c, gates, bsz, seq):
    nslc = seq // SLC_BLOCK
    nq = seq // TQA
    g_ = A_KV_HEADS
    c, s = tabs
    tab_spec = pl.BlockSpec((TQA, LANES), lambda b, g, i: (b * nq + i, 0))
    q_slabs = A_HEADS

    def kv_spec(base):
        return pl.BlockSpec((seq, LANES), lambda b, g, i: (b, base + g))

    return pl.pallas_call(
        functools.partial(_sel_win_kernel, nslc=nslc),
        grid=(bsz, g_, nq),
        in_specs=[
            pl.BlockSpec((TQA, A_GROUP * LANES), lambda b, g, i: (b * nq + i, g)),
            tab_spec, tab_spec,
            pl.BlockSpec((TQA, nslc), lambda b, g, i: (b * nq + i, g)),
            kv_spec(q_slabs), kv_spec(q_slabs + g_), kv_spec(q_slabs + 2 * g_), kv_spec(q_slabs + 3 * g_),
            pl.BlockSpec((None, None, A_GROUP, TQA, HEAD_DIM), lambda b, g, i: (b, g, 0, i, 0)),
            pl.BlockSpec((TQA, LANES), lambda b, g, i: (b * nq + i, g)),
        ],
        out_specs=pl.BlockSpec((TQA, A_GROUP * HEAD_DIM), lambda b, g, i: (b * nq + i, g)),
        out_shape=jax.ShapeDtypeStruct((bsz * seq, A_HEADS * HEAD_DIM), BF16),
        scratch_shapes=[pltpu.VMEM((A_GROUP * TQA, LANES), F32)],
        compiler_params=_cp("parallel", "parallel", "arbitrary"),
        name="sel_win",
    )(proj, c, s, selbias, proj, proj, proj, proj, o_c, gates)


def _swa_kernel(q_ref, k_ref, v_ref, sink_ref, o_ref):
    qi = pl.program_id(2)
    t0 = qi * TQB
    rows = B_GROUP * TQB
    span = SWA_WIN + TQB
    qb = q_ref[...]
    q8 = jnp.concatenate([qb[:, r * LANES:(r + 1) * LANES] for r in range(B_GROUP)], axis=0)
    w0 = pl.multiple_of(jnp.maximum(t0 - SWA_WIN, 0), SWA_WIN)
    s = _dot_nt(q8, k_ref[pl.ds(w0, span), :])
    tq = t0 + lax.broadcasted_iota(jnp.int32, (TQB, span), 0)
    diff = tq - (w0 + lax.broadcasted_iota(jnp.int32, (TQB, span), 1))
    band = jnp.where((diff >= 0) & (diff < SWA_WIN), 0.0, NEG)
    s = (s.reshape(B_GROUP, TQB, span) + band[None]).reshape(rows, span)
    sk = sink_ref[...]
    sink = jnp.concatenate([jnp.broadcast_to(sk[r:r + 1, :], (TQB, LANES)) for r in range(B_GROUP)], axis=0)
    m = jnp.maximum(jnp.broadcast_to(jnp.max(s, axis=1, keepdims=True), (rows, LANES)), sink)
    e = jnp.exp(s - _tile_lanes(m, span // LANES))
    acc = _dot(e.astype(BF16), v_ref[pl.ds(w0, span), :])
    o = _normalize(acc, jnp.exp(sink - m))
    o_ref[...] = _heads_to_lanes(o, B_GROUP).astype(o_ref.dtype)


def _swa(qproj, kvproj, sinks, bsz, seq):
    nq = seq // TQB
    g_ = B_KV_HEADS
    return pl.pallas_call(
        _swa_kernel,
        grid=(bsz, g_, nq),
        in_specs=[
            pl.BlockSpec((TQB, B_GROUP * LANES), lambda b, g, i: (b * nq + i, g)),
            pl.BlockSpec((seq, LANES), lambda b, g, i: (b, g)),
            pl.BlockSpec((seq, LANES), lambda b, g, i: (b, g_ + g)),
            pl.BlockSpec((None, B_GROUP, LANES), lambda b, g, i: (g, 0, 0)),
        ],
        out_specs=pl.BlockSpec((TQB, B_GROUP * HEAD_DIM), lambda b, g, i: (b * nq + i, g)),
        out_shape=jax.ShapeDtypeStruct((bsz * seq, B_HEADS * HEAD_DIM), BF16),
        compiler_params=_cp("parallel", "parallel", "arbitrary"),
        name="swa",
    )(qproj, kvproj, kvproj, sinks)


def _router_kernel(x_ref, wh_ref, wl_ref, o_ref):
    x = x_ref[...]
    xh = x.astype(BF16)
    xl = (x - xh.astype(F32)).astype(BF16)
    logits = _dot(xh, wh_ref[...]) + _dot(xh, wl_ref[...]) + _dot(xl, wh_ref[...])
    lane = lax.broadcasted_iota(jnp.int32, logits.shape, 1)
    lg = jnp.where(lane < N_EXPERTS, logits, -jnp.inf)
    m1 = jnp.max(lg, axis=1, keepdims=True)
    i1 = jnp.min(jnp.where(lg == m1, lane, LANES), axis=1, keepdims=True)
    lg2 = jnp.where(lane == i1, -jnp.inf, lg)
    m2 = jnp.max(lg2, axis=1, keepdims=True)
    i2 = jnp.min(jnp.where(lg2 == m2, lane, LANES), axis=1, keepdims=True)
    e2 = jnp.exp(m2 - m1)
    g1 = 1.0 / (1.0 + e2)
    g2 = e2 * g1
    out = jnp.where(lane == 0, i1.astype(F32), jnp.where(lane == 1, i2.astype(F32),
                    jnp.where(lane == 2, g1, jnp.where(lane == 3, g2, 0.0))))
    o_ref[...] = out


def _router(h, w_hi, w_lo, tm=512):
    m, d = h.shape
    return pl.pallas_call(
        _router_kernel,
        grid=(m // tm,),
        in_specs=[pl.BlockSpec((tm, d), lambda i: (i, 0)), pl.BlockSpec((d, LANES), lambda i: (0, 0)),
                  pl.BlockSpec((d, LANES), lambda i: (0, 0))],
        out_specs=pl.BlockSpec((tm, LANES), lambda i: (i, 0)),
        out_shape=jax.ShapeDtypeStruct((m, LANES), F32),
        compiler_params=_cp("parallel"),
        name="router",
    )(h, w_hi, w_lo)


def _moe_ffn_kernel(blk_e_ref, n_act_ref, x_ref, wg_ref, wu_ref, wo_ref, o_ref, acc_ref):
    i = pl.program_id(0)
    c = pl.program_id(1)

    @pl.when(c == 0)
    def _():
        acc_ref[...] = jnp.zeros_like(acc_ref)

    @pl.when(i < n_act_ref[0])
    def _():
        xb = x_ref[...]
        gate = _dot(xb, wg_ref[...])
        up = _dot(xb, wu_ref[...])
        a = gate * _sigmoid(gate) * up
        acc_ref[...] += _dot(a.astype(BF16), wo_ref[...])

    @pl.when(c == pl.num_programs(1) - 1)
    def _():
        o_ref[...] = acc_ref[...]


def _moe_ffn(xs, blk_e, n_act, w_in, w_out, tm=MOE_TM, fc=1792):
    cap, d = xs.shape
    ff = w_out.shape[1]
    nc = ff // fc

    def chunk(i, c, n_act_ref):
        return jnp.where(i < n_act_ref[0], c, nc - 1)

    grid_spec = pltpu.PrefetchScalarGridSpec(
        num_scalar_prefetch=2,
        grid=(cap // tm, nc),
        in_specs=[
            pl.BlockSpec((tm, d), lambda i, c, e, na: (i, 0)),
            pl.BlockSpec((None, d, fc), lambda i, c, e, na: (e[i], 0, chunk(i, c, na))),
            pl.BlockSpec((None, d, fc), lambda i, c, e, na: (e[i], 0, nc + chunk(i, c, na))),
            pl.BlockSpec((None, fc, d), lambda i, c, e, na: (e[i], chunk(i, c, na), 0)),
        ],
        out_specs=pl.BlockSpec((tm, d), lambda i, c, e, na: (i, 0)),
        scratch_shapes=[pltpu.VMEM((tm, d), F32)],
    )
    return pl.pallas_call(
        _moe_ffn_kernel,
        grid_spec=grid_spec,
        out_shape=jax.ShapeDtypeStruct((cap, d), F32),
        compiler_params=_cp("parallel", "arbitrary"),
        name="moe_ffn",
    )(blk_e, n_act, xs, w_in, w_in, w_out)


def _combine_ln_kernel(h_ref, y1_ref, y2_ref, r_ref, g_ref, b_ref, o_ref, ob_ref):
    r = r_ref[...]
    ffn = r[:, 2:3] * y1_ref[...] + r[:, 3:4] * y2_ref[...]
    y = _layer_norm(ALPHA * h_ref[...] + ffn, g_ref[...], b_ref[...])
    o_ref[...] = y
    ob_ref[...] = y.astype(BF16)


def _combine_ln(h, y1, y2, route, g, b, tm=512):
    m, d = h.shape
    row = pl.BlockSpec((tm, d), lambda i: (i, 0))
    vec = pl.BlockSpec((1, d), lambda i: (0, 0))
    out_specs, out_shape = _dual_out(m, d, tm)
    return pl.pallas_call(
        _combine_ln_kernel,
        grid=(m // tm,),
        in_specs=[row, row, row, pl.BlockSpec((tm, LANES), lambda i: (i, 0)), vec, vec],
        out_specs=out_specs,
        out_shape=out_shape,
        compiler_params=_cp("parallel"),
        name="combine_ln",
    )(h, y1, y2, route, g, b)


def _pad_heads(w, heads, rope_copy=False):
    d = w.shape[0]
    half = ROT_DIM // 2
    w = w.reshape(d, heads, HEAD_DIM)
    if rope_copy:
        fill = jnp.zeros((d, heads, LANES - HEAD_DIM - half), w.dtype)
        w = jnp.concatenate([w, fill, w[:, :, half:ROT_DIM]], axis=2)
    else:
        w = jnp.pad(w, ((0, 0), (0, 0), (0, LANES - HEAD_DIM)))
    return w.reshape(d, heads * LANES)


def _rope_tables(positions):
    half = ROT_DIM // 2
    inv = ROPE_THETA ** (-jnp.arange(0, ROT_DIM, 2, dtype=F32) / ROT_DIM)
    ang = positions.astype(F32).reshape(-1, 1) * inv
    cos, sin = jnp.cos(ang), jnp.sin(ang)
    n = ang.shape[0]
    c = jnp.concatenate([cos, cos, jnp.ones((n, LANES - ROT_DIM - half), F32), jnp.zeros((n, half), F32)], axis=1)
    s = jnp.concatenate([-sin, sin, jnp.zeros((n, LANES - ROT_DIM), F32)], axis=1)
    return c, s


def _cmp_to_slc(ncp, nslc):
    n = np.arange(ncp)[:, None]
    j = np.arange(nslc)[None, :]
    overlap = (np.minimum(n * CMP_STRIDE + CMP_BLOCK, j * SLC_BLOCK + SLC_BLOCK)
               - np.maximum(n * CMP_STRIDE, j * SLC_BLOCK))
    m = np.clip(overlap, 0, None).astype(np.float32) / CMP_BLOCK
    m[ncp - 1, :] = 0.0
    return jnp.asarray(m, BF16)


def _compress_branch(t, pos, w1, w2, bsz, seq):
    g_ = A_KV_HEADS
    nch = seq // CMP_STRIDE
    half = CMP_STRIDE * HEAD_DIM
    a = t.reshape(bsz * nch, CMP_STRIDE * g_ * HEAD_DIM)
    w1b = w1.astype(BF16)
    w1cat = jnp.concatenate([w1b[:half], w1b[half:]], axis=1)
    eye = jnp.eye(g_, dtype=BF16)
    w1blk = (w1cat.reshape(CMP_STRIDE, 1, HEAD_DIM, 1, 2 * CMP_HID) * eye[None, :, None, :, None])
    w1blk = w1blk.reshape(CMP_STRIDE * g_ * HEAD_DIM, g_ * 2 * CMP_HID)
    pq = _mm(a, w1blk, F32, tm=256, tn=2 * CMP_HID)
    pos8 = jnp.zeros((8, CMP_BLOCK * HEAD_DIM), BF16).at[0].set(pos.reshape(-1).astype(BF16))
    out = _compress(pq, pos8, w1b, w2.astype(BF16), bsz)
    return out.reshape(bsz, nch, g_, HEAD_DIM).transpose(0, 2, 1, 3)


def _nsa_attn(hb, tabs, w_in, cmp_pos, cmp_w1, cmp_w2, bsz, seq):
    hb = hb.astype(BF16)
    g_ = A_KV_HEADS
    aq = A_HEADS * HEAD_DIM
    akv = g_ * HEAD_DIM
    wq = w_in[:, :aq] * (HEAD_DIM ** -0.5)
    w_kc, w_vc, w_ks, w_vs, w_kw, w_vw = (w_in[:, aq + i * akv: aq + (i + 1) * akv] for i in range(6))
    w_gl = w_in[:, aq + 6 * akv:]
    w_big = jnp.concatenate([_pad_heads(wq, A_HEADS, True), _pad_heads(w_ks, g_, True), _pad_heads(w_vs, g_),
                             _pad_heads(w_kw, g_, True), _pad_heads(w_vw, g_)], axis=1).astype(BF16)
    slab_modes = ["plain"] * A_HEADS + ["rope_onehot"] * g_ + ["ones"] * g_ + ["rope"] * g_ + ["ones"] * g_
    proj = _proj(hb, w_big, tabs, slab_modes)

    w_gl_g = w_gl.reshape(-1, 3, g_, A_GROUP).transpose(0, 2, 1, 3).reshape(-1, g_, 3 * A_GROUP)
    w_gl_g = jnp.pad(w_gl_g, ((0, 0), (0, 0), (0, LANES - 3 * A_GROUP))).reshape(-1, g_ * LANES)
    w_side = jnp.concatenate([w_gl_g, w_kc, w_vc], axis=1).astype(BF16)
    gates, kc_in, vc_in = _nsa_side(hb, w_side, g_ * LANES, akv)
    kc = _compress_branch(kc_in, cmp_pos[0], cmp_w1[0], cmp_w2[0], bsz, seq)
    vc = _compress_branch(vc_in, cmp_pos[1], cmp_w1[1], cmp_w2[1], bsz, seq)
    pad = ((0, 0), (0, 0), (0, 0), (0, LANES - HEAD_DIM))
    kc = jnp.pad(kc * LOG2E, pad).astype(BF16)
    vc = jnp.pad(vc, pad).astype(BF16)
    ncp = seq // CMP_STRIDE
    cmp_map_t = _cmp_to_slc(ncp, seq // SLC_BLOCK).T

    o_c, selbias = _cmp_select(proj, kc, vc, cmp_map_t, bsz, seq)
    o = _sel_win(proj, tabs, selbias, o_c, gates, bsz, seq)
    return o, selbias, o_c


def _shared_kv(hb, tabs, w_kv):
    g_ = B_KV_HEADS
    bkv = g_ * HEAD_DIM
    w = jnp.concatenate([_pad_heads(w_kv[:, :bkv], g_, True), _pad_heads(w_kv[:, bkv:], g_)], axis=1).astype(BF16)
    return _proj(hb.astype(BF16), w, tabs, ["rope"] * g_ + ["ones"] * g_)


def _swa_attn(hb, tabs, kvproj, w_q, sinks, bsz, seq):
    wq = _pad_heads(w_q * (HEAD_DIM ** -0.5), B_HEADS, True).astype(BF16)
    qproj = _proj(hb.astype(BF16), wq, tabs, ["rope"] * B_HEADS)
    sk = jnp.broadcast_to(sinks.astype(F32).reshape(B_KV_HEADS, B_GROUP, 1), (B_KV_HEADS, B_GROUP, LANES))
    return _swa(qproj, kvproj, sk, bsz, seq)


def _moe_layer(h, hb, w_router, w_in_all, w_out_all, layer, ln_g, ln_b):
    y1, y2, route = _moe_experts(h, hb, w_router, w_in_all, w_out_all, layer)
    return _combine_ln(h, y1, y2, route, ln_g[None, :], ln_b[None, :])


def _moe_ffn_out(h, w_router, w_in_all, w_out_all, layer):
    y1, y2, route = _moe_experts(h, h.astype(BF16), w_router, w_in_all, w_out_all, layer)
    return route[:, 2:3] * y1 + route[:, 3:4] * y2


def _moe_experts(h, hb, w_router, w_in_all, w_out_all, layer):
    n_tok, d = h.shape
    wr = jnp.pad(w_router, ((0, 0), (0, LANES - N_EXPERTS)))
    wr_hi = wr.astype(BF16)
    wr_lo = (wr - wr_hi.astype(F32)).astype(BF16)
    route = _router(h, wr_hi, wr_lo)
    top_e = route[:, :2].astype(jnp.int32)

    e_flat = top_e.reshape(-1)
    onehot = (e_flat[:, None] == jnp.arange(N_EXPERTS)[None, :]).astype(jnp.int32)
    csum = jnp.cumsum(onehot, axis=0)
    counts = csum[-1]
    rank = jnp.take_along_axis(csum, e_flat[:, None], axis=1)[:, 0] - 1
    padded = (counts + MOE_TM - 1) // MOE_TM * MOE_TM
    pad_end = jnp.cumsum(padded)
    pad_start = pad_end - padded
    dest = pad_start[e_flat] + rank
    cap = n_tok * 2 + N_EXPERTS * MOE_TM
    n_blk = cap // MOE_TM
    tok_flat = jnp.repeat(jnp.arange(n_tok, dtype=jnp.int32), 2)
    buf_tok = jnp.zeros((cap,), jnp.int32).at[dest].set(tok_flat)
    blk_start = jnp.arange(n_blk, dtype=jnp.int32) * MOE_TM
    blk_e = jnp.sum((pad_end[None, :] <= blk_start[:, None]).astype(jnp.int32), axis=1)
    blk_e = jnp.minimum(blk_e, N_EXPERTS - 1).astype(jnp.int32)

    xs = hb[buf_tok]
    n_act = (pad_end[-1:] // MOE_TM).astype(jnp.int32)
    y = _moe_ffn(xs, blk_e, n_act, w_in_all[layer].astype(BF16), w_out_all[layer].astype(BF16))
    dest2 = dest.reshape(n_tok, 2)
    y1 = y[dest2[:, 0]]
    y2 = y[dest2[:, 1]]
    return y1, y2, route


def kernel(x, positions, w_in_a, w_out_a, cmp_pos, cmp_w1, cmp_w2, w_kv_shared, w_q_b, w_out_b, sinks_b,
           ln_g, ln_b, dense_w_in, dense_w_out, moe_router, moe_w_in, moe_w_out):
    bsz, seq, d = x.shape
    n_a = DEPTH // 2
    tabs = _rope_tables(positions)
    h = x.reshape(bsz * seq, d)
    hb = h.astype(BF16)
    kvproj = None
    for l in range(DEPTH):
        if l < n_a:
            o = _nsa_attn(hb, tabs, w_in_a[l], cmp_pos[l], cmp_w1[l], cmp_w2[l], bsz, seq)[0]
            w_attn = w_out_a[l].astype(BF16)
        else:
            b = l - n_a
            o = _swa_attn(hb, tabs, kvproj, w_q_b[b], sinks_b[b], bsz, seq)
            w_attn = w_out_b[b].astype(BF16)
        if l % 2 == 0:
            h, hb = _mix_ffn(o, w_attn, h, ln_g[l, 0][None, :], ln_b[l, 0][None, :],
                             dense_w_in[l // 2].astype(BF16), dense_w_out[l // 2].astype(BF16),
                             ln_g[l, 1][None, :], ln_b[l, 1][None, :])
        else:
            h, hb = _out_ln(o, w_attn, h, ln_g[l, 0][None, :], ln_b[l, 0][None, :])
            h, hb = _moe_layer(h, hb, moe_router[l // 2], moe_w_in, moe_w_out, l // 2,
                               ln_g[l, 1], ln_b[l, 1])
        if l == n_a - 1:
            kvproj = _shared_kv(hb, tabs, w_kv_shared)
    return h.reshape(bsz, seq, d)
```

```python
import functools

import numpy as np
import jax
import jax.numpy as jnp
from jax import lax
from jax.experimental import pallas as pl
from jax.experimental.pallas import tpu as pltpu

F32 = jnp.float32
BF16 = jnp.bfloat16

D_MODEL = 1024
DEPTH = 4
HEAD_DIM = 64
LANES = 128
ROT_DIM = HEAD_DIM // 4
ROPE_THETA = 500000.0
A_HEADS = 16
A_KV_HEADS = 4
A_GROUP = 4
CMP_BLOCK = 32
CMP_STRIDE = 16
CMP_HID = 256
SLC_BLOCK = 64
N_SEL = 16
N_FORCED = 3
NSA_WIN = 512
B_HEADS = 16
B_KV_HEADS = 2
B_GROUP = 8
SWA_WIN = 128
D_FF = 2816
N_EXPERTS = 8
D_FF_EXPERT = 3584
ALPHA = (2 * DEPTH) ** 0.25
LN_EPS = 1e-5
NEG = -1e30
FORCE = 1e9

TQB = 256
TQA = 256
CMP_CHUNK = 256
LOG2E = 1.4426950408889634
KT = 512
BLK_PER_KT = KT // SLC_BLOCK
KT_GROUP = 4
MOE_TM = 512
VMEM_LIMIT = 56 * 1024 * 1024


def _cp(*sem):
    return pltpu.CompilerParams(dimension_semantics=sem, vmem_limit_bytes=VMEM_LIMIT)


def _dot(a, b):
    return jnp.dot(a, b, preferred_element_type=F32)


def _dot_nt(a, b):
    return lax.dot_general(a, b, (((1,), (1,)), ((), ())), preferred_element_type=F32)


def _sigmoid(x):
    return 1.0 / (1.0 + jnp.exp(-x))


def _layer_norm(z, g, b):
    mu = jnp.mean(z, axis=-1, keepdims=True)
    zc = z - mu
    var = jnp.mean(zc * zc, axis=-1, keepdims=True)
    return zc * lax.rsqrt(var + LN_EPS) * g + b


def _rope(x, c, s):
    half = ROT_DIM // 2
    ns = x.shape[1] // LANES
    rolled = [pltpu.roll(x[:, i * LANES:(i + 1) * LANES], half, 1) for i in range(ns)]
    z = rolled[0] if ns == 1 else jnp.concatenate(rolled, axis=1)
    return x * c + z * s


def _tile_lanes(t, n):
    return t if n == 1 else jnp.concatenate([t] * n, axis=1)


def _tile_rows(t, n):
    return t if n == 1 else jnp.concatenate([t] * n, axis=0)


def _normalize(acc, extra=0.0):
    den = pltpu.roll(acc, HEAD_DIM, 1) + extra
    return (acc / den)[:, :HEAD_DIM]


def _heads_to_lanes(o, heads):
    t = o.shape[0] // heads
    return jnp.concatenate([o[r * t:(r + 1) * t] for r in range(heads)], axis=1)


def _mm_kernel(x_ref, w_ref, o_ref):
    o_ref[...] = _dot(x_ref[...].astype(BF16), w_ref[...]).astype(o_ref.dtype)


def _mm(x, w, out_dtype, tm=512, tn=None):
    m, k = x.shape
    n = w.shape[1]
    tm = min(tm, m)
    tn = n if tn is None else tn
    return pl.pallas_call(
        _mm_kernel,
        grid=(m // tm, n // tn),
        in_specs=[pl.BlockSpec((tm, k), lambda i, j: (i, 0)), pl.BlockSpec((k, tn), lambda i, j: (0, j))],
        out_specs=pl.BlockSpec((tm, tn), lambda i, j: (i, j)),
        out_shape=jax.ShapeDtypeStruct((m, n), out_dtype),
        compiler_params=_cp("parallel", "arbitrary"),
        name="mm",
    )(x, w)


def _nsa_side_kernel(x_ref, w_ref, gate_ref, kc_ref, vc_ref):
    acc = _dot(x_ref[...], w_ref[...])
    ng = gate_ref.shape[1]
    nk = kc_ref.shape[1]
    gate_ref[...] = _sigmoid(acc[:, :ng])
    kc_ref[...] = acc[:, ng:ng + nk]
    vc_ref[...] = acc[:, ng + nk:]


def _nsa_side(hb, w, n_gate, n_kv, tm=512):
    m, k = hb.shape
    row = lambda n: pl.BlockSpec((tm, n), lambda i: (i, 0))
    return pl.pallas_call(
        _nsa_side_kernel,
        grid=(m // tm,),
        in_specs=[row(k), pl.BlockSpec(w.shape, lambda i: (0, 0))],
        out_specs=[row(n_gate), row(n_kv), row(n_kv)],
        out_shape=[jax.ShapeDtypeStruct((m, n_gate), F32), jax.ShapeDtypeStruct((m, n_kv), F32),
                   jax.ShapeDtypeStruct((m, n_kv), F32)],
        compiler_params=_cp("parallel"),
        name="nsa_side",
    )(hb, w)


def _cast_kernel(x_ref, o_ref):
    o_ref[...] = x_ref[...].astype(o_ref.dtype)


def _cast_layer_bf16(w, layer, tr):
    _, e, r, c = w.shape
    return pl.pallas_call(
        _cast_kernel,
        grid=(e, r // tr),
        in_specs=[pl.BlockSpec((None, None, tr, c), lambda ei, i: (layer, ei, i, 0))],
        out_specs=pl.BlockSpec((None, tr, c), lambda ei, i: (ei, i, 0)),
        out_shape=jax.ShapeDtypeStruct((e, r, c), BF16),
        compiler_params=_cp("parallel", "arbitrary"),
        name="cast_bf16",
    )(w)


def _proj_kernel(x_ref, w_ref, c_ref, s_ref, o_ref, *, block_modes, tm, tn):
    i = pl.program_id(0)
    j = pl.program_id(1)
    acc = _dot(x_ref[...], w_ref[...])

    def piece(mode, s0, ns):
        a = acc[:, s0 * LANES:(s0 + ns) * LANES]
        lane = lax.broadcasted_iota(jnp.int32, a.shape, 1) % LANES
        if mode in ("rope", "rope_onehot"):
            a = _rope(a, _tile_lanes(c_ref[...], ns), _tile_lanes(s_ref[...], ns))
        if mode == "rope_onehot":
            row = i * tm + lax.broadcasted_iota(jnp.int32, a.shape, 0)
            a = a + jnp.where(lane == HEAD_DIM + (row // SLC_BLOCK) % BLK_PER_KT, 1.0, 0.0)
        if mode == "ones":
            a = a + jnp.where(lane >= HEAD_DIM, 1.0, 0.0)
        return a.astype(o_ref.dtype)

    for modes in sorted(set(block_modes)):
        pred = functools.reduce(jnp.logical_or, [j == jj for jj, mm in enumerate(block_modes) if mm == modes])

        @pl.when(pred)
        def _(modes=modes):
            runs = []
            for s, mode in enumerate(modes):
                if runs and runs[-1][0] == mode:
                    runs[-1][2] += 1
                else:
                    runs.append([mode, s, 1])
            pieces = [piece(*r) for r in runs]
            o_ref[...] = pieces[0] if len(pieces) == 1 else jnp.concatenate(pieces, axis=1)


def _proj(x, w, tabs, slab_modes, tn=1024, tm=1024):
    m, k = x.shape
    n = w.shape[1]
    tn = min(tn, n)
    spb = tn // LANES
    block_modes = tuple(tuple(slab_modes[jj * spb:(jj + 1) * spb]) for jj in range(n // tn))
    c, s = tabs
    return pl.pallas_call(
        functools.partial(_proj_kernel, block_modes=block_modes, tm=tm, tn=tn),
        grid=(m // tm, n // tn),
        in_specs=[
            pl.BlockSpec((tm, k), lambda i, j: (i, 0)),
            pl.BlockSpec((k, tn), lambda i, j: (0, j)),
            pl.BlockSpec((tm, LANES), lambda i, j: (i, 0)),
            pl.BlockSpec((tm, LANES), lambda i, j: (i, 0)),
        ],
        out_specs=pl.BlockSpec((tm, tn), lambda i, j: (i, j)),
        out_shape=jax.ShapeDtypeStruct((m, n), BF16),
        compiler_params=_cp("parallel", "arbitrary"),
        name="proj",
    )(x, w, c, s)


def _out_ln_kernel(x_ref, w_ref, h_ref, g_ref, b_ref, o_ref, ob_ref):
    mix = _dot(x_ref[...], w_ref[...])
    y = _layer_norm(ALPHA * h_ref[...] + mix, g_ref[...], b_ref[...])
    o_ref[...] = y
    ob_ref[...] = y.astype(BF16)


def _dual_out(m, d, tm):
    spec = pl.BlockSpec((tm, d), lambda i, *_: (i, 0))
    return [spec, spec], [jax.ShapeDtypeStruct((m, d), F32), jax.ShapeDtypeStruct((m, d), BF16)]


def _out_ln(x, w, h, g, b, tm=512):
    m, k = x.shape
    d = w.shape[1]
    out_specs, out_shape = _dual_out(m, d, tm)
    return pl.pallas_call(
        _out_ln_kernel,
        grid=(m // tm,),
        in_specs=[
            pl.BlockSpec((tm, k), lambda i: (i, 0)),
            pl.BlockSpec((k, d), lambda i: (0, 0)),
            pl.BlockSpec((tm, d), lambda i: (i, 0)),
            pl.BlockSpec((1, d), lambda i: (0, 0)),
            pl.BlockSpec((1, d), lambda i: (0, 0)),
        ],
        out_specs=out_specs,
        out_shape=out_shape,
        compiler_params=_cp("parallel"),
        name="out_ln",
    )(x, w, h, g, b)


def _mix_ffn_kernel(o_ref, wa_ref, h_ref, g1_ref, b1_ref, wg_ref, wu_ref, wo_ref, g2_ref, b2_ref, y_ref, yb_ref):
    h1 = _layer_norm(ALPHA * h_ref[...] + _dot(o_ref[...], wa_ref[...]), g1_ref[...], b1_ref[...])
    xb = h1.astype(BF16)
    gate = _dot(xb, wg_ref[...])
    up = _dot(xb, wu_ref[...])
    a = gate * _sigmoid(gate) * up
    y = _layer_norm(ALPHA * h1 + _dot(a.astype(BF16), wo_ref[...]), g2_ref[...], b2_ref[...])
    y_ref[...] = y
    yb_ref[...] = y.astype(BF16)


def _mix_ffn(o, w_attn, h, g1, b1, w_in, w_out, g2, b2, tm=512):
    m, d = h.shape
    ff = w_out.shape[0]
    once = pl.Buffered(1)
    row = pl.BlockSpec((tm, d), lambda i: (i, 0))
    vec = pl.BlockSpec((1, d), lambda i: (0, 0))
    out_specs, out_shape = _dual_out(m, d, tm)
    return pl.pallas_call(
        _mix_ffn_kernel,
        grid=(m // tm,),
        in_specs=[
            pl.BlockSpec((tm, o.shape[1]), lambda i: (i, 0)),
            pl.BlockSpec(w_attn.shape, lambda i: (0, 0), pipeline_mode=once),
            row, vec, vec,
            pl.BlockSpec((d, ff), lambda i: (0, 0), pipeline_mode=once),
            pl.BlockSpec((d, ff), lambda i: (0, 1), pipeline_mode=once),
            pl.BlockSpec((ff, d), lambda i: (0, 0), pipeline_mode=once),
            vec, vec,
        ],
        out_specs=out_specs,
        out_shape=out_shape,
        compiler_params=_cp("parallel"),
        name="mix_ffn",
    )(o, w_attn, h, g1, b1, w_in, w_in, w_out, g2, b2)


def _compress_kernel(pq_ref, pos_ref, w1_ref, w2_ref, o_ref):
    nch = pq_ref.shape[0]
    hid = w2_ref.shape[0]
    posb = _dot(pos_ref[...], w1_ref[...])[0:1, :]
    outs = []
    for g in range(A_KV_HEADS):
        p = pq_ref[:, 2 * g * hid:(2 * g + 1) * hid]
        q_next = pltpu.roll(pq_ref[:, (2 * g + 1) * hid:(2 * g + 2) * hid], nch - 1, 0)
        act = jax.nn.gelu(p + q_next + posb, approximate=True)
        outs.append(_dot(act.astype(BF16), w2_ref[...]))
    o_ref[...] = jnp.concatenate(outs, axis=1)


def _compress(pq, pos8, w1, w2, bsz):
    m, n = pq.shape
    nch = m // bsz
    dh = w2.shape[1]
    return pl.pallas_call(
        _compress_kernel,
        grid=(bsz,),
        in_specs=[
            pl.BlockSpec((nch, n), lambda b: (b, 0)),
            pl.BlockSpec(pos8.shape, lambda b: (0, 0)),
            pl.BlockSpec(w1.shape, lambda b: (0, 0)),
            pl.BlockSpec(w2.shape, lambda b: (0, 0)),
        ],
        out_specs=pl.BlockSpec((nch, A_KV_HEADS * dh), lambda b: (b, 0)),
        out_shape=jax.ShapeDtypeStruct((m, A_KV_HEADS * dh), F32),
        compiler_params=_cp("parallel"),
        name="compress",
    )(pq, pos8, w1, w2)


def _cmp_select_kernel(q_ref, kc_ref, vc_ref, mapt_ref, oc_ref, sb_ref, *, ncp, nslc):
    qi = pl.program_id(2)
    t0 = qi * TQA
    rows = A_GROUP * TQA
    qb = q_ref[...]
    q4 = jnp.concatenate([qb[:, r * LANES:(r + 1) * LANES] for r in range(A_GROUP)], axis=0)

    def branch(nb):
        nk = (nb + 1) * CMP_CHUNK
        nr = nk * CMP_STRIDE // SLC_BLOCK
        s = _dot_nt(q4, kc_ref[0:nk, :])
        t_tok = t0 + lax.broadcasted_iota(jnp.int32, (TQA, nk), 0)
        n_idx = lax.broadcasted_iota(jnp.int32, (TQA, nk), 1)
        valid = (n_idx * CMP_STRIDE + CMP_BLOCK - 1 <= t_tok) & (n_idx < ncp - 1)
        s3 = s.reshape(A_GROUP, TQA, nk) + jnp.where(valid, 0.0, NEG)[None]
        m = jnp.max(s3, axis=2, keepdims=True)
        p = jnp.exp2(s3 - m)
        l = jnp.sum(p, axis=2, keepdims=True)
        pn = p * jnp.where(m > 0.5 * NEG, 1.0 / l, 0.0)
        oc = _dot(pn.reshape(rows, nk).astype(BF16), vc_ref[0:nk, :])
        oc_ref[...] = oc[:, :HEAD_DIM].reshape(A_GROUP, TQA, HEAD_DIM)
        psum = pn[0] + pn[1] + pn[2] + pn[3]
        p_hi = psum.astype(BF16)
        p_lo = (psum - p_hi.astype(F32)).astype(BF16)
        mapt = mapt_ref[0:nr, 0:nk]
        imp = _dot_nt(mapt, p_hi) + _dot_nt(mapt, p_lo)

        j_idx = lax.broadcasted_iota(jnp.int32, (nr, TQA), 0)
        cur = (t0 + lax.broadcasted_iota(jnp.int32, (nr, TQA), 1)) // SLC_BLOCK
        forced = (j_idx == 0) | (j_idx == cur) | (j_idx == cur - 1)
        score = jnp.where(j_idx > cur, -1.0, jnp.where(forced, -jnp.inf, imp))
        for _ in range(N_SEL - N_FORCED):
            mx = jnp.max(score, axis=0, keepdims=True)
            first = jnp.min(jnp.where(score == mx, j_idx, nr), axis=0, keepdims=True)
            score = jnp.where(j_idx == first, -jnp.inf, score)
        sb_t = jnp.where((score == -jnp.inf) & (j_idx <= cur), 0.0, NEG)
        if nr < nslc:
            sb_t = jnp.concatenate([sb_t, jnp.full((nslc - nr, TQA), NEG, F32)], axis=0)
        sb_ref[...] = sb_t.T

    bucket = ((qi + 1) * (TQA // CMP_STRIDE) - 1) // CMP_CHUNK
    for nb in range(ncp // CMP_CHUNK):
        pl.when(bucket == nb)(functools.partial(branch, nb))


def _cmp_select(proj, kc_cmp, vc_cmp, cmp_map_t, bsz, seq):
    ncp = kc_cmp.shape[2]
    nslc = seq // SLC_BLOCK
    nq = seq // TQA
    g_ = A_KV_HEADS
    assert nslc >= N_SEL and ncp % CMP_CHUNK == 0
    return pl.pallas_call(
        functools.partial(_cmp_select_kernel, ncp=ncp, nslc=nslc),
        grid=(bsz, g_, nq),
        in_specs=[
            pl.BlockSpec((TQA, A_GROUP * LANES), lambda b, g, i: (b * nq + i, g)),
            pl.BlockSpec((None, None, ncp, LANES), lambda b, g, i: (b, g, 0, 0)),
            pl.BlockSpec((None, None, ncp, LANES), lambda b, g, i: (b, g, 0, 0)),
            pl.BlockSpec((nslc, ncp), lambda b, g, i: (0, 0)),
        ],
        out_specs=[
            pl.BlockSpec((None, None, A_GROUP, TQA, HEAD_DIM), lambda b, g, i: (b, g, 0, i, 0)),
            pl.BlockSpec((TQA, nslc), lambda b, g, i: (b * nq + i, g)),
        ],
        out_shape=[
            jax.ShapeDtypeStruct((bsz, g_, A_GROUP, seq, HEAD_DIM), F32),
            jax.ShapeDtypeStruct((bsz * seq, g_ * nslc), F32),
        ],
        compiler_params=_cp("parallel", "parallel", "arbitrary"),
        name="cmp_select",
    )(proj, kc_cmp, vc_cmp, cmp_map_t)


def _sel_win_kernel(q_ref, c_ref, s_ref, bias_ref, ks_ref, vs_ref, kw_ref, vw_ref, oc_ref, gate_ref,
                    o_ref, acc_ref, *, nslc):
    qi = pl.program_id(2)
    t0 = qi * TQA
    rows = A_GROUP * TQA
    qb = q_ref[...]
    q4 = jnp.concatenate([qb[:, r * LANES:(r + 1) * LANES] for r in range(A_GROUP)], axis=0).astype(F32)
    q_rot = _rope(q4, _tile_rows(c_ref[...], A_GROUP), _tile_rows(s_ref[...], A_GROUP)) * LOG2E
    q_rot_b = q_rot.astype(BF16)
    bias = bias_ref[...]
    lane = lax.broadcasted_iota(jnp.int32, (TQA, LANES), 1)
    bias_lanes = (lane >= HEAD_DIM) & (lane < HEAD_DIM + BLK_PER_KT)
    t_q = t0 + lax.broadcasted_iota(jnp.int32, (TQA, KT), 0)
    k_off = lax.broadcasted_iota(jnp.int32, (TQA, KT), 1)

    def add_mask(s, mask_bias):
        return (s.reshape(A_GROUP, TQA, -1) + mask_bias[None]).reshape(s.shape)

    def scores(kt):
        shift = (HEAD_DIM + nslc - BLK_PER_KT * kt) % nslc
        rolled = pltpu.roll(bias, shift, 1)[:, :LANES]
        qa = (q_rot + _tile_rows(jnp.where(bias_lanes, rolled, 0.0), A_GROUP)).astype(BF16)
        k0 = pl.multiple_of(kt * KT, KT)
        return _dot_nt(qa, ks_ref[pl.ds(k0, KT), :])

    def causal_bias(kt):
        return jnp.where(kt * KT + k_off <= t_q, 0.0, NEG)

    def group(kt0, n, carry, diagonal_last):
        m, acc = carry
        ss = [scores(kt0 + i) for i in range(n)]
        if diagonal_last:
            ss[-1] = add_mask(ss[-1], causal_bias(kt0 + n - 1))
        m_row = functools.reduce(jnp.maximum, [jnp.max(s, axis=1, keepdims=True) for s in ss])
        m_new = jnp.maximum(m, jnp.broadcast_to(m_row, (rows, LANES)))
        m_keys = _tile_lanes(m_new, KT // LANES)
        pv = None
        for i, s in enumerate(ss):
            k0 = pl.multiple_of((kt0 + i) * KT, KT)
            d = _dot(jnp.exp2(s - m_keys).astype(BF16), vs_ref[pl.ds(k0, KT), :])
            pv = d if pv is None else pv + d
        return m_new, jnp.exp2(m - m_new) * acc + pv

    kt_d = t0 // KT
    n_quads = kt_d // KT_GROUP
    carry = (jnp.full((rows, LANES), NEG, F32), jnp.zeros((rows, LANES), F32))
    carry = lax.fori_loop(0, n_quads, lambda j, c: group(j * KT_GROUP, KT_GROUP, c, False), carry)
    for rem in range(KT_GROUP):
        @pl.when(kt_d - n_quads * KT_GROUP == rem)
        def _(rem=rem):
            acc_ref[...] = group(kt_d - rem, rem + 1, carry, True)[1]
    o_s = _normalize(acc_ref[...])

    wspan = NSA_WIN + TQA
    w0 = pl.multiple_of(jnp.maximum(t0 - NSA_WIN, 0), TQA)
    s_w = _dot_nt(q_rot_b, kw_ref[pl.ds(w0, wspan), :])
    tw = t0 + lax.broadcasted_iota(jnp.int32, (TQA, wspan), 0)
    diff = tw - (w0 + lax.broadcasted_iota(jnp.int32, (TQA, wspan), 1))
    s_w = add_mask(s_w, jnp.where((diff >= 0) & (diff < NSA_WIN), 0.0, NEG))
    p_w = jnp.exp2(s_w - jnp.max(s_w, axis=1, keepdims=True))
    acc_w = _dot(p_w.astype(BF16), vw_ref[pl.ds(w0, wspan), :])
    o_w = _normalize(acc_w)

    gates = gate_ref[...]

    def gcol(c):
        return jnp.concatenate([gates[:, c * A_GROUP + r:c * A_GROUP + r + 1] for r in range(A_GROUP)], axis=0)

    o_c = oc_ref[...].reshape(rows, HEAD_DIM)
    o = gcol(0) * o_c + gcol(1) * o_s + gcol(2) * o_w
    o_ref[...] = _heads_to_lanes(o, A_GROUP).astype(o_ref.dtype)


def _sel_win(proj, tabs, selbias, o_c, gates, bsz, seq):
    nslc = seq // SLC_BLOCK
    nq = seq // TQA
    g_ = A_KV_HEADS
    c, s = tabs
    tab_spec = pl.BlockSpec((TQA, LANES), lambda b, g, i: (b * nq + i, 0))
    q_slabs = A_HEADS

    def kv_spec(base):
        return pl.BlockSpec((seq, LANES), lambda b, g, i: (b, base + g))

    return pl.pallas_call(
        functools.partial(_sel_win_kernel, nslc=nslc),
        grid=(bsz, g_, nq),
        in_specs=[
            pl.BlockSpec((TQA, A_GROUP * LANES), lambda b, g, i: (b * nq + i, g)),
            tab_spec, tab_spec,
            pl.BlockSpec((TQA, nslc), lambda b, g, i: (b * nq + i, g)),
            kv_spec(q_slabs), kv_spec(q_slabs + g_), kv_spec(q_slabs + 2 * g_), kv_spec(q_slabs + 3 * g_),
            pl.BlockSpec((None, None, A_GROUP, TQA, HEAD_DIM), lambda b, g, i: (b, g, 0, i, 0)),
            pl.BlockSpec((TQA, LANES), lambda b, g, i: (b * nq + i, g)),
        ],
        out_specs=pl.BlockSpec((TQA, A_GROUP * HEAD_DIM), lambda b, g, i: (b * nq + i, g)),
        out_shape=jax.ShapeDtypeStruct((bsz * seq, A_HEADS * HEAD_DIM), BF16),
        scratch_shapes=[pltpu.VMEM((A_GROUP * TQA, LANES), F32)],
        compiler_params=_cp("parallel", "parallel", "arbitrary"),
        name="sel_win",
    )(proj, c, s, selbias, proj, proj, proj, proj, o_c, gates)


def _swa_kernel(q_ref, k_ref, v_ref, sink_ref, o_ref):
    qi = pl.program_id(2)
    t0 = qi * TQB
    rows = B_GROUP * TQB
    span = SWA_WIN + TQB
    qb = q_ref[...]
    q8 = jnp.concatenate([qb[:, r * LANES:(r + 1) * LANES] for r in range(B_GROUP)], axis=0)
    w0 = pl.multiple_of(jnp.maximum(t0 - SWA_WIN, 0), SWA_WIN)
    s = _dot_nt(q8, k_ref[pl.ds(w0, span), :])
    tq = t0 + lax.broadcasted_iota(jnp.int32, (TQB, span), 0)
    diff = tq - (w0 + lax.broadcasted_iota(jnp.int32, (TQB, span), 1))
    band = jnp.where((diff >= 0) & (diff < SWA_WIN), 0.0, NEG)
    s = (s.reshape(B_GROUP, TQB, span) + band[None]).reshape(rows, span)
    sk = sink_ref[...]
    sink = jnp.concatenate([jnp.broadcast_to(sk[r:r + 1, :], (TQB, LANES)) for r in range(B_GROUP)], axis=0)
    m = jnp.maximum(jnp.broadcast_to(jnp.max(s, axis=1, keepdims=True), (rows, LANES)), sink)
    e = jnp.exp(s - _tile_lanes(m, span // LANES))
    acc = _dot(e.astype(BF16), v_ref[pl.ds(w0, span), :])
    o = _normalize(acc, jnp.exp(sink - m))
    o_ref[...] = _heads_to_lanes(o, B_GROUP).astype(o_ref.dtype)


def _swa(qproj, kvproj, sinks, bsz, seq):
    nq = seq // TQB
    g_ = B_KV_HEADS
    return pl.pallas_call(
        _swa_kernel,
        grid=(bsz, g_, nq),
        in_specs=[
            pl.BlockSpec((TQB, B_GROUP * LANES), lambda b, g, i: (b * nq + i, g)),
            pl.BlockSpec((seq, LANES), lambda b, g, i: (b, g)),
            pl.BlockSpec((seq, LANES), lambda b, g, i: (b, g_ + g)),
            pl.BlockSpec((None, B_GROUP, LANES), lambda b, g, i: (g, 0, 0)),
        ],
        out_specs=pl.BlockSpec((TQB, B_GROUP * HEAD_DIM), lambda b, g, i: (b * nq + i, g)),
        out_shape=jax.ShapeDtypeStruct((bsz * seq, B_HEADS * HEAD_DIM), BF16),
        compiler_params=_cp("parallel", "parallel", "arbitrary"),
        name="swa",
    )(qproj, kvproj, kvproj, sinks)


def _router_kernel(x_ref, wh_ref, wl_ref, o_ref):
    x = x_ref[...]
    xh = x.astype(BF16)
    xl = (x - xh.astype(F32)).astype(BF16)
    logits = _dot(xh, wh_ref[...]) + _dot(xh, wl_ref[...]) + _dot(xl, wh_ref[...])
    lane = lax.broadcasted_iota(jnp.int32, logits.shape, 1)
    lg = jnp.where(lane < N_EXPERTS, logits, -jnp.inf)
    m1 = jnp.max(lg, axis=1, keepdims=True)
    i1 = jnp.min(jnp.where(lg == m1, lane, LANES), axis=1, keepdims=True)
    lg2 = jnp.where(lane == i1, -jnp.inf, lg)
    m2 = jnp.max(lg2, axis=1, keepdims=True)
    i2 = jnp.min(jnp.where(lg2 == m2, lane, LANES), axis=1, keepdims=True)
    e2 = jnp.exp(m2 - m1)
    g1 = 1.0 / (1.0 + e2)
    g2 = e2 * g1
    out = jnp.where(lane == 0, i1.astype(F32), jnp.where(lane == 1, i2.astype(F32),
                    jnp.where(lane == 2, g1, jnp.where(lane == 3, g2, 0.0))))
    o_ref[...] = out


def _router(h, w_hi, w_lo, tm=512):
    m, d = h.shape
    return pl.pallas_call(
        _router_kernel,
        grid=(m // tm,),
        in_specs=[pl.BlockSpec((tm, d), lambda i: (i, 0)), pl.BlockSpec((d, LANES), lambda i: (0, 0)),
                  pl.BlockSpec((d, LANES), lambda i: (0, 0))],
        out_specs=pl.BlockSpec((tm, LANES), lambda i: (i, 0)),
        out_shape=jax.ShapeDtypeStruct((m, LANES), F32),
        compiler_params=_cp("parallel"),
        name="router",
    )(h, w_hi, w_lo)


def _moe_ffn_kernel(blk_e_ref, n_act_ref, x_ref, wg_ref, wu_ref, wo_ref, o_ref, acc_ref):
    i = pl.program_id(0)
    c = pl.program_id(1)

    @pl.when(c == 0)
    def _():
        acc_ref[...] = jnp.zeros_like(acc_ref)

    @pl.when(i < n_act_ref[0])
    def _():
        xb = x_ref[...]
        gate = _dot(xb, wg_ref[...])
        up = _dot(xb, wu_ref[...])
        a = gate * _sigmoid(gate) * up
        acc_ref[...] += _dot(a.astype(BF16), wo_ref[...])

    @pl.when(c == pl.num_programs(1) - 1)
    def _():
        o_ref[...] = acc_ref[...]


def _moe_ffn(xs, blk_e, n_act, w_in, w_out, tm=MOE_TM, fc=1792):
    cap, d = xs.shape
    ff = w_out.shape[1]
    nc = ff // fc

    def chunk(i, c, n_act_ref):
        return jnp.where(i < n_act_ref[0], c, nc - 1)

    grid_spec = pltpu.PrefetchScalarGridSpec(
        num_scalar_prefetch=2,
        grid=(cap // tm, nc),
        in_specs=[
            pl.BlockSpec((tm, d), lambda i, c, e, na: (i, 0)),
            pl.BlockSpec((None, d, fc), lambda i, c, e, na: (e[i], 0, chunk(i, c, na))),
            pl.BlockSpec((None, d, fc), lambda i, c, e, na: (e[i], 0, nc + chunk(i, c, na))),
            pl.BlockSpec((None, fc, d), lambda i, c, e, na: (e[i], chunk(i, c, na), 0)),
        ],
        out_specs=pl.BlockSpec((tm, d), lambda i, c, e, na: (i, 0)),
        scratch_shapes=[pltpu.VMEM((tm, d), F32)],
    )
    return pl.pallas_call(
        _moe_ffn_kernel,
        grid_spec=grid_spec,
        out_shape=jax.ShapeDtypeStruct((cap, d), F32),
        compiler_params=_cp("parallel", "arbitrary"),
        name="moe_ffn",
    )(blk_e, n_act, xs, w_in, w_in, w_out)


def _combine_ln_kernel(h_ref, y1_ref, y2_ref, r_ref, g_ref, b_ref, o_ref, ob_ref):
    r = r_ref[...]
    ffn = r[:, 2:3] * y1_ref[...] + r[:, 3:4] * y2_ref[...]
    y = _layer_norm(ALPHA * h_ref[...] + ffn, g_ref[...], b_ref[...])
    o_ref[...] = y
    ob_ref[...] = y.astype(BF16)


def _combine_ln(h, y1, y2, route, g, b, tm=512):
    m, d = h.shape
    row = pl.BlockSpec((tm, d), lambda i: (i, 0))
    vec = pl.BlockSpec((1, d), lambda i: (0, 0))
    out_specs, out_shape = _dual_out(m, d, tm)
    return pl.pallas_call(
        _combine_ln_kernel,
        grid=(m // tm,),
        in_specs=[row, row, row, pl.BlockSpec((tm, LANES), lambda i: (i, 0)), vec, vec],
        out_specs=out_specs,
        out_shape=out_shape,
        compiler_params=_cp("parallel"),
        name="combine_ln",
    )(h, y1, y2, route, g, b)


def _pad_heads(w, heads, rope_copy=False):
    d = w.shape[0]
    half = ROT_DIM // 2
    w = w.reshape(d, heads, HEAD_DIM)
    if rope_copy:
        fill = jnp.zeros((d, heads, LANES - HEAD_DIM - half), w.dtype)
        w = jnp.concatenate([w, fill, w[:, :, half:ROT_DIM]], axis=2)
    else:
        w = jnp.pad(w, ((0, 0), (0, 0), (0, LANES - HEAD_DIM)))
    return w.reshape(d, heads * LANES)


def _rope_tables(positions):
    half = ROT_DIM // 2
    inv = ROPE_THETA ** (-jnp.arange(0, ROT_DIM, 2, dtype=F32) / ROT_DIM)
    ang = positions.astype(F32).reshape(-1, 1) * inv
    cos, sin = jnp.cos(ang), jnp.sin(ang)
    n = ang.shape[0]
    c = jnp.concatenate([cos, cos, jnp.ones((n, LANES - ROT_DIM - half), F32), jnp.zeros((n, half), F32)], axis=1)
    s = jnp.concatenate([-sin, sin, jnp.zeros((n, LANES - ROT_DIM), F32)], axis=1)
    return c, s


def _cmp_to_slc(ncp, nslc):
    n = np.arange(ncp)[:, None]
    j = np.arange(nslc)[None, :]
    overlap = (np.minimum(n * CMP_STRIDE + CMP_BLOCK, j * SLC_BLOCK + SLC_BLOCK)
               - np.maximum(n * CMP_STRIDE, j * SLC_BLOCK))
    m = np.clip(overlap, 0, None).astype(np.float32) / CMP_BLOCK
    m[ncp - 1, :] = 0.0
    return jnp.asarray(m, BF16)


def _compress_branch(t, pos, w1, w2, bsz, seq):
    g_ = A_KV_HEADS
    nch = seq // CMP_STRIDE
    half = CMP_STRIDE * HEAD_DIM
    a = t.reshape(bsz * nch, CMP_STRIDE * g_ * HEAD_DIM)
    w1b = w1.astype(BF16)
    w1cat = jnp.concatenate([w1b[:half], w1b[half:]], axis=1)
    eye = jnp.eye(g_, dtype=BF16)
    w1blk = (w1cat.reshape(CMP_STRIDE, 1, HEAD_DIM, 1, 2 * CMP_HID) * eye[None, :, None, :, None])
    w1blk = w1blk.reshape(CMP_STRIDE * g_ * HEAD_DIM, g_ * 2 * CMP_HID)
    pq = _mm(a, w1blk, F32, tm=256, tn=2 * CMP_HID)
    pos8 = jnp.zeros((8, CMP_BLOCK * HEAD_DIM), BF16).at[0].set(pos.reshape(-1).astype(BF16))
    out = _compress(pq, pos8, w1b, w2.astype(BF16), bsz)
    return out.reshape(bsz, nch, g_, HEAD_DIM).transpose(0, 2, 1, 3)


def _nsa_attn(hb, tabs, w_in, cmp_pos, cmp_w1, cmp_w2, bsz, seq):
    hb = hb.astype(BF16)
    g_ = A_KV_HEADS
    aq = A_HEADS * HEAD_DIM
    akv = g_ * HEAD_DIM
    wq = w_in[:, :aq] * (HEAD_DIM ** -0.5)
    w_kc, w_vc, w_ks, w_vs, w_kw, w_vw = (w_in[:, aq + i * akv: aq + (i + 1) * akv] for i in range(6))
    w_gl = w_in[:, aq + 6 * akv:]
    w_big = jnp.concatenate([_pad_heads(wq, A_HEADS, True), _pad_heads(w_ks, g_, True), _pad_heads(w_vs, g_),
                             _pad_heads(w_kw, g_, True), _pad_heads(w_vw, g_)], axis=1).astype(BF16)
    slab_modes = ["plain"] * A_HEADS + ["rope_onehot"] * g_ + ["ones"] * g_ + ["rope"] * g_ + ["ones"] * g_
    proj = _proj(hb, w_big, tabs, slab_modes)

    w_gl_g = w_gl.reshape(-1, 3, g_, A_GROUP).transpose(0, 2, 1, 3).reshape(-1, g_, 3 * A_GROUP)
    w_gl_g = jnp.pad(w_gl_g, ((0, 0), (0, 0), (0, LANES - 3 * A_GROUP))).reshape(-1, g_ * LANES)
    w_side = jnp.concatenate([w_gl_g, w_kc, w_vc], axis=1).astype(BF16)
    gates, kc_in, vc_in = _nsa_side(hb, w_side, g_ * LANES, akv)
    kc = _compress_branch(kc_in, cmp_pos[0], cmp_w1[0], cmp_w2[0], bsz, seq)
    vc = _compress_branch(vc_in, cmp_pos[1], cmp_w1[1], cmp_w2[1], bsz, seq)
    pad = ((0, 0), (0, 0), (0, 0), (0, LANES - HEAD_DIM))
    kc = jnp.pad(kc * LOG2E, pad).astype(BF16)
    vc = jnp.pad(vc, pad).astype(BF16)
    ncp = seq // CMP_STRIDE
    cmp_map_t = _cmp_to_slc(ncp, seq // SLC_BLOCK).T

    o_c, selbias = _cmp_select(proj, kc, vc, cmp_map_t, bsz, seq)
    o = _sel_win(proj, tabs, selbias, o_c, gates, bsz, seq)
    return o, selbias, o_c


def _shared_kv(hb, tabs, w_kv):
    g_ = B_KV_HEADS
    bkv = g_ * HEAD_DIM
    w = jnp.concatenate([_pad_heads(w_kv[:, :bkv], g_, True), _pad_heads(w_kv[:, bkv:], g_)], axis=1).astype(BF16)
    return _proj(hb.astype(BF16), w, tabs, ["rope"] * g_ + ["ones"] * g_)


def _swa_attn(hb, tabs, kvproj, w_q, sinks, bsz, seq):
    wq = _pad_heads(w_q * (HEAD_DIM ** -0.5), B_HEADS, True).astype(BF16)
    qproj = _proj(hb.astype(BF16), wq, tabs, ["rope"] * B_HEADS)
    sk = jnp.broadcast_to(sinks.astype(F32).reshape(B_KV_HEADS, B_GROUP, 1), (B_KV_HEADS, B_GROUP, LANES))
    return _swa(qproj, kvproj, sk, bsz, seq)


def _moe_layer(h, hb, w_router, w_in_all, w_out_all, layer, ln_g, ln_b):
    y1, y2, route = _moe_experts(h, hb, w_router, w_in_all, w_out_all, layer)
    return _combine_ln(h, y1, y2, route, ln_g[None, :], ln_b[None, :])


def _moe_ffn_out(h, w_router, w_in_all, w_out_all, layer):
    y1, y2, route = _moe_experts(h, h.astype(BF16), w_router, w_in_all, w_out_all, layer)
    return route[:, 2:3] * y1 + route[:, 3:4] * y2


def _moe_experts(h, hb, w_router, w_in_all, w_out_all, layer):
    n_tok, d = h.shape
    wr = jnp.pad(w_router, ((0, 0), (0, LANES - N_EXPERTS)))
    wr_hi = wr.astype(BF16)
    wr_lo = (wr - wr_hi.astype(F32)).astype(BF16)
    route = _router(h, wr_hi, wr_lo)
    top_e = route[:, :2].astype(jnp.int32)

    e_flat = top_e.reshape(-1)
    onehot = (e_flat[:, None] == jnp.arange(N_EXPERTS)[None, :]).astype(jnp.int32)
    csum = jnp.cumsum(onehot, axis=0)
    counts = csum[-1]
    rank = jnp.take_along_axis(csum, e_flat[:, None], axis=1)[:, 0] - 1
    padded = (counts + MOE_TM - 1) // MOE_TM * MOE_TM
    pad_end = jnp.cumsum(padded)
    pad_start = pad_end - padded
    dest = pad_start[e_flat] + rank
    cap = n_tok * 2 + N_EXPERTS * MOE_TM
    n_blk = cap // MOE_TM
    tok_flat = jnp.repeat(jnp.arange(n_tok, dtype=jnp.int32), 2)
    buf_tok = jnp.zeros((cap,), jnp.int32).at[dest].set(tok_flat, unique_indices=True, mode="promise_in_bounds")
    blk_start = jnp.arange(n_blk, dtype=jnp.int32) * MOE_TM
    blk_e = jnp.sum((pad_end[None, :] <= blk_start[:, None]).astype(jnp.int32), axis=1)
    blk_e = jnp.minimum(blk_e, N_EXPERTS - 1).astype(jnp.int32)

    xs = hb.at[buf_tok].get(mode="promise_in_bounds")
    n_act = (pad_end[-1:] // MOE_TM).astype(jnp.int32)
    w_in = _cast_layer_bf16(w_in_all, layer, tr=128)
    w_out = _cast_layer_bf16(w_out_all, layer, tr=D_FF_EXPERT // 4)
    y = _moe_ffn(xs, blk_e, n_act, w_in, w_out)
    dest2 = dest.reshape(n_tok, 2)
    y1 = y.at[dest2[:, 0]].get(mode="promise_in_bounds", unique_indices=True)
    y2 = y.at[dest2[:, 1]].get(mode="promise_in_bounds", unique_indices=True)
    return y1, y2, route


def kernel(x, positions, w_in_a, w_out_a, cmp_pos, cmp_w1, cmp_w2, w_kv_shared, w_q_b, w_out_b, sinks_b,
           ln_g, ln_b, dense_w_in, dense_w_out, moe_router, moe_w_in, moe_w_out):
    bsz, seq, d = x.shape
    n_a = DEPTH // 2
    tabs = _rope_tables(positions)
    h = x.reshape(bsz * seq, d)
    hb = h.astype(BF16)
    kvproj = None
    for l in range(DEPTH):
        if l < n_a:
            o = _nsa_attn(hb, tabs, w_in_a[l], cmp_pos[l], cmp_w1[l], cmp_w2[l], bsz, seq)[0]
            w_attn = w_out_a[l].astype(BF16)
        else:
            b = l - n_a
            o = _swa_attn(hb, tabs, kvproj, w_q_b[b], sinks_b[b], bsz, seq)
            w_attn = w_out_b[b].astype(BF16)
        if l % 2 == 0:
            h, hb = _mix_ffn(o, w_attn, h, ln_g[l, 0][None, :], ln_b[l, 0][None, :],
                             dense_w_in[l // 2].astype(BF16), dense_w_out[l // 2].astype(BF16),
                             ln_g[l, 1][None, :], ln_b[l, 1][None, :])
        else:
            h, hb = _out_ln(o, w_attn, h, ln_g[l, 0][None, :], ln_b[l, 0][None, :])
            h, hb = _moe_layer(h, hb, moe_router[l // 2], moe_w_in, moe_w_out, l // 2,
                               ln_g[l, 1], ln_b[l, 1])
        if l == n_a - 1:
            kvproj = _shared_kv(hb, tabs, w_kv_shared)
    return h.reshape(bsz, seq, d)
```

```python
import functools

import numpy as np
import jax
import jax.numpy as jnp
from jax import lax
from jax.experimental import pallas as pl
from jax.experimental.pallas import tpu as pltpu

F32 = jnp.float32
BF16 = jnp.bfloat16

D_MODEL = 1024
DEPTH = 4
HEAD_DIM = 64
LANES = 128
ROT_DIM = HEAD_DIM // 4
ROPE_THETA = 500000.0
A_HEADS = 16
A_KV_HEADS = 4
A_GROUP = 4
CMP_BLOCK = 32
CMP_STRIDE = 16
CMP_HID = 256
SLC_BLOCK = 64
N_SEL = 16
N_FORCED = 3
NSA_WIN = 512
B_HEADS = 16
B_KV_HEADS = 2
B_GROUP = 8
SWA_WIN = 128
D_FF = 2816
N_EXPERTS = 8
D_FF_EXPERT = 3584
ALPHA = (2 * DEPTH) ** 0.25
LN_EPS = 1e-5
NEG = -1e30
FORCE = 1e9

TQB = 256
TQA = 256
CMP_CHUNK = 256
LOG2E = 1.4426950408889634
KT = 512
BLK_PER_KT = KT // SLC_BLOCK
KT_GROUP = 4
MOE_TM = 512
VMEM_LIMIT = 56 * 1024 * 1024


def _cp(*sem):
    return pltpu.CompilerParams(dimension_semantics=sem, vmem_limit_bytes=VMEM_LIMIT)


def _dot(a, b):
    return jnp.dot(a, b, preferred_element_type=F32)


def _dot_nt(a, b):
    return lax.dot_general(a, b, (((1,), (1,)), ((), ())), preferred_element_type=F32)


def _sigmoid(x):
    return 1.0 / (1.0 + jnp.exp(-x))


def _layer_norm(z, g, b):
    mu = jnp.mean(z, axis=-1, keepdims=True)
    zc = z - mu
    var = jnp.mean(zc * zc, axis=-1, keepdims=True)
    return zc * lax.rsqrt(var + LN_EPS) * g + b


def _rope(x, c, s):
    half = ROT_DIM // 2
    ns = x.shape[1] // LANES
    rolled = [pltpu.roll(x[:, i * LANES:(i + 1) * LANES], half, 1) for i in range(ns)]
    z = rolled[0] if ns == 1 else jnp.concatenate(rolled, axis=1)
    return x * c + z * s


def _tile_lanes(t, n):
    return t if n == 1 else jnp.concatenate([t] * n, axis=1)


def _tile_rows(t, n):
    return t if n == 1 else jnp.concatenate([t] * n, axis=0)


def _normalize(acc, extra=0.0):
    den = pltpu.roll(acc, HEAD_DIM, 1) + extra
    return (acc / den)[:, :HEAD_DIM]


def _heads_to_lanes(o, heads):
    t = o.shape[0] // heads
    return jnp.concatenate([o[r * t:(r + 1) * t] for r in range(heads)], axis=1)


def _mm_kernel(x_ref, w_ref, o_ref):
    o_ref[...] = _dot(x_ref[...].astype(BF16), w_ref[...]).astype(o_ref.dtype)


def _mm(x, w, out_dtype, tm=512, tn=None):
    m, k = x.shape
    n = w.shape[1]
    tm = min(tm, m)
    tn = n if tn is None else tn
    return pl.pallas_call(
        _mm_kernel,
        grid=(m // tm, n // tn),
        in_specs=[pl.BlockSpec((tm, k), lambda i, j: (i, 0)), pl.BlockSpec((k, tn), lambda i, j: (0, j))],
        out_specs=pl.BlockSpec((tm, tn), lambda i, j: (i, j)),
        out_shape=jax.ShapeDtypeStruct((m, n), out_dtype),
        compiler_params=_cp("parallel", "arbitrary"),
        name="mm",
    )(x, w)


def _nsa_side_kernel(x_ref, w_ref, gate_ref, kc_ref, vc_ref):
    acc = _dot(x_ref[...], w_ref[...])
    ng = gate_ref.shape[1]
    nk = kc_ref.shape[1]
    gate_ref[...] = _sigmoid(acc[:, :ng])
    kc_ref[...] = acc[:, ng:ng + nk]
    vc_ref[...] = acc[:, ng + nk:]


def _nsa_side(hb, w, n_gate, n_kv, tm=512):
    m, k = hb.shape
    row = lambda n: pl.BlockSpec((tm, n), lambda i: (i, 0))
    return pl.pallas_call(
        _nsa_side_kernel,
        grid=(m // tm,),
        in_specs=[row(k), pl.BlockSpec(w.shape, lambda i: (0, 0))],
        out_specs=[row(n_gate), row(n_kv), row(n_kv)],
        out_shape=[jax.ShapeDtypeStruct((m, n_gate), F32), jax.ShapeDtypeStruct((m, n_kv), F32),
                   jax.ShapeDtypeStruct((m, n_kv), F32)],
        compiler_params=_cp("parallel"),
        name="nsa_side",
    )(hb, w)


def _cast_kernel(x_ref, o_ref):
    o_ref[...] = x_ref[...].astype(o_ref.dtype)


def _cast_layer_bf16(w, layer, tr):
    _, e, r, c = w.shape
    return pl.pallas_call(
        _cast_kernel,
        grid=(e, r // tr),
        in_specs=[pl.BlockSpec((None, None, tr, c), lambda ei, i: (layer, ei, i, 0))],
        out_specs=pl.BlockSpec((None, tr, c), lambda ei, i: (ei, i, 0)),
        out_shape=jax.ShapeDtypeStruct((e, r, c), BF16),
        compiler_params=_cp("parallel", "arbitrary"),
        name="cast_bf16",
    )(w)


def _proj_kernel(x_ref, w_ref, c_ref, s_ref, o_ref, *, block_modes, tm, tn):
    i = pl.program_id(0)
    j = pl.program_id(1)
    acc = _dot(x_ref[...], w_ref[...])

    def piece(mode, s0, ns):
        a = acc[:, s0 * LANES:(s0 + ns) * LANES]
        lane = lax.broadcasted_iota(jnp.int32, a.shape, 1) % LANES
        if mode in ("rope", "rope_onehot"):
            a = _rope(a, _tile_lanes(c_ref[...], ns), _tile_lanes(s_ref[...], ns))
        if mode == "rope_onehot":
            row = i * tm + lax.broadcasted_iota(jnp.int32, a.shape, 0)
            a = a + jnp.where(lane == HEAD_DIM + (row // SLC_BLOCK) % BLK_PER_KT, 1.0, 0.0)
        if mode == "ones":
            a = a + jnp.where(lane >= HEAD_DIM, 1.0, 0.0)
        return a.astype(o_ref.dtype)

    for modes in sorted(set(block_modes)):
        pred = functools.reduce(jnp.logical_or, [j == jj for jj, mm in enumerate(block_modes) if mm == modes])

        @pl.when(pred)
        def _(modes=modes):
            runs = []
            for s, mode in enumerate(modes):
                if runs and runs[-1][0] == mode:
                    runs[-1][2] += 1
                else:
                    runs.append([mode, s, 1])
            pieces = [piece(*r) for r in runs]
            o_ref[...] = pieces[0] if len(pieces) == 1 else jnp.concatenate(pieces, axis=1)


def _proj(x, w, tabs, slab_modes, tn=1024, tm=1024):
    m, k = x.shape
    n = w.shape[1]
    tn = min(tn, n)
    spb = tn // LANES
    block_modes = tuple(tuple(slab_modes[jj * spb:(jj + 1) * spb]) for jj in range(n // tn))
    c, s = tabs
    return pl.pallas_call(
        functools.partial(_proj_kernel, block_modes=block_modes, tm=tm, tn=tn),
        grid=(m // tm, n // tn),
        in_specs=[
            pl.BlockSpec((tm, k), lambda i, j: (i, 0)),
            pl.BlockSpec((k, tn), lambda i, j: (0, j)),
            pl.BlockSpec((tm, LANES), lambda i, j: (i, 0)),
            pl.BlockSpec((tm, LANES), lambda i, j: (i, 0)),
        ],
        out_specs=pl.BlockSpec((tm, tn), lambda i, j: (i, j)),
        out_shape=jax.ShapeDtypeStruct((m, n), BF16),
        compiler_params=_cp("parallel", "arbitrary"),
        name="proj",
    )(x, w, c, s)


def _out_ln_kernel(x_ref, w_ref, h_ref, g_ref, b_ref, o_ref, ob_ref):
    mix = _dot(x_ref[...], w_ref[...])
    y = _layer_norm(ALPHA * h_ref[...] + mix, g_ref[...], b_ref[...])
    o_ref[...] = y
    ob_ref[...] = y.astype(BF16)


def _dual_out(m, d, tm):
    spec = pl.BlockSpec((tm, d), lambda i, *_: (i, 0))
    return [spec, spec], [jax.ShapeDtypeStruct((m, d), F32), jax.ShapeDtypeStruct((m, d), BF16)]


def _out_ln(x, w, h, g, b, tm=512):
    m, k = x.shape
    d = w.shape[1]
    out_specs, out_shape = _dual_out(m, d, tm)
    return pl.pallas_call(
        _out_ln_kernel,
        grid=(m // tm,),
        in_specs=[
            pl.BlockSpec((tm, k), lambda i: (i, 0)),
            pl.BlockSpec((k, d), lambda i: (0, 0)),
            pl.BlockSpec((tm, d), lambda i: (i, 0)),
            pl.BlockSpec((1, d), lambda i: (0, 0)),
            pl.BlockSpec((1, d), lambda i: (0, 0)),
        ],
        out_specs=out_specs,
        out_shape=out_shape,
        compiler_params=_cp("parallel"),
        name="out_ln",
    )(x, w, h, g, b)


def _mix_ffn_kernel(o_ref, wa_ref, h_ref, g1_ref, b1_ref, wg_ref, wu_ref, wo_ref, g2_ref, b2_ref, y_ref, yb_ref):
    h1 = _layer_norm(ALPHA * h_ref[...] + _dot(o_ref[...], wa_ref[...]), g1_ref[...], b1_ref[...])
    xb = h1.astype(BF16)
    gate = _dot(xb, wg_ref[...])
    up = _dot(xb, wu_ref[...])
    a = gate * _sigmoid(gate) * up
    y = _layer_norm(ALPHA * h1 + _dot(a.astype(BF16), wo_ref[...]), g2_ref[...], b2_ref[...])
    y_ref[...] = y
    yb_ref[...] = y.astype(BF16)


def _mix_ffn(o, w_attn, h, g1, b1, w_in, w_out, g2, b2, tm=512):
    m, d = h.shape
    ff = w_out.shape[0]
    once = pl.Buffered(1)
    row = pl.BlockSpec((tm, d), lambda i: (i, 0))
    vec = pl.BlockSpec((1, d), lambda i: (0, 0))
    out_specs, out_shape = _dual_out(m, d, tm)
    return pl.pallas_call(
        _mix_ffn_kernel,
        grid=(m // tm,),
        in_specs=[
            pl.BlockSpec((tm, o.shape[1]), lambda i: (i, 0)),
            pl.BlockSpec(w_attn.shape, lambda i: (0, 0), pipeline_mode=once),
            row, vec, vec,
            pl.BlockSpec((d, ff), lambda i: (0, 0), pipeline_mode=once),
            pl.BlockSpec((d, ff), lambda i: (0, 1), pipeline_mode=once),
            pl.BlockSpec((ff, d), lambda i: (0, 0), pipeline_mode=once),
            vec, vec,
        ],
        out_specs=out_specs,
        out_shape=out_shape,
        compiler_params=_cp("parallel"),
        name="mix_ffn",
    )(o, w_attn, h, g1, b1, w_in, w_in, w_out, g2, b2)


def _compress_kernel(pq_ref, pos_ref, w1_ref, w2_ref, o_ref):
    nch = pq_ref.shape[0]
    hid = w2_ref.shape[0]
    posb = _dot(pos_ref[...], w1_ref[...])[0:1, :]
    outs = []
    for g in range(A_KV_HEADS):
        p = pq_ref[:, 2 * g * hid:(2 * g + 1) * hid]
        q_next = pltpu.roll(pq_ref[:, (2 * g + 1) * hid:(2 * g + 2) * hid], nch - 1, 0)
        act = jax.nn.gelu(p + q_next + posb, approximate=True)
        outs.append(_dot(act.astype(BF16), w2_ref[...]))
    o_ref[...] = jnp.concatenate(outs, axis=1)


def _compress(pq, pos8, w1, w2, bsz):
    m, n = pq.shape
    nch = m // bsz
    dh = w2.shape[1]
    return pl.pallas_call(
        _compress_kernel,
        grid=(bsz,),
        in_specs=[
            pl.BlockSpec((nch, n), lambda b: (b, 0)),
            pl.BlockSpec(pos8.shape, lambda b: (0, 0)),
            pl.BlockSpec(w1.shape, lambda b: (0, 0)),
            pl.BlockSpec(w2.shape, lambda b: (0, 0)),
        ],
        out_specs=pl.BlockSpec((nch, A_KV_HEADS * dh), lambda b: (b, 0)),
        out_shape=jax.ShapeDtypeStruct((m, A_KV_HEADS * dh), F32),
        compiler_params=_cp("parallel"),
        name="compress",
    )(pq, pos8, w1, w2)


def _cmp_select_kernel(q_ref, kc_ref, vc_ref, mapt_ref, oc_ref, sb_ref, *, ncp, nslc):
    qi = pl.program_id(2)
    t0 = qi * TQA
    rows = A_GROUP * TQA
    qb = q_ref[...]
    q4 = jnp.concatenate([qb[:, r * LANES:(r + 1) * LANES] for r in range(A_GROUP)], axis=0)

    def branch(nb):
        nk = (nb + 1) * CMP_CHUNK
        nr = nk * CMP_STRIDE // SLC_BLOCK
        s = _dot_nt(q4, kc_ref[0:nk, :])
        t_tok = t0 + lax.broadcasted_iota(jnp.int32, (TQA, nk), 0)
        n_idx = lax.broadcasted_iota(jnp.int32, (TQA, nk), 1)
        valid = (n_idx * CMP_STRIDE + CMP_BLOCK - 1 <= t_tok) & (n_idx < ncp - 1)
        s3 = s.reshape(A_GROUP, TQA, nk) + jnp.where(valid, 0.0, NEG)[None]
        m = jnp.max(s3, axis=2, keepdims=True)
        p = jnp.exp2(s3 - m)
        l = jnp.sum(p, axis=2, keepdims=True)
        pn = p * jnp.where(m > 0.5 * NEG, 1.0 / l, 0.0)
        oc = _dot(pn.reshape(rows, nk).astype(BF16), vc_ref[0:nk, :])
        oc_ref[...] = oc[:, :HEAD_DIM].reshape(A_GROUP, TQA, HEAD_DIM)
        psum = pn[0] + pn[1] + pn[2] + pn[3]
        p_hi = psum.astype(BF16)
        p_lo = (psum - p_hi.astype(F32)).astype(BF16)
        mapt = mapt_ref[0:nr, 0:nk]
        imp = _dot_nt(mapt, p_hi) + _dot_nt(mapt, p_lo)

        j_idx = lax.broadcasted_iota(jnp.int32, (nr, TQA), 0)
        cur = (t0 + lax.broadcasted_iota(jnp.int32, (nr, TQA), 1)) // SLC_BLOCK
        forced = (j_idx == 0) | (j_idx == cur) | (j_idx == cur - 1)
        score = jnp.where(j_idx > cur, -1.0, jnp.where(forced, -jnp.inf, imp))
        for _ in range(N_SEL - N_FORCED):
            mx = jnp.max(score, axis=0, keepdims=True)
            first = jnp.min(jnp.where(score == mx, j_idx, nr), axis=0, keepdims=True)
            score = jnp.where(j_idx == first, -jnp.inf, score)
        sb_t = jnp.where((score == -jnp.inf) & (j_idx <= cur), 0.0, NEG)
        if nr < nslc:
            sb_t = jnp.concatenate([sb_t, jnp.full((nslc - nr, TQA), NEG, F32)], axis=0)
        sb_ref[...] = sb_t.T

    bucket = ((qi + 1) * (TQA // CMP_STRIDE) - 1) // CMP_CHUNK
    for nb in range(ncp // CMP_CHUNK):
        pl.when(bucket == nb)(functools.partial(branch, nb))


def _cmp_select(proj, kc_cmp, vc_cmp, cmp_map_t, bsz, seq):
    ncp = kc_cmp.shape[2]
    nslc = seq // SLC_BLOCK
    nq = seq // TQA
    g_ = A_KV_HEADS
    assert nslc >= N_SEL and ncp % CMP_CHUNK == 0
    return pl.pallas_call(
        functools.partial(_cmp_select_kernel, ncp=ncp, nslc=nslc),
        grid=(bsz, g_, nq),
        in_specs=[
            pl.BlockSpec((TQA, A_GROUP * LANES), lambda b, g, i: (b * nq + i, g)),
            pl.BlockSpec((None, None, ncp, LANES), lambda b, g, i: (b, g, 0, 0)),
            pl.BlockSpec((None, None, ncp, LANES), lambda b, g, i: (b, g, 0, 0)),
            pl.BlockSpec((nslc, ncp), lambda b, g, i: (0, 0)),
        ],
        out_specs=[
            pl.BlockSpec((None, None, A_GROUP, TQA, HEAD_DIM), lambda b, g, i: (b, g, 0, i, 0)),
            pl.BlockSpec((TQA, nslc), lambda b, g, i: (b * nq + i, g)),
        ],
        out_shape=[
            jax.ShapeDtypeStruct((bsz, g_, A_GROUP, seq, HEAD_DIM), F32),
            jax.ShapeDtypeStruct((bsz * seq, g_ * nslc), F32),
        ],
        compiler_params=_cp("parallel", "parallel", "arbitrary"),
        name="cmp_select",
    )(proj, kc_cmp, vc_cmp, cmp_map_t)


def _sel_win_kernel(q_ref, c_ref, s_ref, bias_ref, ks_ref, vs_ref, kw_ref, vw_ref, oc_ref, gate_ref,
                    o_ref, acc_ref, *, nslc):
    qi = pl.program_id(2)
    t0 = qi * TQA
    rows = A_GROUP * TQA
    qb = q_ref[...]
    q4 = jnp.concatenate([qb[:, r * LANES:(r + 1) * LANES] for r in range(A_GROUP)], axis=0).astype(F32)
    q_rot = _rope(q4, _tile_rows(c_ref[...], A_GROUP), _tile_rows(s_ref[...], A_GROUP)) * LOG2E
    q_rot_b = q_rot.astype(BF16)
    bias = bias_ref[...]
    lane = lax.broadcasted_iota(jnp.int32, (TQA, LANES), 1)
    bias_lanes = (lane >= HEAD_DIM) & (lane < HEAD_DIM + BLK_PER_KT)
    t_q = t0 + lax.broadcasted_iota(jnp.int32, (TQA, KT), 0)
    k_off = lax.broadcasted_iota(jnp.int32, (TQA, KT), 1)

    def add_mask(s, mask_bias):
        return (s.reshape(A_GROUP, TQA, -1) + mask_bias[None]).reshape(s.shape)

    def scores(kt):
        shift = (HEAD_DIM + nslc - BLK_PER_KT * kt) % nslc
        rolled = pltpu.roll(bias, shift, 1)[:, :LANES]
        qa = (q_rot + _tile_rows(jnp.where(bias_lanes, rolled, 0.0), A_GROUP)).astype(BF16)
        k0 = pl.multiple_of(kt * KT, KT)
        return _dot_nt(qa, ks_ref[pl.ds(k0, KT), :])

    def causal_bias(kt):
        return jnp.where(kt * KT + k_off <= t_q, 0.0, NEG)

    def group(kt0, n, carry, diagonal_last):
        m, acc = carry
        ss = [scores(kt0 + i) for i in range(n)]
        if diagonal_last:
            ss[-1] = add_mask(ss[-1], causal_bias(kt0 + n - 1))
        m_row = functools.reduce(jnp.maximum, [jnp.max(s, axis=1, keepdims=True) for s in ss])
        m_new = jnp.maximum(m, jnp.broadcast_to(m_row, (rows, LANES)))
        m_keys = _tile_lanes(m_new, KT // LANES)
        pv = None
        for i, s in enumerate(ss):
            k0 = pl.multiple_of((kt0 + i) * KT, KT)
            d = _dot(jnp.exp2(s - m_keys).astype(BF16), vs_ref[pl.ds(k0, KT), :])
            pv = d if pv is None else pv + d
        return m_new, jnp.exp2(m - m_new) * acc + pv

    kt_d = t0 // KT
    n_quads = kt_d // KT_GROUP
    carry = (jnp.full((rows, LANES), NEG, F32), jnp.zeros((rows, LANES), F32))
    carry = lax.fori_loop(0, n_quads, lambda j, c: group(j * KT_GROUP, KT_GROUP, c, False), carry)
    for rem in range(KT_GROUP):
        @pl.when(kt_d - n_quads * KT_GROUP == rem)
        def _(rem=rem):
            acc_ref[...] = group(kt_d - rem, rem + 1, carry, True)[1]
    o_s = _normalize(acc_ref[...])

    wspan = NSA_WIN + TQA
    w0 = pl.multiple_of(jnp.maximum(t0 - NSA_WIN, 0), TQA)
    s_w = _dot_nt(q_rot_b, kw_ref[pl.ds(w0, wspan), :])
    tw = t0 + lax.broadcasted_iota(jnp.int32, (TQA, wspan), 0)
    diff = tw - (w0 + lax.broadcasted_iota(jnp.int32, (TQA, wspan), 1))
    s_w = add_mask(s_w, jnp.where((diff >= 0) & (diff < NSA_WIN), 0.0, NEG))
    p_w = jnp.exp2(s_w - jnp.max(s_w, axis=1, keepdims=True))
    acc_w = _dot(p_w.astype(BF16), vw_ref[pl.ds(w0, wspan), :])
    o_w = _normalize(acc_w)

    gates = gate_ref[...]

    def gcol(c):
        return jnp.concatenate([gates[:, c * A_GROUP + r:c * A_GROUP + r + 1] for r in range(A_GROUP)], axis=0)

    o_c = oc_ref[...].reshape(rows, HEAD_DIM)
    o = gcol(0) * o_c + gcol(1) * o_s + gcol(2) * o_w
    o_ref[...] = _heads_to_lanes(o, A_GROUP).astype(o_ref.dtype)


def _sel_win(proj, tabs, selbias, o_c, gates, bsz, seq):
    nslc = seq // SLC_BLOCK
    nq = seq // TQA
    g_ = A_KV_HEADS
    c, s = tabs
    tab_spec = pl.BlockSpec((TQA, LANES), lambda b, g, i: (b * nq + i, 0))
    q_slabs = A_HEADS

    def kv_spec(base):
        return pl.BlockSpec((seq, LANES), lambda b, g, i: (b, base + g))

    return pl.pallas_call(
        functools.partial(_sel_win_kernel, nslc=nslc),
        grid=(bsz, g_, nq),
        in_specs=[
            pl.BlockSpec((TQA, A_GROUP * LANES), lambda b, g, i: (b * nq + i, g)),
            tab_spec, tab_spec,
            pl.BlockSpec((TQA, nslc), lambda b, g, i: (b * nq + i, g)),
            kv_spec(q_slabs), kv_spec(q_slabs + g_), kv_spec(q_slabs + 2 * g_), kv_spec(q_slabs + 3 * g_),
            pl.BlockSpec((None, None, A_GROUP, TQA, HEAD_DIM), lambda b, g, i: (b, g, 0, i, 0)),
            pl.BlockSpec((TQA, LANES), lambda b, g, i: (b * nq + i, g)),
        ],
        out_specs=pl.BlockSpec((TQA, A_GROUP * HEAD_DIM), lambda b, g, i: (b * nq + i, g)),
        out_shape=jax.ShapeDtypeStruct((bsz * seq, A_HEADS * HEAD_DIM), BF16),
        scratch_shapes=[pltpu.VMEM((A_GROUP * TQA, LANES), F32)],
        compiler_params=_cp("parallel", "parallel", "arbitrary"),
        name="sel_win",
    )(proj, c, s, selbias, proj, proj, proj, proj, o_c, gates)


def _swa_kernel(q_ref, k_ref, v_ref, sink_ref, o_ref):
    qi = pl.program_id(2)
    t0 = qi * TQB
    rows = B_GROUP * TQB
    span = SWA_WIN + TQB
    qb = q_ref[...]
    q8 = jnp.concatenate([qb[:, r * LANES:(r + 1) * LANES] for r in range(B_GROUP)], axis=0)
    w0 = pl.multiple_of(jnp.maximum(t0 - SWA_WIN, 0), SWA_WIN)
    s = _dot_nt(q8, k_ref[pl.ds(w0, span), :])
    tq = t0 + lax.broadcasted_iota(jnp.int32, (TQB, span), 0)
    diff = tq - (w0 + lax.broadcasted_iota(jnp.int32, (TQB, span), 1))
    band = jnp.where((diff >= 0) & (diff < SWA_WIN), 0.0, NEG)
    s = (s.reshape(B_GROUP, TQB, span) + band[None]).reshape(rows, span)
    sk = sink_ref[...]
    sink = jnp.concatenate([jnp.broadcast_to(sk[r:r + 1, :], (TQB, LANES)) for r in range(B_GROUP)], axis=0)
    m = jnp.maximum(jnp.broadcast_to(jnp.max(s, axis=1, keepdims=True), (rows, LANES)), sink)
    e = jnp.exp(s - _tile_lanes(m, span // LANES))
    acc = _dot(e.astype(BF16), v_ref[pl.ds(w0, span), :])
    o = _normalize(acc, jnp.exp(sink - m))
    o_ref[...] = _heads_to_lanes(o, B_GROUP).astype(o_ref.dtype)


def _swa(qproj, kvproj, sinks, bsz, seq):
    nq = seq // TQB
    g_ = B_KV_HEADS
    return pl.pallas_call(
        _swa_kernel,
        grid=(bsz, g_, nq),
        in_specs=[
            pl.BlockSpec((TQB, B_GROUP * LANES), lambda b, g, i: (b * nq + i, g)),
            pl.BlockSpec((seq, LANES), lambda b, g, i: (b, g)),
            pl.BlockSpec((seq, LANES), lambda b, g, i: (b, g_ + g)),
            pl.BlockSpec((None, B_GROUP, LANES), lambda b, g, i: (g, 0, 0)),
        ],
        out_specs=pl.BlockSpec((TQB, B_GROUP * HEAD_DIM), lambda b, g, i: (b * nq + i, g)),
        out_shape=jax.ShapeDtypeStruct((bsz * seq, B_HEADS * HEAD_DIM), BF16),
        compiler_params=_cp("parallel", "parallel", "arbitrary"),
        name="swa",
    )(qproj, kvproj, kvproj, sinks)


def _router_kernel(x_ref, wh_ref, wl_ref, o_ref):
    x = x_ref[...]
    xh = x.astype(BF16)
    xl = (x - xh.astype(F32)).astype(BF16)
    logits = _dot(xh, wh_ref[...]) + _dot(xh, wl_ref[...]) + _dot(xl, wh_ref[...])
    lane = lax.broadcasted_iota(jnp.int32, logits.shape, 1)
    lg = jnp.where(lane < N_EXPERTS, logits, -jnp.inf)
    m1 = jnp.max(lg, axis=1, keepdims=True)
    i1 = jnp.min(jnp.where(lg == m1, lane, LANES), axis=1, keepdims=True)
    lg2 = jnp.where(lane == i1, -jnp.inf, lg)
    m2 = jnp.max(lg2, axis=1, keepdims=True)
    i2 = jnp.min(jnp.where(lg2 == m2, lane, LANES), axis=1, keepdims=True)
    e2 = jnp.exp(m2 - m1)
    g1 = 1.0 / (1.0 + e2)
    g2 = e2 * g1
    out = jnp.where(lane == 0, i1.astype(F32), jnp.where(lane == 1, i2.astype(F32),
                    jnp.where(lane == 2, g1, jnp.where(lane == 3, g2, 0.0))))
    o_ref[...] = out


def _router(h, w_hi, w_lo, tm=512):
    m, d = h.shape
    return pl.pallas_call(
        _router_kernel,
        grid=(m // tm,),
        in_specs=[pl.BlockSpec((tm, d), lambda i: (i, 0)), pl.BlockSpec((d, LANES), lambda i: (0, 0)),
                  pl.BlockSpec((d, LANES), lambda i: (0, 0))],
        out_specs=pl.BlockSpec((tm, LANES), lambda i: (i, 0)),
        out_shape=jax.ShapeDtypeStruct((m, LANES), F32),
        compiler_params=_cp("parallel"),
        name="router",
    )(h, w_hi, w_lo)


def _moe_ffn_kernel(blk_e_ref, n_act_ref, x_ref, wg_ref, wu_ref, wo_ref, o_ref, acc_ref):
    i = pl.program_id(0)
    c = pl.program_id(1)

    @pl.when(c == 0)
    def _():
        acc_ref[...] = jnp.zeros_like(acc_ref)

    @pl.when(i < n_act_ref[0])
    def _():
        xb = x_ref[...]
        gate = _dot(xb, wg_ref[...])
        up = _dot(xb, wu_ref[...])
        a = gate * _sigmoid(gate) * up
        acc_ref[...] += _dot(a.astype(BF16), wo_ref[...])

    @pl.when(c == pl.num_programs(1) - 1)
    def _():
        o_ref[...] = acc_ref[...].astype(o_ref.dtype)


def _moe_ffn(xs, blk_e, n_act, w_in, w_out, tm=MOE_TM, fc=1792):
    cap, d = xs.shape
    ff = w_out.shape[1]
    nc = ff // fc

    def chunk(i, c, n_act_ref):
        return jnp.where(i < n_act_ref[0], c, nc - 1)

    grid_spec = pltpu.PrefetchScalarGridSpec(
        num_scalar_prefetch=2,
        grid=(cap // tm, nc),
        in_specs=[
            pl.BlockSpec((tm, d), lambda i, c, e, na: (i, 0)),
            pl.BlockSpec((None, d, fc), lambda i, c, e, na: (e[i], 0, chunk(i, c, na))),
            pl.BlockSpec((None, d, fc), lambda i, c, e, na: (e[i], 0, nc + chunk(i, c, na))),
            pl.BlockSpec((None, fc, d), lambda i, c, e, na: (e[i], chunk(i, c, na), 0)),
        ],
        out_specs=pl.BlockSpec((tm, d), lambda i, c, e, na: (i, 0)),
        scratch_shapes=[pltpu.VMEM((tm, d), F32)],
    )
    return pl.pallas_call(
        _moe_ffn_kernel,
        grid_spec=grid_spec,
        out_shape=jax.ShapeDtypeStruct((cap, d), BF16),
        compiler_params=_cp("parallel", "arbitrary"),
        name="moe_ffn",
    )(blk_e, n_act, xs, w_in, w_in, w_out)


def _combine_ln_kernel(h_ref, y1_ref, y2_ref, r_ref, g_ref, b_ref, o_ref, ob_ref):
    r = r_ref[...]
    ffn = r[:, 2:3] * y1_ref[...].astype(F32) + r[:, 3:4] * y2_ref[...].astype(F32)
    y = _layer_norm(ALPHA * h_ref[...] + ffn, g_ref[...], b_ref[...])
    o_ref[...] = y
    ob_ref[...] = y.astype(BF16)


def _combine_ln(h, y1, y2, route, g, b, tm=512):
    m, d = h.shape
    row = pl.BlockSpec((tm, d), lambda i: (i, 0))
    vec = pl.BlockSpec((1, d), lambda i: (0, 0))
    out_specs, out_shape = _dual_out(m, d, tm)
    return pl.pallas_call(
        _combine_ln_kernel,
        grid=(m // tm,),
        in_specs=[row, row, row, pl.BlockSpec((tm, LANES), lambda i: (i, 0)), vec, vec],
        out_specs=out_specs,
        out_shape=out_shape,
        compiler_params=_cp("parallel"),
        name="combine_ln",
    )(h, y1, y2, route, g, b)


def _pad_heads(w, heads, rope_copy=False):
    d = w.shape[0]
    half = ROT_DIM // 2
    w = w.reshape(d, heads, HEAD_DIM)
    if rope_copy:
        fill = jnp.zeros((d, heads, LANES - HEAD_DIM - half), w.dtype)
        w = jnp.concatenate([w, fill, w[:, :, half:ROT_DIM]], axis=2)
    else:
        w = jnp.pad(w, ((0, 0), (0, 0), (0, LANES - HEAD_DIM)))
    return w.reshape(d, heads * LANES)


def _rope_tables(positions):
    half = ROT_DIM // 2
    inv = ROPE_THETA ** (-jnp.arange(0, ROT_DIM, 2, dtype=F32) / ROT_DIM)
    ang = positions.astype(F32).reshape(-1, 1) * inv
    cos, sin = jnp.cos(ang), jnp.sin(ang)
    n = ang.shape[0]
    c = jnp.concatenate([cos, cos, jnp.ones((n, LANES - ROT_DIM - half), F32), jnp.zeros((n, half), F32)], axis=1)
    s = jnp.concatenate([-sin, sin, jnp.zeros((n, LANES - ROT_DIM), F32)], axis=1)
    return c, s


def _cmp_to_slc(ncp, nslc):
    n = np.arange(ncp)[:, None]
    j = np.arange(nslc)[None, :]
    overlap = (np.minimum(n * CMP_STRIDE + CMP_BLOCK, j * SLC_BLOCK + SLC_BLOCK)
               - np.maximum(n * CMP_STRIDE, j * SLC_BLOCK))
    m = np.clip(overlap, 0, None).astype(np.float32) / CMP_BLOCK
    m[ncp - 1, :] = 0.0
    return jnp.asarray(m, BF16)


def _compress_branch(t, pos, w1, w2, bsz, seq):
    g_ = A_KV_HEADS
    nch = seq // CMP_STRIDE
    half = CMP_STRIDE * HEAD_DIM
    a = t.reshape(bsz * nch, CMP_STRIDE * g_ * HEAD_DIM)
    w1b = w1.astype(BF16)
    w1cat = jnp.concatenate([w1b[:half], w1b[half:]], axis=1)
    eye = jnp.eye(g_, dtype=BF16)
    w1blk = (w1cat.reshape(CMP_STRIDE, 1, HEAD_DIM, 1, 2 * CMP_HID) * eye[None, :, None, :, None])
    w1blk = w1blk.reshape(CMP_STRIDE * g_ * HEAD_DIM, g_ * 2 * CMP_HID)
    pq = _mm(a, w1blk, F32, tm=256, tn=2 * CMP_HID)
    pos8 = jnp.zeros((8, CMP_BLOCK * HEAD_DIM), BF16).at[0].set(pos.reshape(-1).astype(BF16))
    out = _compress(pq, pos8, w1b, w2.astype(BF16), bsz)
    return out.reshape(bsz, nch, g_, HEAD_DIM).transpose(0, 2, 1, 3)


def _nsa_attn(hb, tabs, w_in, cmp_pos, cmp_w1, cmp_w2, bsz, seq):
    hb = hb.astype(BF16)
    g_ = A_KV_HEADS
    aq = A_HEADS * HEAD_DIM
    akv = g_ * HEAD_DIM
    wq = w_in[:, :aq] * (HEAD_DIM ** -0.5)
    w_kc, w_vc, w_ks, w_vs, w_kw, w_vw = (w_in[:, aq + i * akv: aq + (i + 1) * akv] for i in range(6))
    w_gl = w_in[:, aq + 6 * akv:]
    w_big = jnp.concatenate([_pad_heads(wq, A_HEADS, True), _pad_heads(w_ks, g_, True), _pad_heads(w_vs, g_),
                             _pad_heads(w_kw, g_, True), _pad_heads(w_vw, g_)], axis=1).astype(BF16)
    slab_modes = ["plain"] * A_HEADS + ["rope_onehot"] * g_ + ["ones"] * g_ + ["rope"] * g_ + ["ones"] * g_
    proj = _proj(hb, w_big, tabs, slab_modes)

    w_gl_g = w_gl.reshape(-1, 3, g_, A_GROUP).transpose(0, 2, 1, 3).reshape(-1, g_, 3 * A_GROUP)
    w_gl_g = jnp.pad(w_gl_g, ((0, 0), (0, 0), (0, LANES - 3 * A_GROUP))).reshape(-1, g_ * LANES)
    w_side = jnp.concatenate([w_gl_g, w_kc, w_vc], axis=1).astype(BF16)
    gates, kc_in, vc_in = _nsa_side(hb, w_side, g_ * LANES, akv)
    kc = _compress_branch(kc_in, cmp_pos[0], cmp_w1[0], cmp_w2[0], bsz, seq)
    vc = _compress_branch(vc_in, cmp_pos[1], cmp_w1[1], cmp_w2[1], bsz, seq)
    pad = ((0, 0), (0, 0), (0, 0), (0, LANES - HEAD_DIM))
    kc = jnp.pad(kc * LOG2E, pad).astype(BF16)
    vc = jnp.pad(vc, pad).astype(BF16)
    ncp = seq // CMP_STRIDE
    cmp_map_t = _cmp_to_slc(ncp, seq // SLC_BLOCK).T

    o_c, selbias = _cmp_select(proj, kc, vc, cmp_map_t, bsz, seq)
    o = _sel_win(proj, tabs, selbias, o_c, gates, bsz, seq)
    return o, selbias, o_c


def _shared_kv(hb, tabs, w_kv):
    g_ = B_KV_HEADS
    bkv = g_ * HEAD_DIM
    w = jnp.concatenate([_pad_heads(w_kv[:, :bkv], g_, True), _pad_heads(w_kv[:, bkv:], g_)], axis=1).astype(BF16)
    return _proj(hb.astype(BF16), w, tabs, ["rope"] * g_ + ["ones"] * g_)


def _swa_attn(hb, tabs, kvproj, w_q, sinks, bsz, seq):
    wq = _pad_heads(w_q * (HEAD_DIM ** -0.5), B_HEADS, True).astype(BF16)
    qproj = _proj(hb.astype(BF16), wq, tabs, ["rope"] * B_HEADS)
    sk = jnp.broadcast_to(sinks.astype(F32).reshape(B_KV_HEADS, B_GROUP, 1), (B_KV_HEADS, B_GROUP, LANES))
    return _swa(qproj, kvproj, sk, bsz, seq)


def _moe_layer(h, hb, w_router, w_in_all, w_out_all, layer, ln_g, ln_b):
    y1, y2, route = _moe_experts(h, hb, w_router, w_in_all, w_out_all, layer)
    return _combine_ln(h, y1, y2, route, ln_g[None, :], ln_b[None, :])


def _moe_ffn_out(h, w_router, w_in_all, w_out_all, layer):
    y1, y2, route = _moe_experts(h, h.astype(BF16), w_router, w_in_all, w_out_all, layer)
    return route[:, 2:3] * y1.astype(F32) + route[:, 3:4] * y2.astype(F32)


def _moe_experts(h, hb, w_router, w_in_all, w_out_all, layer):
    n_tok, d = h.shape
    wr = jnp.pad(w_router, ((0, 0), (0, LANES - N_EXPERTS)))
    wr_hi = wr.astype(BF16)
    wr_lo = (wr - wr_hi.astype(F32)).astype(BF16)
    route = _router(h, wr_hi, wr_lo)
    top_e = route[:, :2].astype(jnp.int32)

    e_flat = top_e.reshape(-1)
    onehot = (e_flat[:, None] == jnp.arange(N_EXPERTS)[None, :]).astype(jnp.int32)
    csum = jnp.cumsum(onehot, axis=0)
    counts = csum[-1]
    rank = jnp.take_along_axis(csum, e_flat[:, None], axis=1)[:, 0] - 1
    padded = (counts + MOE_TM - 1) // MOE_TM * MOE_TM
    pad_end = jnp.cumsum(padded)
    pad_start = pad_end - padded
    dest = pad_start[e_flat] + rank
    cap = n_tok * 2 + N_EXPERTS * MOE_TM
    n_blk = cap // MOE_TM
    tok_flat = jnp.repeat(jnp.arange(n_tok, dtype=jnp.int32), 2)
    buf_tok = jnp.zeros((cap,), jnp.int32).at[dest].set(tok_flat, unique_indices=True, mode="promise_in_bounds")
    blk_start = jnp.arange(n_blk, dtype=jnp.int32) * MOE_TM
    blk_e = jnp.sum((pad_end[None, :] <= blk_start[:, None]).astype(jnp.int32), axis=1)
    blk_e = jnp.minimum(blk_e, N_EXPERTS - 1).astype(jnp.int32)

    xs = hb.at[buf_tok].get(mode="promise_in_bounds")
    n_act = (pad_end[-1:] // MOE_TM).astype(jnp.int32)
    w_in = _cast_layer_bf16(w_in_all, layer, tr=128)
    w_out = _cast_layer_bf16(w_out_all, layer, tr=D_FF_EXPERT // 4)
    y = _moe_ffn(xs, blk_e, n_act, w_in, w_out)
    dest2 = dest.reshape(n_tok, 2)
    y1 = y.at[dest2[:, 0]].get(mode="promise_in_bounds", unique_indices=True)
    y2 = y.at[dest2[:, 1]].get(mode="promise_in_bounds", unique_indices=True)
    return y1, y2, route


def kernel(x, positions, w_in_a, w_out_a, cmp_pos, cmp_w1, cmp_w2, w_kv_shared, w_q_b, w_out_b, sinks_b,
           ln_g, ln_b, dense_w_in, dense_w_out, moe_router, moe_w_in, moe_w_out):
    bsz, seq, d = x.shape
    n_a = DEPTH // 2
    tabs = _rope_tables(positions)
    h = x.reshape(bsz * seq, d)
    hb = h.astype(BF16)
    kvproj = None
    for l in range(DEPTH):
        if l < n_a:
            o = _nsa_attn(hb, tabs, w_in_a[l], cmp_pos[l], cmp_w1[l], cmp_w2[l], bsz, seq)[0]
            w_attn = w_out_a[l].astype(BF16)
        else:
            b = l - n_a
            o = _swa_attn(hb, tabs, kvproj, w_q_b[b], sinks_b[b], bsz, seq)
            w_attn = w_out_b[b].astype(BF16)
        if l % 2 == 0:
            h, hb = _mix_ffn(o, w_attn, h, ln_g[l, 0][None, :], ln_b[l, 0][None, :],
                             dense_w_in[l // 2].astype(BF16), dense_w_out[l // 2].astype(BF16),
                             ln_g[l, 1][None, :], ln_b[l, 1][None, :])
        else:
            h, hb = _out_ln(o, w_attn, h, ln_g[l, 0][None, :], ln_b[l, 0][None, :])
            h, hb = _moe_layer(h, hb, moe_router[l // 2], moe_w_in, moe_w_out, l // 2,
                               ln_g[l, 1], ln_b[l, 1])
        if l == n_a - 1:
            kvproj = _shared_kv(hb, tabs, w_kv_shared)
    return h.reshape(bsz, seq, d)
```
